```python
import math
import jax, jax.numpy as jnp
from jax import lax
import numpy as np

D_MODEL = 1024
BATCH = 2
SEQ = 8192
DEPTH = 4
DEC_BATCH = 32
DEC_SEQ = 1
PAST_LEN = 8192
PAGE_SIZE = 128

HEAD_DIM = 64
GLA_HEADS = 4
GLA_DK = 32
GLA_DV = 64
GLA_RANK = 16
GLA_TAU = 16.0
GLA_CHUNK = 64
MOBA_HEADS = 6
MOBA_KV_HEADS = 2
MOBA_BLOCK = 256
MOBA_TOPK = 3
MOBA_Q_BLOCK = 32
SB_HEADS = 6
SB_KV_HEADS = 2
SB_Q_BLOCK = 128
REL_BUCKETS = 32
REL_MAX_DIST = 128
RMS_EPS = 1e-6
NEG_INF = -1e30
FF_RAW = -(-8 * D_MODEL // 3)
D_FF = -(-FF_RAW // 256) * 256
GLA_QK_W = GLA_HEADS * GLA_DK
GLA_V_W = GLA_HEADS * GLA_DV
MOBA_Q_W = MOBA_HEADS * HEAD_DIM
MOBA_KV_W = MOBA_KV_HEADS * HEAD_DIM
SB_Q_W = SB_HEADS * HEAD_DIM
SB_KV_W = SB_KV_HEADS * HEAD_DIM
MIX_W = GLA_V_W + MOBA_Q_W + SB_Q_W
IN_SIZES = (GLA_QK_W, GLA_QK_W, GLA_V_W, GLA_V_W, GLA_RANK, MOBA_Q_W, MOBA_KV_W, MOBA_KV_W, SB_Q_W, SB_KV_W, SB_KV_W)
IN_W = 2 * GLA_QK_W + 2 * GLA_V_W + GLA_RANK + MOBA_Q_W + 2 * MOBA_KV_W + SB_Q_W + 2 * SB_KV_W

kernel_name = 'hybrid_gla_moba_stickbreaking_step'


def rms_norm(x, g):
    xf = x.astype(jnp.float32)
    y = xf * lax.rsqrt(jnp.mean(xf * xf, axis=-1, keepdims=True) + RMS_EPS)
    return (y * g.astype(jnp.float32)).astype(x.dtype)


def t5_bucket(n):
    max_exact = REL_BUCKETS // 2
    nf = jnp.maximum(n, 1).astype(jnp.float32)
    large = max_exact + (jnp.log(nf / max_exact) / math.log(REL_MAX_DIST / max_exact)
                         * (REL_BUCKETS - max_exact)).astype(jnp.int32)
    return jnp.where(n < max_exact, n, jnp.minimum(large, REL_BUCKETS - 1))


def to_query_blocks(q, qb):
    B, T, H, D = q.shape
    nq = -(-T // qb)
    q = jnp.pad(q, ((0, 0), (0, nq * qb - T), (0, 0), (0, 0)))
    return q.reshape(B, nq, qb, H, D).transpose(1, 0, 3, 2, 4)


def from_query_blocks(o, T):
    nq, B, H, qb, D = o.shape
    return o.transpose(1, 0, 3, 2, 4).reshape(B, nq * qb, H, D)[:, :T]


def gla_recurrence(q, k, v, log_a, s0):
    B, T = q.shape[:2]
    C = min(GLA_CHUNK, T)
    n_chunks = -(-T // C)
    pad = n_chunks * C - T

    def prep(a):
        a = jnp.pad(a.astype(jnp.float32), ((0, 0), (0, pad), (0, 0), (0, 0)))
        return a.reshape(B, n_chunks, C, a.shape[2], a.shape[3]).transpose(1, 0, 3, 2, 4)

    qc, kc, vc, gc = prep(q), prep(k), prep(v), prep(log_a)
    causal = jnp.tril(jnp.ones((C, C), bool))

    def step(s, inp):
        qi, ki, vi, gi = inp
        b = jnp.cumsum(gi, axis=2)
        o_inter = jnp.einsum('bhck,bhkv->bhcv', qi * jnp.exp(b), s)
        diff = b[:, :, :, None, :] - b[:, :, None, :, :]
        decay = jnp.where(causal[:, :, None], jnp.exp(jnp.minimum(diff, 0.0)), 0.0)
        scores = jnp.einsum('bhtk,bhsk,bhtsk->bhts', qi, ki, decay)
        o_intra = jnp.einsum('bhts,bhsv->bhtv', scores, vi)
        b_last = b[:, :, -1:, :]
        s_new = jnp.exp(b_last[:, :, 0, :])[..., None] * s + jnp.einsum(
            'bhck,bhcv->bhkv', ki * jnp.exp(b_last - b), vi)
        return s_new, o_inter + o_intra

    s_fin, o = lax.scan(step, s0.astype(jnp.float32), (qc, kc, vc, gc))
    o = o.transpose(1, 0, 3, 2, 4).reshape(B, n_chunks * C, q.shape[2], v.shape[3])[:, :T]
    return o, s_fin


def moba_attention(q, k, v, q_pos0, rel_bias):
    B, Tq, H, D = q.shape
    Tk, KVH = k.shape[1], k.shape[2]
    G = H // KVH
    nb = -(-Tk // MOBA_BLOCK)
    padk = nb * MOBA_BLOCK - Tk

    def to_blocks(a):
        a = jnp.pad(a.astype(jnp.float32), ((0, 0), (0, padk), (0, 0), (0, 0)))
        return a.reshape(B, nb, MOBA_BLOCK, KVH, D).transpose(0, 3, 1, 2, 4)

    kb, vb = to_blocks(k), to_blocks(v)
    k_mean = jnp.mean(kb, axis=3)
    qb = min(MOBA_Q_BLOCK, Tq)
    qblk = to_query_blocks(q, qb)
    nq = qblk.shape[0]
    qblk = qblk.reshape(nq, B, KVH, G, qb, D)
    pos = (q_pos0 + jnp.arange(nq * qb)).reshape(nq, qb)
    n_sel = min(MOBA_TOPK, nb)
    head_bias = rel_bias.astype(jnp.float32).T.reshape(KVH, G, REL_BUCKETS)
    kv_ids = jnp.arange(KVH)[:, None, None, None, None]
    g_ids = jnp.arange(G)[:, None, None, None]
    scale = D ** -0.5

    def one_block(args):
        qi, p = args
        qf = qi.astype(jnp.float32)
        cur = p // MOBA_BLOCK
        gate = jnp.einsum('bngqd,bnjd->bngqj', qf, k_mean)
        gate = jnp.where(jnp.arange(nb)[None, :] < cur[:, None], gate, NEG_INF)
        _, sel = lax.top_k(gate, n_sel)
        own = jnp.broadcast_to(jnp.minimum(cur, nb - 1)[:, None], sel.shape[:-1] + (1,))
        idx = jnp.concatenate([sel, own], axis=-1)
        blk_ok = jnp.concatenate([sel < cur[:, None], jnp.ones(own.shape, bool)], axis=-1)
        flat = idx.reshape(B, KVH, -1, 1, 1)
        kg = jnp.take_along_axis(kb, flat, axis=2).reshape(idx.shape + (MOBA_BLOCK, D))
        vg = jnp.take_along_axis(vb, flat, axis=2).reshape(idx.shape + (MOBA_BLOCK, D))
        key_pos = idx[..., None] * MOBA_BLOCK + jnp.arange(MOBA_BLOCK)
        dist = p[:, None, None] - key_pos
        bias = head_bias[kv_ids, g_ids, t5_bucket(jnp.maximum(dist, 0))]
        logits = jnp.einsum('bngqd,bngqskd->bngqsk', qf, kg) * scale + bias
        logits = jnp.where(blk_ok[..., None] & (dist >= 0), logits, NEG_INF)
        w = jax.nn.softmax(logits.reshape(idx.shape[:-1] + (-1,)), axis=-1).reshape(logits.shape)
        o = jnp.einsum('bngqsk,bngqskd->bngqd', w, vg)
        return o.reshape(B, H, qb, D).astype(q.dtype)

    o = lax.map(one_block, (qblk, pos))
    return from_query_blocks(o, Tq)


def stick_breaking_attention(q, k, v, q_pos0):
    B, Tq, H, D = q.shape
    Tk, KVH = k.shape[1], k.shape[2]
    G = H // KVH
    kh = k.transpose(0, 2, 1, 3).astype(jnp.float32)
    vh = v.transpose(0, 2, 1, 3).astype(jnp.float32)
    qb = min(SB_Q_BLOCK, Tq)
    qblk = to_query_blocks(q, qb)
    nq = qblk.shape[0]
    qblk = qblk.reshape(nq, B, KVH, G, qb, D)
    pos = (q_pos0 + jnp.arange(nq * qb)).reshape(nq, qb)
    key_pos = jnp.arange(Tk)
    scale = D ** -0.5

    def one_block(args):
        qi, p = args
        z = jnp.einsum('bngqd,bnkd->bngqk', qi.astype(jnp.float32), kh) * scale
        mask = key_pos[None, :] < p[:, None]
        log_1mb = jnp.where(mask, jax.nn.log_sigmoid(-z), 0.0)
        rc = lax.cumsum(log_1mb, axis=4, reverse=True)
        suffix = jnp.concatenate([rc[..., 1:], jnp.zeros_like(rc[..., :1])], axis=-1)
        w = jnp.where(mask, jnp.exp(jax.nn.log_sigmoid(z) + suffix), 0.0)
        o = jnp.einsum('bngqk,bnkd->bngqd', w, vh)
        return o.reshape(B, H, qb, D).astype(q.dtype)

    o = lax.map(one_block, (qblk, pos))
    return from_query_blocks(o, Tq)


def decoder_layer(x, gla_s0, moba_past, sb_past, rel_bias, norm1, w_in, w_alpha, b_alpha, gla_norm,
                  moba_q_norm, moba_k_norm, sb_q_norm, sb_k_norm, w_out, norm2, w_gate_up, w_down):
    B, T, _ = x.shape
    pos0 = moba_past.shape[1]
    h = rms_norm(x, norm1)
    proj = h @ w_in
    split_points = np.cumsum(IN_SIZES)[:-1].tolist()
    gq, gk, gv, gg, ga, mq, mk, mv, sq, sk, sv = jnp.split(proj, split_points, axis=-1)

    log_a = jax.nn.log_sigmoid((ga @ w_alpha + b_alpha).astype(jnp.float32)) / GLA_TAU
    o_gla, s_new = gla_recurrence(
        gq.reshape(B, T, GLA_HEADS, GLA_DK) * GLA_DK ** -0.5,
        gk.reshape(B, T, GLA_HEADS, GLA_DK),
        gv.reshape(B, T, GLA_HEADS, GLA_DV),
        log_a.reshape(B, T, GLA_HEADS, GLA_DK), gla_s0)
    o_gla = rms_norm(o_gla, gla_norm) * jax.nn.silu(gg.reshape(B, T, GLA_HEADS, GLA_DV).astype(jnp.float32))
    o_gla = o_gla.reshape(B, T, GLA_V_W).astype(x.dtype)

    mq = rms_norm(mq.reshape(B, T, MOBA_HEADS, HEAD_DIM), moba_q_norm)
    mk = rms_norm(mk.reshape(B, T, MOBA_KV_HEADS, HEAD_DIM), moba_k_norm)
    moba_new = jnp.stack([mk, mv.reshape(B, T, MOBA_KV_HEADS, HEAD_DIM)], axis=2)
    moba_all = jnp.concatenate([moba_past.astype(x.dtype), moba_new], axis=1)
    o_moba = moba_attention(mq, moba_all[:, :, 0], moba_all[:, :, 1], pos0, rel_bias)

    sq = rms_norm(sq.reshape(B, T, SB_HEADS, HEAD_DIM), sb_q_norm)
    sk = rms_norm(sk.reshape(B, T, SB_KV_HEADS, HEAD_DIM), sb_k_norm)
    sb_new = jnp.stack([sk, sv.reshape(B, T, SB_KV_HEADS, HEAD_DIM)], axis=2)
    sb_all = jnp.concatenate([sb_past.astype(x.dtype), sb_new], axis=1)
    o_sb = stick_breaking_attention(sq, sb_all[:, :, 0], sb_all[:, :, 1], pos0)

    mix = jnp.concatenate([o_gla, o_moba.reshape(B, T, MOBA_Q_W), o_sb.reshape(B, T, SB_Q_W)], axis=-1)
    x = x + mix @ w_out

    h2 = rms_norm(x, norm2)
    gate, up = jnp.split(h2 @ w_gate_up, 2, axis=-1)
    x = x + (jax.nn.silu(gate) * up) @ w_down
    return x, s_new, moba_new, sb_new


def setup_inputs(seed: int = 0) -> dict:
    key = jax.random.key(seed)
    ks = jax.random.split(key, 20)
    nrm = jax.random.normal
    f32 = jnp.float32
    n_pages = PAST_LEN // PAGE_SIZE
    n_used = DEC_BATCH * n_pages
    n_pool = n_used + -(-n_used // 4)
    page_table = jax.random.permutation(ks[0], n_pool)[:n_used].reshape(DEC_BATCH, n_pages).astype(jnp.int32)

    def gain(k, n):
        return 1.0 + 0.02 * nrm(k, (DEPTH, n), f32)

    return {
        'x_prompt': nrm(ks[1], (BATCH, SEQ, D_MODEL), f32),
        'x_sample': nrm(ks[2], (DEC_BATCH, DEC_SEQ, D_MODEL), f32),
        'cache_moba_kv': nrm(ks[3], (DEPTH, n_pool, PAGE_SIZE, 2, MOBA_KV_HEADS, HEAD_DIM), f32),
        'cache_sb_kv': nrm(ks[4], (DEPTH, n_pool, PAGE_SIZE, 2, SB_KV_HEADS, HEAD_DIM), f32),
        'state_gla': 0.5 * nrm(ks[5], (DEPTH, DEC_BATCH, GLA_HEADS, GLA_DK, GLA_DV), f32),
        'page_table': page_table,
        'rel_bias': 0.5 * nrm(ks[6], (REL_BUCKETS, MOBA_HEADS), f32),
        'norm1': gain(ks[7], D_MODEL),
        'w_in': nrm(ks[8], (DEPTH, D_MODEL, IN_W), f32) * D_MODEL ** -0.5,
        'w_alpha': nrm(ks[9], (DEPTH, GLA_RANK, GLA_QK_W), f32) * GLA_RANK ** -0.5,
        'b_alpha': 0.1 * nrm(ks[10], (DEPTH, GLA_QK_W), f32),
        'gla_norm': gain(ks[11], GLA_DV),
        'moba_q_norm': gain(ks[12], HEAD_DIM),
        'moba_k_norm': gain(ks[13], HEAD_DIM),
        'sb_q_norm': gain(ks[14], HEAD_DIM),
        'sb_k_norm': gain(ks[15], HEAD_DIM),
        'w_out': nrm(ks[16], (DEPTH, MIX_W, D_MODEL), f32) * MIX_W ** -0.5,
        'norm2': gain(ks[17], D_MODEL),
        'w_gate_up': nrm(ks[18], (DEPTH, D_MODEL, 2 * D_FF), f32) * D_MODEL ** -0.5,
        'w_down': nrm(ks[19], (DEPTH, D_FF, D_MODEL), f32) * D_FF ** -0.5,
    }


def reference(x_prompt, x_sample, cache_moba_kv, cache_sb_kv, state_gla, page_table, rel_bias, norm1, w_in,
              w_alpha, b_alpha, gla_norm, moba_q_norm, moba_k_norm, sb_q_norm, sb_k_norm, w_out, norm2,
              w_gate_up, w_down):
    n_p = x_prompt.shape[0]
    n_s = x_sample.shape[0]
    past_len = page_table.shape[1] * cache_moba_kv.shape[2]
    yp, ys = x_prompt, x_sample
    p_moba, p_sb, p_gla, s_moba, s_sb, s_gla = [], [], [], [], [], []
    for l in range(DEPTH):
        w = (norm1[l], w_in[l], w_alpha[l], b_alpha[l], gla_norm[l], moba_q_norm[l], moba_k_norm[l],
             sb_q_norm[l], sb_k_norm[l], w_out[l], norm2[l], w_gate_up[l], w_down[l])
        empty_moba = jnp.zeros((n_p, 0, 2, MOBA_KV_HEADS, HEAD_DIM), x_prompt.dtype)
        empty_sb = jnp.zeros((n_p, 0, 2, SB_KV_HEADS, HEAD_DIM), x_prompt.dtype)
        s0 = jnp.zeros((n_p, GLA_HEADS, GLA_DK, GLA_DV), jnp.float32)
        yp, sp, mp, bp = decoder_layer(yp, s0, empty_moba, empty_sb, rel_bias, *w)
        moba_past = cache_moba_kv[l][page_table].reshape(n_s, past_len, 2, MOBA_KV_HEADS, HEAD_DIM)
        sb_past = cache_sb_kv[l][page_table].reshape(n_s, past_len, 2, SB_KV_HEADS, HEAD_DIM)
        ys, ss, ms, bs = decoder_layer(ys, state_gla[l], moba_past, sb_past, rel_bias, *w)
        p_moba.append(mp)
        p_sb.append(bp)
        p_gla.append(sp)
        s_moba.append(ms)
        s_sb.append(bs)
        s_gla.append(ss)
    prompt_moba_kv = jnp.stack(p_moba)
    prompt_sb_kv = jnp.stack(p_sb)
    prompt_gla_state = jnp.stack(p_gla)
    sample_moba_kv = jnp.stack(s_moba)
    sample_sb_kv = jnp.stack(s_sb)
    sample_gla_state = jnp.stack(s_gla)
    return (yp, ys, prompt_moba_kv, prompt_sb_kv, prompt_gla_state, sample_moba_kv, sample_sb_kv, sample_gla_state)
```

```python
import functools
import math

import jax
import jax.numpy as jnp
import numpy as np
from jax import lax
from jax.experimental import pallas as pl
from jax.experimental.pallas import tpu as pltpu

F32 = jnp.float32
BF16 = jnp.bfloat16

D_MODEL = 1024
DEPTH = 4
HEAD_DIM = 64
GLA_HEADS = 4
GLA_DK = 32
GLA_DV = 64
GLA_RANK = 16
GLA_TAU = 16.0
GLA_CHUNK = 64
ATT_HEADS = 6
KV_HEADS = 2
GROUP = ATT_HEADS // KV_HEADS
MOBA_BLOCK = 256
MOBA_TOPK = 3
REL_BUCKETS = 32
REL_MAX_DIST = 128
RMS_EPS = 1e-6
NEG_INF = -1e30
PAGE_SIZE = 128
D_FF = 2816
GLA_QK_W = GLA_HEADS * GLA_DK
GLA_V_W = GLA_HEADS * GLA_DV
ATT_Q_W = ATT_HEADS * HEAD_DIM
ATT_KV_W = 2 * KV_HEADS * HEAD_DIM
LANES = 128
ATT_TILE = 256
ATT_ROWS = ATT_HEADS * ATT_TILE
VMEM_LIMIT = 56 * 1024 * 1024
PAGES_PER_STEP = 16

_OFF = dict(gq=0, gk=128, gv=256, gg=512, ga=768, mq=784, mk=1168, mv=1296, sq=1424, sk=1808, sv=1936)
IN_W_PAD = 2176
_HEAD_PERM = np.concatenate([np.concatenate([np.arange(64) + 64 * j, np.arange(64) + 64 * (GROUP + j)])
                             for j in range(GROUP)])


def _t5_thresholds():
    n = np.arange(0, 4 * REL_MAX_DIST, dtype=np.int64)
    max_exact = REL_BUCKETS // 2
    nf = np.maximum(n, 1).astype(np.float32)
    large = max_exact + (np.log(nf / np.float32(max_exact)) / np.float32(math.log(REL_MAX_DIST / max_exact))
                         * np.float32(REL_BUCKETS - max_exact)).astype(np.int32)
    bucket = np.where(n < max_exact, n, np.minimum(large, REL_BUCKETS - 1))
    return [int(np.argmax(bucket >= b)) for b in range(REL_BUCKETS)]


_T5_THR = _t5_thresholds()


def _dot(a, b):
    return jnp.dot(a, b, preferred_element_type=F32)


def _dot_nt(a, b):
    return lax.dot_general(a, b, (((1,), (1,)), ((), ())), preferred_element_type=F32)


def _split_hilo(a):
    hi = a.astype(BF16)
    lo = (a - hi.astype(F32)).astype(BF16)
    return hi, lo


def _dot_hilo(a, b_bf16):
    hi, lo = _split_hilo(a)
    return _dot(hi, b_bf16) + _dot(lo, b_bf16)


def _group_mean_matrix(width, group):
    r = lax.broadcasted_iota(jnp.int32, (width, width), 0) // group
    c = lax.broadcasted_iota(jnp.int32, (width, width), 1) // group
    return jnp.where(r == c, 1.0 / group, 0.0).astype(BF16)


def _neg_softplus(z):
    return -(jnp.maximum(z, 0.0) + jnp.log(1.0 + jnp.exp(-jnp.abs(z))))


def _log_sigmoid(x):
    return jnp.minimum(x, 0.0) - jnp.log(1.0 + jnp.exp(-jnp.abs(x)))


def _silu(x):
    return x / (1.0 + jnp.exp(-x))


def _head_rms(x, gain, group):
    ms = _dot_hilo(x * x, _group_mean_matrix(x.shape[1], group))
    return x * lax.rsqrt(ms + RMS_EPS) * gain


def _inproj_kernel(x_ref, g1_ref, w_ref, wa_ref, ba_ref, mqg_ref, mkg_ref, sqg_ref, skg_ref,
                   gq_ref, gk_ref, gl_ref, gv_ref, gg_ref,
                   mq_ref, mkv32_ref, mkv16_ref, mkm_ref, sq_ref, skv32_ref, skv16_ref, *, nblk, kv_transposed):
    x = x_ref[...]
    ms = jnp.mean(x * x, axis=-1, keepdims=True)
    h = (x * lax.rsqrt(ms + RMS_EPS) * g1_ref[...]).astype(BF16)
    p = _dot(h, w_ref[...])
    gq_ref[...] = p[:, 0:128] * (GLA_DK ** -0.5)
    gk_ref[...] = p[:, 128:256]
    gv_ref[...] = p[:, 256:512]
    gg_ref[...] = p[:, 512:768]
    alpha = _dot(p[:, 768:896].astype(BF16), wa_ref[...]) + ba_ref[...]
    gl_ref[...] = _log_sigmoid(alpha) / GLA_TAU

    def attn_group(base, qg_ref, kg_ref, q_ref, kv32_ref, kv16_ref):
        q = _head_rms(p[:, base:base + ATT_Q_W], qg_ref[...], HEAD_DIM) * (HEAD_DIM ** -0.5)
        q_ref[...] = q.astype(BF16)
        k = _head_rms(p[:, base + 384:base + 512], kg_ref[...], HEAD_DIM)
        v = p[:, base + 512:base + 640]
        if kv_transposed:
            kv32_ref[0, 0:128, :] = k.T
            kv32_ref[0, 128:256, :] = v.T
        else:
            kv32_ref[:, 0:128] = k
            kv32_ref[:, 128:256] = v
        kv16_ref[:, 0:128] = k.astype(BF16)
        kv16_ref[:, 128:256] = v.astype(BF16)
        return k

    mk = attn_group(896, mqg_ref, mkg_ref, mq_ref, mkv32_ref, mkv16_ref)
    attn_group(1536, sqg_ref, skg_ref, sq_ref, skv32_ref, skv16_ref)
    if nblk == 0:
        mkm_ref[...] = jnp.zeros(mkm_ref.shape, F32)
    for j in range(nblk):
        mkm_ref[0, j:j + 1, :] = jnp.mean(mk[j * MOBA_BLOCK:(j + 1) * MOBA_BLOCK], axis=0, keepdims=True)


def _inproj(x, lw, tm, seq=None):
    n = x.shape[0]
    grid = n // tm
    nblk = tm // MOBA_BLOCK
    nblk_alloc = max(nblk, 1)
    row = lambda w: pl.BlockSpec((tm, w), lambda i: (i, 0))
    full = lambda a: pl.BlockSpec(a.shape, lambda i: (0,) * a.ndim)
    if seq is None:
        kv32_shape, kv32_spec = jax.ShapeDtypeStruct((n, ATT_KV_W), F32), row(ATT_KV_W)
    else:
        tiles = seq // tm
        kv32_shape = jax.ShapeDtypeStruct((n // seq, ATT_KV_W, seq), F32)
        kv32_spec = pl.BlockSpec((1, ATT_KV_W, tm), lambda i: (i // tiles, 0, i % tiles))
    outs = [
        jax.ShapeDtypeStruct((n, 128), F32), jax.ShapeDtypeStruct((n, 128), F32),
        jax.ShapeDtypeStruct((n, 128), F32), jax.ShapeDtypeStruct((n, 256), F32),
        jax.ShapeDtypeStruct((n, 256), F32),
        jax.ShapeDtypeStruct((n, ATT_Q_W), BF16), kv32_shape,
        jax.ShapeDtypeStruct((n, ATT_KV_W), BF16), jax.ShapeDtypeStruct((grid, nblk_alloc, 128), F32),
        jax.ShapeDtypeStruct((n, ATT_Q_W), BF16), kv32_shape,
        jax.ShapeDtypeStruct((n, ATT_KV_W), BF16),
    ]
    out_specs = [row(128), row(128), row(128), row(256), row(256),
                 row(ATT_Q_W), kv32_spec, row(ATT_KV_W),
                 pl.BlockSpec((1, nblk_alloc, 128), lambda i: (i, 0, 0)),
                 row(ATT_Q_W), kv32_spec, row(ATT_KV_W)]
    ins = [x, lw['g1'], lw['w_in'], lw['w_alpha'], lw['b_alpha'], lw['mqg'], lw['mkg'], lw['sqg'], lw['skg']]
    in_specs = [row(D_MODEL)] + [full(a) for a in ins[1:]]
    return pl.pallas_call(
        functools.partial(_inproj_kernel, nblk=nblk, kv_transposed=seq is not None), grid=(grid,),
        in_specs=in_specs, out_specs=out_specs, out_shape=outs,
        compiler_params=pltpu.CompilerParams(dimension_semantics=("arbitrary",), vmem_limit_bytes=VMEM_LIMIT),
        name=f"inproj_{tm}")(*ins)


def _gla_kernel(q_ref, k_ref, gl_ref, v_ref, gg_ref, gn_ref, s0_ref, o_ref, sT_ref,
                st_scr, b_scr, p_scr, w_scr, *, chunk, n_chunks):
    c = chunk
    step = pl.program_id(1)

    @pl.when(step == 0)
    def _():
        st_scr[...] = s0_ref[0]

    ri = lax.broadcasted_iota(jnp.int32, (c, c), 0)
    ci = lax.broadcasted_iota(jnp.int32, (c, c), 1)
    ltri = jnp.where(ri >= ci, 1.0, 0.0).astype(BF16)
    kh = lax.broadcasted_iota(jnp.int32, (GLA_QK_W, GLA_V_W), 0) // GLA_DK
    vh = lax.broadcasted_iota(jnp.int32, (GLA_QK_W, GLA_V_W), 1) // GLA_DV
    head_ones = jnp.where(kh == vh, 1.0, 0.0).astype(BF16)
    vh2 = lax.broadcasted_iota(jnp.int32, (GLA_V_W, GLA_QK_W), 0) // GLA_DV
    kh2 = lax.broadcasted_iota(jnp.int32, (GLA_V_W, GLA_QK_W), 1) // GLA_DK
    bd_mask = vh2 == kh2
    trow = lax.broadcasted_iota(jnp.int32, (c, GLA_QK_W), 0)

    def one_chunk(ic, carry):
        r0 = pl.multiple_of(ic * c, c)
        q = q_ref[pl.ds(r0, c), :]
        k = k_ref[pl.ds(r0, c), :]
        g = gl_ref[pl.ds(r0, c), :]
        v = v_ref[pl.ds(r0, c), :]
        g_hi, g_lo = _split_hilo(g)
        b = _dot(ltri, g_hi) + _dot(ltri, g_lo)
        b_scr[...] = b
        st = st_scr[...]
        o_inter = _dot_nt((q * jnp.exp(b)).astype(BF16), st.astype(BF16))

        def slab(s, carry2):
            bs = b_scr[pl.ds(s, 1), :]
            ks = k_ref[pl.ds(r0 + s, 1), :]
            sl = q * jnp.exp(jnp.minimum(b - bs, 0.0)) * ks
            off = pl.multiple_of(s * c, c)
            p_scr[pl.ds(off, c), :] = jnp.where(trow >= s, sl, 0.0)
            return carry2
        lax.fori_loop(0, c, slab, 0)
        w_scr[...] = _dot_hilo(p_scr[...], head_ones)

        def acc_s(s, acc):
            off = pl.multiple_of(s * c, c)
            return acc + w_scr[pl.ds(off, c), :] * v_ref[pl.ds(r0 + s, 1), :]
        o_intra = lax.fori_loop(0, c, acc_s, jnp.zeros((c, GLA_V_W), F32))

        b_last = b_scr[c - 1:c, :]
        kt = (k * jnp.exp(b_last - b)).astype(BF16)
        upd = _dot(v.T.astype(BF16), kt)
        st_scr[...] = st * jnp.exp(b_last) + jnp.where(bd_mask, upd, 0.0)

        o = o_inter + o_intra
        on = _head_rms(o, gn_ref[...], GLA_DV)
        o_ref[pl.ds(r0, c), :] = (on * _silu(gg_ref[pl.ds(r0, c), :])).astype(o_ref.dtype)
        return carry

    lax.fori_loop(0, n_chunks, one_chunk, 0)

    @pl.when(step == pl.num_programs(1) - 1)
    def _():
        sT_ref[0] = st_scr[...]


def _gla(gq, gk, gl, gv, gg, gnorm, s0T, *, batch, seq, chunk, chunks_per_step):
    rows = chunk * chunks_per_step
    steps = seq // rows
    row = lambda w: pl.BlockSpec((rows, w), lambda b, i: (b * steps + i, 0))
    kern = functools.partial(_gla_kernel, chunk=chunk, n_chunks=chunks_per_step)
    return pl.pallas_call(
        kern, grid=(batch, steps),
        in_specs=[row(128), row(128), row(128), row(256), row(256),
                  pl.BlockSpec((1, GLA_V_W), lambda b, i: (0, 0)),
                  pl.BlockSpec((1, GLA_V_W, GLA_QK_W), lambda b, i: (b, 0, 0))],
        out_specs=[row(256), pl.BlockSpec((1, GLA_V_W, GLA_QK_W), lambda b, i: (b, 0, 0))],
        out_shape=[jax.ShapeDtypeStruct((batch * seq, GLA_V_W), BF16),
                   jax.ShapeDtypeStruct((batch, GLA_V_W, GLA_QK_W), F32)],
        scratch_shapes=[pltpu.VMEM((GLA_V_W, GLA_QK_W), F32), pltpu.VMEM((chunk, GLA_QK_W), F32),
                        pltpu.VMEM((chunk * chunk, GLA_QK_W), F32),
                        pltpu.VMEM((chunk * chunk, GLA_V_W), F32)],
        compiler_params=pltpu.CompilerParams(dimension_semantics=("arbitrary", "arbitrary"),
                                             vmem_limit_bytes=VMEM_LIMIT),
        name="gla_prompt")(gq, gk, gl, gv, gg, gnorm, s0T)


def _gla_sample_kernel(q_ref, k_ref, gl_ref, v_ref, gg_ref, gn_ref, s0_ref, o_ref, sT_ref):
    rows = (8, GLA_QK_W)
    decay = jnp.exp(gl_ref[0])
    r = lax.broadcasted_iota(jnp.int32, (GLA_V_W, GLA_V_W), 0)
    cc = lax.broadcasted_iota(jnp.int32, (GLA_V_W, GLA_V_W), 1)
    v_diag = jnp.where(r == cc, jnp.broadcast_to(v_ref[0], (GLA_V_W, GLA_V_W)), 0.0).astype(BF16)
    k_rows = jnp.broadcast_to(k_ref[0], (GLA_V_W, GLA_QK_W)).astype(BF16)
    vh = lax.broadcasted_iota(jnp.int32, (GLA_V_W, GLA_QK_W), 0) // GLA_DV
    kh = lax.broadcasted_iota(jnp.int32, (GLA_V_W, GLA_QK_W), 1) // GLA_DK
    outer = jnp.where(vh == kh, _dot(v_diag, k_rows), 0.0)
    st = s0_ref[0] * decay + outer
    sT_ref[0] = st
    q8 = jnp.broadcast_to(q_ref[0], rows).astype(BF16)
    o = _dot_nt(q8, st.astype(BF16))
    on = _head_rms(o, gn_ref[...], GLA_DV)
    o_ref[0] = (on * _silu(gg_ref[0]))[0:1].astype(o_ref.dtype)


def _gla_sample(gq, gk, gl, gv, gg, gnorm, s0T):
    n = gq.shape[0]
    r3 = lambda a: a.reshape(n, 1, a.shape[-1])
    row = lambda w: pl.BlockSpec((1, 1, w), lambda b: (b, 0, 0))
    st_spec = pl.BlockSpec((1, GLA_V_W, GLA_QK_W), lambda b: (b, 0, 0))
    o, st = pl.pallas_call(
        _gla_sample_kernel, grid=(n,),
        in_specs=[row(128), row(128), row(128), row(256), row(256),
                  pl.BlockSpec((1, GLA_V_W), lambda b: (0, 0)), st_spec],
        out_specs=[row(256), st_spec],
        out_shape=[jax.ShapeDtypeStruct((n, 1, GLA_V_W), BF16), jax.ShapeDtypeStruct((n, GLA_V_W, GLA_QK_W), F32)],
        compiler_params=pltpu.CompilerParams(dimension_semantics=("arbitrary",), vmem_limit_bytes=VMEM_LIMIT),
        name="gla_sample")(r3(gq), r3(gk), r3(gl), r3(gv), r3(gg), gnorm, s0T)
    return o.reshape(n, GLA_V_W), st


def _stack_heads(q_cols):
    lane = lax.broadcasted_iota(jnp.int32, q_cols[0].shape, 1)
    low = lane < HEAD_DIM
    zero = jnp.zeros_like(q_cols[0])
    parts = [jnp.where(low, qc, zero) for qc in q_cols] + [jnp.where(low, zero, qc) for qc in q_cols]
    return jnp.concatenate(parts, axis=0)


def _unstack_heads(o, rows):
    lane = lax.broadcasted_iota(jnp.int32, (rows, LANES), 1)
    low = lane < HEAD_DIM
    return [jnp.where(low, o[j * rows:(j + 1) * rows], o[(GROUP + j) * rows:(GROUP + j + 1) * rows])
            for j in range(GROUP)]


def _suffix_matrix():
    r = lax.broadcasted_iota(jnp.int32, (ATT_TILE, ATT_TILE), 0)
    c = lax.broadcasted_iota(jnp.int32, (ATT_TILE, ATT_TILE), 1)
    u = jnp.where(r > c, 1.0, 0.0).astype(BF16)
    return jnp.concatenate([u, u], axis=0)


def _sb_tile(q, k, v, u2, carry, acc, mask, feature_major=False):
    z = _dot(q, k) if feature_major else _dot_nt(q, k)
    l = _neg_softplus(z)
    if mask is not None:
        l = jnp.where(mask, l, 0.0)
    hi, lo = _split_hilo(l)
    c = _dot(jnp.concatenate([hi, lo], axis=1), u2)
    w = jnp.exp(z + l + c + carry)
    if mask is not None:
        w = jnp.where(mask, w, 0.0)
    acc = acc + (_dot_nt(w.astype(BF16), v) if feature_major else _dot(w.astype(BF16), v))
    carry = carry + jnp.sum(l, axis=1, keepdims=True)
    return carry, acc


def _sb_prompt_kernel(q_ref, kv_ref, o_ref, carry_scr, acc_scr):
    i = pl.program_id(1)
    t = ATT_TILE
    q = _stack_heads([q_ref[0, :, 128 * j:128 * (j + 1)] for j in range(GROUP)])
    u2 = _suffix_matrix()
    r = lax.broadcasted_iota(jnp.int32, (ATT_ROWS, t), 0) & (t - 1)
    c = lax.broadcasted_iota(jnp.int32, (ATT_ROWS, t), 1)
    causal = c < r

    def tile(kj, mask):
        r0 = pl.multiple_of(kj * t, t)
        k = kv_ref[0, pl.ds(r0, t), 0:128]
        v = kv_ref[0, pl.ds(r0, t), 128:256]
        carry, acc = _sb_tile(q, k, v, u2, carry_scr[:, 0:1], acc_scr[...], mask)
        carry_scr[...] = jnp.broadcast_to(carry, carry_scr.shape)
        acc_scr[...] = acc

    carry_scr[...] = jnp.zeros(carry_scr.shape, F32)
    acc_scr[...] = jnp.zeros(acc_scr.shape, F32)
    tile(i, causal)

    def body(it, carry):
        tile(i - 1 - it, None)
        return carry
    lax.fori_loop(0, i, body, 0)

    cols = _unstack_heads(acc_scr[...], t)
    for j in range(GROUP):
        o_ref[0, :, 128 * j:128 * (j + 1)] = cols[j].astype(o_ref.dtype)


def _sb_prompt(q, kv, *, batch, seq):
    t = ATT_TILE
    q3 = q.reshape(batch, seq, ATT_Q_W)
    kv3 = kv.reshape(batch, seq, ATT_KV_W)
    out = pl.pallas_call(
        _sb_prompt_kernel, grid=(batch, seq // t),
        in_specs=[pl.BlockSpec((1, t, ATT_Q_W), lambda b, i: (b, i, 0)),
                  pl.BlockSpec((1, seq, ATT_KV_W), lambda b, i: (b, 0, 0))],
        out_specs=pl.BlockSpec((1, t, ATT_Q_W), lambda b, i: (b, i, 0)),
        out_shape=jax.ShapeDtypeStruct((batch, seq, ATT_Q_W), BF16),
        scratch_shapes=[pltpu.VMEM((ATT_ROWS, LANES), F32), pltpu.VMEM((ATT_ROWS, LANES), F32)],
        compiler_params=pltpu.CompilerParams(dimension_semantics=("arbitrary", "arbitrary"),
                                             vmem_limit_bytes=VMEM_LIMIT),
        name="sb_prompt")(q3, kv3)
    return out.reshape(batch * seq, ATT_Q_W)


def _bias_of_dist(dist, rb_ref, h):
    bias = jnp.full(dist.shape, rb_ref[0, h], F32)
    for b in range(1, REL_BUCKETS):
        bias = jnp.where(dist >= _T5_THR[b], rb_ref[b, h], bias)
    return bias - rb_ref[REL_BUCKETS - 1, h]


def _bias_kernel(rb_ref, own_ref, prev_ref, samp_ref):
    t = ATT_TILE
    r = lax.broadcasted_iota(jnp.int32, (t, t), 0)
    c = lax.broadcasted_iota(jnp.int32, (t, t), 1)
    for h in range(ATT_HEADS):
        own = _bias_of_dist(jnp.maximum(r - c, 0), rb_ref, h)
        own_ref[h * t:(h + 1) * t, :] = jnp.where(c <= r, own, NEG_INF)
        prev_ref[h * t:(h + 1) * t, :] = _bias_of_dist(r - c + t, rb_ref, h)
    s = lax.broadcasted_iota(jnp.int32, (8, t), 1)
    hrow = lax.broadcasted_iota(jnp.int32, (8, t), 0)
    last = jnp.zeros((8, t), F32)
    self_b = jnp.zeros((8, LANES), F32)
    hrow2 = lax.broadcasted_iota(jnp.int32, (8, LANES), 0)
    for h in range(ATT_HEADS):
        last = jnp.where(hrow == h, _bias_of_dist(t - s, rb_ref, h), last)
        self_b = jnp.where(hrow2 == h, rb_ref[0, h] - rb_ref[REL_BUCKETS - 1, h], self_b)
    samp_ref[:, 0:t] = last
    samp_ref[:, t:t + LANES] = self_b


def _bias_tiles(rel_bias):
    t = ATT_TILE
    return pl.pallas_call(
        _bias_kernel,
        in_specs=[pl.BlockSpec(memory_space=pltpu.SMEM)],
        out_shape=[jax.ShapeDtypeStruct((ATT_ROWS, t), F32), jax.ShapeDtypeStruct((ATT_ROWS, t), F32),
                   jax.ShapeDtypeStruct((8, t + LANES), F32)],
        name="t5_bias_tiles")(rel_bias)


def _top3_select(gate, n_valid):
    lane_i = lax.broadcasted_iota(jnp.int32, gate.shape, 1)
    lane = lane_i.astype(F32)
    valid = lane_i < n_valid
    g = jnp.where(valid, gate, NEG_INF)
    sel = jnp.zeros(gate.shape, F32)
    for _ in range(MOBA_TOPK):
        m = jnp.max(g, axis=1, keepdims=True)
        idx = jnp.min(jnp.where(g == m, lane, float(LANES)), axis=1, keepdims=True)
        pick = lane == idx
        sel = jnp.where(pick, 1.0, sel)
        g = jnp.where(pick, -jnp.inf, g)
    return jnp.where(valid, sel, 0.0) > 0.5


def _moba_prompt_kernel(q_ref, kv_ref, km_ref, own_ref, prev_ref, o_ref, m_scr, l_scr, acc_scr):
    i = pl.program_id(1)
    t = ATT_TILE
    q = _stack_heads([q_ref[0, :, 128 * j:128 * (j + 1)] for j in range(GROUP)])
    gate = _dot_nt(q, km_ref[0].astype(BF16))
    sel = _top3_select(gate, i)
    qx = jnp.concatenate([q, jnp.where(sel, 0.0, NEG_INF).astype(BF16)], axis=1)
    lane_blk = lax.broadcasted_iota(jnp.int32, (t, LANES), 1)

    def online(s, v):
        m_old = m_scr[:, 0:1]
        m_new = jnp.maximum(m_old, jnp.max(s, axis=1, keepdims=True))
        alpha = jnp.exp(m_old - m_new)
        p = jnp.exp(s - m_new)
        l_scr[...] = jnp.broadcast_to(alpha * l_scr[:, 0:1] + jnp.sum(p, axis=1, keepdims=True), l_scr.shape)
        acc_scr[...] = alpha * acc_scr[...] + _dot(p.astype(BF16), v)
        m_scr[...] = jnp.broadcast_to(m_new, m_scr.shape)

    def kv_tile(kj):
        r0 = pl.multiple_of(kj * t, t)
        return kv_ref[0, pl.ds(r0, t), 0:128], kv_ref[0, pl.ds(r0, t), 128:256]

    def past_logits(kj, k):
        kx = jnp.concatenate([k, jnp.where(lane_blk == kj, 1.0, 0.0).astype(BF16)], axis=1)
        return _dot_nt(qx, kx)

    k, v = kv_tile(i)
    s = _dot_nt(q, k) + own_ref[...]
    m0 = jnp.max(s, axis=1, keepdims=True)
    p = jnp.exp(s - m0)
    m_scr[...] = jnp.broadcast_to(m0, m_scr.shape)
    l_scr[...] = jnp.broadcast_to(jnp.sum(p, axis=1, keepdims=True), l_scr.shape)
    acc_scr[...] = _dot(p.astype(BF16), v)

    @pl.when(i >= 1)
    def _():
        k, v = kv_tile(i - 1)
        online(past_logits(i - 1, k) + prev_ref[...], v)

    def body(kj, carry):
        k, v = kv_tile(kj)
        online(past_logits(kj, k), v)
        return carry
    lax.fori_loop(0, jnp.maximum(i - 1, 0), body, 0)

    o = acc_scr[...] / l_scr[:, 0:1]
    cols = _unstack_heads(o, t)
    for j in range(GROUP):
        o_ref[0, :, 128 * j:128 * (j + 1)] = cols[j].astype(o_ref.dtype)


def _moba_prompt(q, kv, kmean, bias_own, bias_prev, *, batch, seq):
    t = ATT_TILE
    q3 = q.reshape(batch, seq, ATT_Q_W)
    kv3 = kv.reshape(batch, seq, ATT_KV_W)
    const = lambda a: pl.BlockSpec(a.shape, lambda b, i: (0, 0))
    out = pl.pallas_call(
        _moba_prompt_kernel, grid=(batch, seq // t),
        in_specs=[pl.BlockSpec((1, t, ATT_Q_W), lambda b, i: (b, i, 0)),
                  pl.BlockSpec((1, seq, ATT_KV_W), lambda b, i: (b, 0, 0)),
                  pl.BlockSpec((1, LANES, LANES), lambda b, i: (b, 0, 0)),
                  const(bias_own), const(bias_prev)],
        out_specs=pl.BlockSpec((1, t, ATT_Q_W), lambda b, i: (b, i, 0)),
        out_shape=jax.ShapeDtypeStruct((batch, seq, ATT_Q_W), BF16),
        scratch_shapes=[pltpu.VMEM((ATT_ROWS, LANES), F32)] * 3,
        compiler_params=pltpu.CompilerParams(dimension_semantics=("arbitrary", "arbitrary"),
                                             vmem_limit_bytes=VMEM_LIMIT),
        name="moba_prompt")(q3, kv3, kmean, bias_own, bias_prev)
    return out.reshape(batch * seq, ATT_Q_W)


def _sample_q_rows(q_row):
    row = lax.broadcasted_iota(jnp.int32, (8, LANES), 0)
    lane = lax.broadcasted_iota(jnp.int32, (8, LANES), 1)
    qf = q_row.astype(F32)
    out = jnp.zeros((8, LANES), F32)
    for h in range(ATT_HEADS):
        j, n = h % GROUP, h // GROUP
        col = jnp.broadcast_to(qf[:, 128 * j:128 * (j + 1)], (8, LANES))
        half = lane >= HEAD_DIM if n == 1 else lane < HEAD_DIM
        out = jnp.where(jnp.logical_and(row == h, half), col, out)
    return out.astype(BF16)


def _sample_o_row(o):
    lane = lax.broadcasted_iota(jnp.int32, (1, LANES), 1)
    return [jnp.where(lane < HEAD_DIM, o[j:j + 1], o[GROUP + j:GROUP + j + 1]) for j in range(GROUP)]


def _page_pair(p0_ref, p1_ref):
    k = jnp.concatenate([p0_ref[0, 0:128, :], p1_ref[0, 0:128, :]], axis=1).astype(BF16)
    v = jnp.concatenate([p0_ref[0, 128:256, :], p1_ref[0, 128:256, :]], axis=1).astype(BF16)
    return k, v


def _sb_sample_kernel(pt_ref, q_ref, *refs):
    del pt_ref
    pages = refs[:PAGES_PER_STEP]
    o_ref, carry_scr, acc_scr = refs[PAGES_PER_STEP:]
    step = pl.program_id(1)

    @pl.when(step == 0)
    def _():
        carry_scr[...] = jnp.zeros(carry_scr.shape, F32)
        acc_scr[...] = jnp.zeros(acc_scr.shape, F32)

    q = _sample_q_rows(q_ref[0])
    u2 = _suffix_matrix()
    carry, acc = carry_scr[:, 0:1], acc_scr[...]
    for tl in reversed(range(PAGES_PER_STEP // 2)):
        k, v = _page_pair(pages[2 * tl], pages[2 * tl + 1])
        carry, acc = _sb_tile(q, k, v, u2, carry, acc, None, feature_major=True)
    carry_scr[...] = jnp.broadcast_to(carry, carry_scr.shape)
    acc_scr[...] = acc

    @pl.when(step == pl.num_programs(1) - 1)
    def _():
        cols = _sample_o_row(acc)
        for j in range(GROUP):
            o_ref[0, :, 128 * j:128 * (j + 1)] = cols[j].astype(o_ref.dtype)


def _page_specs(layer, n_pool, n_steps, reverse):
    specs = []
    for p in range(PAGES_PER_STEP):
        def imap(b, c, pt, p=p):
            cc = (n_steps - 1 - c) if reverse else c
            return (layer * n_pool + pt[b, cc * PAGES_PER_STEP + p], 0, 0)
        specs.append(pl.BlockSpec((1, ATT_KV_W, PAGE_SIZE), imap))
    return specs


def _cache_pages(cache):
    d, n_pool = cache.shape[:2]
    return cache.transpose(0, 1, 3, 4, 5, 2).reshape(d * n_pool, ATT_KV_W, PAGE_SIZE)


def _sb_sample(q, cache, page_table, layer):
    nseq, n_pages = page_table.shape
    n_pool = cache.shape[1]
    n_steps = n_pages // PAGES_PER_STEP
    cache2 = _cache_pages(cache)
    grid_spec = pltpu.PrefetchScalarGridSpec(
        num_scalar_prefetch=1, grid=(nseq, n_steps),
        in_specs=[pl.BlockSpec((1, 1, ATT_Q_W), lambda b, c, pt: (b, 0, 0))]
        + _page_specs(layer, n_pool, n_steps, reverse=True),
        out_specs=pl.BlockSpec((1, 1, ATT_Q_W), lambda b, c, pt: (b, 0, 0)),
        scratch_shapes=[pltpu.VMEM((8, LANES), F32), pltpu.VMEM((8, LANES), F32)])
    out = pl.pallas_call(
        _sb_sample_kernel, grid_spec=grid_spec,
        out_shape=jax.ShapeDtypeStruct((nseq, 1, ATT_Q_W), BF16),
        compiler_params=pltpu.CompilerParams(dimension_semantics=("arbitrary", "arbitrary"),
                                             vmem_limit_bytes=VMEM_LIMIT),
        name="sb_sample")(page_table, q.reshape(nseq, 1, ATT_Q_W), *([cache2] * PAGES_PER_STEP))
    return out.reshape(nseq, ATT_Q_W)


def _moba_sample_kernel(pt_ref, q_ref, kvn_ref, sb_ref, *refs, n_blocks):
    del pt_ref
    pages = refs[:PAGES_PER_STEP]
    o_ref, gate_scr, m_scr, l_scr, acc_scr = refs[PAGES_PER_STEP:]
    step = pl.program_id(1)
    bps = PAGES_PER_STEP // 2
    lane = lax.broadcasted_iota(jnp.int32, (8, LANES), 1)

    @pl.when(step == 0)
    def _():
        gate_scr[...] = jnp.zeros(gate_scr.shape, F32)
        m_scr[...] = jnp.zeros(m_scr.shape, F32)
        l_scr[...] = jnp.zeros(l_scr.shape, F32)

    q = _sample_q_rows(q_ref[0])
    qf = q.astype(F32)
    for tl in range(bps):
        blk = step * bps + tl
        k, v = _page_pair(pages[2 * tl], pages[2 * tl + 1])
        s = _dot(q, k)
        g = jnp.mean(s, axis=1, keepdims=True)
        s = s + jnp.where(blk == n_blocks - 1, 1.0, 0.0) * sb_ref[:, 0:ATT_TILE]
        m = jnp.max(s, axis=1, keepdims=True)
        p = jnp.exp(s - m)
        hit = lane == blk
        gate_scr[...] = jnp.where(hit, g, gate_scr[...])
        m_scr[...] = jnp.where(hit, m, m_scr[...])
        l_scr[...] = jnp.where(hit, jnp.sum(p, axis=1, keepdims=True), l_scr[...])
        acc_scr[blk] = _dot_nt(p.astype(BF16), v)

    @pl.when(step == pl.num_programs(1) - 1)
    def _():
        sel = _top3_select(gate_scr[...], n_blocks)
        kvn = kvn_ref[0]
        s_self = jnp.sum(qf * kvn[:, 0:128].astype(F32), axis=1, keepdims=True) + sb_ref[:, ATT_TILE:ATT_TILE + 1]
        m_all = m_scr[...]
        m_tot = jnp.maximum(jnp.max(jnp.where(sel, m_all, NEG_INF), axis=1, keepdims=True), s_self)
        coef = jnp.where(sel, jnp.exp(m_all - m_tot), 0.0)
        p_self = jnp.exp(s_self - m_tot)
        denom = jnp.sum(coef * l_scr[...], axis=1, keepdims=True) + p_self
        o = p_self * kvn[:, 128:256].astype(F32)
        for j in range(n_blocks):
            o = o + coef[:, j:j + 1] * acc_scr[j]
        cols = _sample_o_row(o / denom)
        for j in range(GROUP):
            o_ref[0, :, 128 * j:128 * (j + 1)] = cols[j].astype(o_ref.dtype)


def _moba_sample(q, kv_new, cache, page_table, bias_samp, layer):
    nseq, n_pages = page_table.shape
    n_pool = cache.shape[1]
    n_steps = n_pages // PAGES_PER_STEP
    n_blocks = n_pages * PAGE_SIZE // MOBA_BLOCK
    cache2 = _cache_pages(cache)
    grid_spec = pltpu.PrefetchScalarGridSpec(
        num_scalar_prefetch=1, grid=(nseq, n_steps),
        in_specs=[pl.BlockSpec((1, 1, ATT_Q_W), lambda b, c, pt: (b, 0, 0)),
                  pl.BlockSpec((1, 1, ATT_KV_W), lambda b, c, pt: (b, 0, 0)),
                  pl.BlockSpec(bias_samp.shape, lambda b, c, pt: (0, 0))]
        + _page_specs(layer, n_pool, n_steps, reverse=False),
        out_specs=pl.BlockSpec((1, 1, ATT_Q_W), lambda b, c, pt: (b, 0, 0)),
        scratch_shapes=[pltpu.VMEM((8, LANES), F32)] * 3 + [pltpu.VMEM((n_blocks, 8, LANES), F32)])
    out = pl.pallas_call(
        functools.partial(_moba_sample_kernel, n_blocks=n_blocks), grid_spec=grid_spec,
        out_shape=jax.ShapeDtypeStruct((nseq, 1, ATT_Q_W), BF16),
        compiler_params=pltpu.CompilerParams(dimension_semantics=("arbitrary", "arbitrary"),
                                             vmem_limit_bytes=VMEM_LIMIT),
        name="moba_sample")(page_table, q.reshape(nseq, 1, ATT_Q_W), kv_new.reshape(nseq, 1, ATT_KV_W),
                            bias_samp, *([cache2] * PAGES_PER_STEP))
    return out.reshape(nseq, ATT_Q_W)


FF_CHUNK = 1408


def _ffn_kernel(x_ref, og_ref, om_ref, os_ref, wog_ref, wom_ref, wos_ref, g2_ref, wg_ref, wu_ref, wd_ref, y_ref,
                h2_scr):
    @pl.when(pl.program_id(1) == 0)
    def _():
        x1 = (x_ref[...] + _dot(og_ref[...], wog_ref[...]) + _dot(om_ref[...], wom_ref[...])
              + _dot(os_ref[...], wos_ref[...]))
        ms = jnp.mean(x1 * x1, axis=-1, keepdims=True)
        h2_scr[...] = (x1 * lax.rsqrt(ms + RMS_EPS) * g2_ref[...]).astype(BF16)
        y_ref[...] = x1

    h2 = h2_scr[...]
    a = _silu(_dot(h2, wg_ref[...])) * _dot(h2, wu_ref[...])
    y_ref[...] += _dot(a.astype(BF16), wd_ref[...])


def _ffn(x, og, om, osb, lw, tm):
    n = x.shape[0]
    row = lambda w: pl.BlockSpec((tm, w), lambda i, f: (i, 0))
    full = lambda a: pl.BlockSpec(a.shape, lambda i, f: (0, 0))
    ws = [lw['wo_g'], lw['wo_m'], lw['wo_s'], lw['g2'], lw['w_gate'], lw['w_up'], lw['w_down']]
    w_specs = [full(w) for w in ws[:4]] + [pl.BlockSpec((D_MODEL, FF_CHUNK), lambda i, f: (0, f)),
                                           pl.BlockSpec((D_MODEL, FF_CHUNK), lambda i, f: (0, f)),
                                           pl.BlockSpec((FF_CHUNK, D_MODEL), lambda i, f: (f, 0))]
    return pl.pallas_call(
        _ffn_kernel, grid=(n // tm, D_FF // FF_CHUNK),
        in_specs=[row(D_MODEL), row(GLA_V_W), row(ATT_Q_W), row(ATT_Q_W)] + w_specs,
        out_specs=row(D_MODEL), out_shape=jax.ShapeDtypeStruct((n, D_MODEL), F32),
        scratch_shapes=[pltpu.VMEM((tm, D_MODEL), BF16)],
        compiler_params=pltpu.CompilerParams(dimension_semantics=("arbitrary", "arbitrary"),
                                             vmem_limit_bytes=VMEM_LIMIT),
        name=f"ffn_{tm}")(x, og, om, osb, *ws)


def _prep_layer(l, norm1, w_in, w_alpha, b_alpha, gla_norm, moba_q_norm, moba_k_norm, sb_q_norm, sb_k_norm,
                w_out, norm2, w_gate_up, w_down):
    wi = w_in[l]
    o = _OFF
    cols = [wi[:, o['gq']:o['ga']],
            jnp.pad(wi[:, o['ga']:o['mq']], ((0, 0), (0, LANES - GLA_RANK))),
            wi[:, o['mq']:o['mk']][:, _HEAD_PERM], wi[:, o['mk']:o['sq']],
            wi[:, o['sq']:o['sk']][:, _HEAD_PERM], wi[:, o['sk']:]]
    wo = w_out[l]
    tile = lambda g, reps: jnp.tile(g[l], reps).reshape(1, -1)
    return dict(
        g1=norm1[l].reshape(1, -1), w_in=jnp.concatenate(cols, axis=1).astype(BF16),
        w_alpha=jnp.pad(w_alpha[l], ((0, LANES - GLA_RANK), (0, 0))).astype(BF16),
        b_alpha=b_alpha[l].reshape(1, -1),
        gnorm=tile(gla_norm, GLA_HEADS), mqg=tile(moba_q_norm, ATT_HEADS), mkg=tile(moba_k_norm, KV_HEADS),
        sqg=tile(sb_q_norm, ATT_HEADS), skg=tile(sb_k_norm, KV_HEADS),
        wo_g=wo[0:256].astype(BF16), wo_m=wo[256:640][_HEAD_PERM].astype(BF16),
        wo_s=wo[640:1024][_HEAD_PERM].astype(BF16), g2=norm2[l].reshape(1, -1),
        w_gate=w_gate_up[l][:, :D_FF].astype(BF16), w_up=w_gate_up[l][:, D_FF:].astype(BF16),
        w_down=w_down[l].astype(BF16))


def _state_to_blockdiag_T(s):
    b = s.shape[0]
    eye = jnp.eye(GLA_HEADS, dtype=s.dtype)
    return jnp.einsum('bhkv,hg->bhvgk', s, eye).reshape(b, GLA_V_W, GLA_QK_W)


def _blockdiag_T_to_state(st):
    b = st.shape[0]
    s5 = st.reshape(b, GLA_HEADS, GLA_DV, GLA_HEADS, GLA_DK)
    return jnp.stack([s5[:, h, :, h, :] for h in range(GLA_HEADS)], axis=1).transpose(0, 1, 3, 2)


def kernel(x_prompt, x_sample, cache_moba_kv, cache_sb_kv, state_gla, page_table, rel_bias, norm1, w_in, w_alpha,
           b_alpha, gla_norm, moba_q_norm, moba_k_norm, sb_q_norm, sb_k_norm, w_out, norm2, w_gate_up, w_down):
    nb, seq, _ = x_prompt.shape
    ns = x_sample.shape[0]
    n_prompt = nb * seq
    tm = 512
    assert seq % (2 * MOBA_BLOCK) == 0 and x_sample.shape[1] == 1 and seq // MOBA_BLOCK <= LANES
    bias_own, bias_prev, bias_samp = _bias_tiles(rel_bias)
    xp = x_prompt.reshape(n_prompt, D_MODEL)
    xs = x_sample.reshape(ns, D_MODEL)
    zero_state = jnp.zeros((nb, GLA_V_W, GLA_QK_W), F32)
    outs = dict(pm=[], ps=[], pg=[], sm=[], ss=[], sg=[])
    for l in range(DEPTH):
        lw = _prep_layer(l, norm1, w_in, w_alpha, b_alpha, gla_norm, moba_q_norm, moba_k_norm, sb_q_norm,
                         sb_k_norm, w_out, norm2, w_gate_up, w_down)
        gq, gk, gl, gv, gg, mq, mkv32, mkv16, mkm, sq, skv32, skv16 = _inproj(xp, lw, tm, seq=seq)
        og, st = _gla(gq, gk, gl, gv, gg, lw['gnorm'], zero_state, batch=nb, seq=seq, chunk=GLA_CHUNK,
                      chunks_per_step=8)
        osb = _sb_prompt(sq, skv16, batch=nb, seq=seq)
        kmean = jnp.pad(mkm.reshape(nb, seq // MOBA_BLOCK, LANES), ((0, 0), (0, LANES - seq // MOBA_BLOCK), (0, 0)))
        om = _moba_prompt(mq, mkv16, kmean, bias_own, bias_prev, batch=nb, seq=seq)
        xp = _ffn(xp, og, om, osb, lw, tm)
        leaf = lambda a: a.reshape(nb, 2, KV_HEADS, HEAD_DIM, seq).transpose(0, 4, 1, 2, 3)
        outs['pm'].append(leaf(mkv32))
        outs['ps'].append(leaf(skv32))
        outs['pg'].append(_blockdiag_T_to_state(st))
        gq, gk, gl, gv, gg, mq, mkv32, mkv16, _, sq, skv32, skv16 = _inproj(xs, lw, ns)
        og, st = _gla_sample(gq, gk, gl, gv, gg, lw['gnorm'], _state_to_blockdiag_T(state_gla[l]))
        osb = _sb_sample(sq, cache_sb_kv, page_table, l)
        om = _moba_sample(mq, mkv16, cache_moba_kv, page_table, bias_samp, l)
        xs = _ffn(xs, og, om, osb, lw, ns)
        outs['sm'].append(mkv32.reshape(ns, 1, 2, KV_HEADS, HEAD_DIM))
        outs['ss'].append(skv32.reshape(ns, 1, 2, KV_HEADS, HEAD_DIM))
        outs['sg'].append(_blockdiag_T_to_state(st))
    return (xp.reshape(nb, seq, D_MODEL), xs.reshape(ns, 1, D_MODEL), jnp.stack(outs['pm']), jnp.stack(outs['ps']),
            jnp.stack(outs['pg']), jnp.stack(outs['sm']), jnp.stack(outs['ss']), jnp.stack(outs['sg']))
```

```python
import functools
import math

import jax
import jax.numpy as jnp
import numpy as np
from jax import lax
from jax.experimental import pallas as pl
from jax.experimental.pallas import tpu as pltpu

F32 = jnp.float32
BF16 = jnp.bfloat16

D_MODEL = 1024
DEPTH = 4
HEAD_DIM = 64
GLA_HEADS = 4
GLA_DK = 32
GLA_DV = 64
GLA_RANK = 16
GLA_TAU = 16.0
GLA_CHUNK = 64
GLA_SUB = 16
ATT_HEADS = 6
KV_HEADS = 2
GROUP = ATT_HEADS // KV_HEADS
MOBA_BLOCK = 256
MOBA_TOPK = 3
REL_BUCKETS = 32
REL_MAX_DIST = 128
RMS_EPS = 1e-6
NEG_INF = -1e30
PAGE_SIZE = 128
D_FF = 2816
GLA_QK_W = GLA_HEADS * GLA_DK
GLA_V_W = GLA_HEADS * GLA_DV
ATT_Q_W = ATT_HEADS * HEAD_DIM
ATT_KV_W = 2 * KV_HEADS * HEAD_DIM
LANES = 128
ATT_TILE = 256
ATT_ROWS = ATT_HEADS * ATT_TILE
ROW_TILE = 128
VMEM_LIMIT = 56 * 1024 * 1024
PAGES_PER_STEP = 16

_OFF = dict(gq=0, gk=128, gv=256, gg=512, ga=768, mq=784, mk=1168, mv=1296, sq=1424, sk=1808, sv=1936)
IN_W_PAD = 2176
_HEAD_PERM = np.concatenate([np.concatenate([np.arange(64) + 64 * j, np.arange(64) + 64 * (GROUP + j)])
                             for j in range(GROUP)])


def _t5_thresholds():
    n = np.arange(0, 4 * REL_MAX_DIST, dtype=np.int64)
    max_exact = REL_BUCKETS // 2
    nf = np.maximum(n, 1).astype(np.float32)
    large = max_exact + (np.log(nf / np.float32(max_exact)) / np.float32(math.log(REL_MAX_DIST / max_exact))
                         * np.float32(REL_BUCKETS - max_exact)).astype(np.int32)
    bucket = np.where(n < max_exact, n, np.minimum(large, REL_BUCKETS - 1))
    return [int(np.argmax(bucket >= b)) for b in range(REL_BUCKETS)]


_T5_THR = _t5_thresholds()


def _dot(a, b):
    return jnp.dot(a, b, preferred_element_type=F32)


def _dot_nt(a, b):
    return lax.dot_general(a, b, (((1,), (1,)), ((), ())), preferred_element_type=F32)


def _split_hilo(a):
    hi = a.astype(BF16)
    lo = (a - hi.astype(F32)).astype(BF16)
    return hi, lo


def _dot_hilo(a, b_bf16):
    hi, lo = _split_hilo(a)
    return _dot(hi, b_bf16) + _dot(lo, b_bf16)


def _group_mean_matrix(width, group):
    r = lax.broadcasted_iota(jnp.int32, (width, width), 0) // group
    c = lax.broadcasted_iota(jnp.int32, (width, width), 1) // group
    return jnp.where(r == c, 1.0 / group, 0.0).astype(BF16)


def _neg_softplus(z):
    return -(jnp.maximum(z, 0.0) + jnp.log(1.0 + jnp.exp(-jnp.abs(z))))


def _log_sigmoid(x):
    return jnp.minimum(x, 0.0) - jnp.log(1.0 + jnp.exp(-jnp.abs(x)))


def _silu(x):
    return x / (1.0 + jnp.exp(-x))


def _head_rms(x, gain, group):
    ms = _dot_hilo(x * x, _group_mean_matrix(x.shape[1], group))
    return x * lax.rsqrt(ms + RMS_EPS) * gain


def _inproj_kernel(x_ref, g1_ref, w_ref, wa_ref, ba_ref, mqg_ref, mkg_ref, sqg_ref, skg_ref,
                   gq_ref, gk_ref, gl_ref, gv_ref, gg_ref, *att_refs, prompt):
    x = x_ref[...]
    ms = jnp.mean(x * x, axis=-1, keepdims=True)
    h = (x * lax.rsqrt(ms + RMS_EPS) * g1_ref[...]).astype(BF16)
    p = _dot(h, w_ref[...])
    gq_ref[...] = p[:, 0:128] * (GLA_DK ** -0.5)
    gk_ref[...] = p[:, 128:256]
    gv_ref[...] = p[:, 256:512]
    gg_ref[...] = p[:, 512:768]
    alpha = _dot(p[:, 768:896].astype(BF16), wa_ref[...]) + ba_ref[...]
    gl_ref[...] = _log_sigmoid(alpha) / GLA_TAU

    def attn_group(base, qg_ref, kg_ref, refs):
        q = _head_rms(p[:, base:base + ATT_Q_W], qg_ref[...], HEAD_DIM) * (HEAD_DIM ** -0.5)
        refs[0][...] = q.astype(BF16)
        k = _head_rms(p[:, base + 384:base + 512], kg_ref[...], HEAD_DIM)
        v = p[:, base + 512:base + 640]
        if prompt:
            _, kv32t_ref, kt16_ref, v16_ref = refs
            kt = k.T
            kv32t_ref[0, 0:128, :] = kt
            kv32t_ref[0, 128:256, :] = v.T
            kt16 = kt.astype(BF16)
            for c in range(kt16_ref.shape[1]):
                kt16_ref[0, c] = kt16[:, c * ATT_TILE:(c + 1) * ATT_TILE]
            v16_ref[...] = v.astype(BF16)
        else:
            _, kv32_ref, kv16_ref = refs
            kv32_ref[:, 0:128] = k
            kv32_ref[:, 128:256] = v
            kv16_ref[:, 0:128] = k.astype(BF16)
            kv16_ref[:, 128:256] = v.astype(BF16)
        return k

    if prompt:
        mk = attn_group(896, mqg_ref, mkg_ref, att_refs[0:4])
        attn_group(1536, sqg_ref, skg_ref, att_refs[5:9])
        mkm_ref = att_refs[4]
        for j in range(mkm_ref.shape[1]):
            mkm_ref[0, j:j + 1, :] = jnp.mean(mk[j * MOBA_BLOCK:(j + 1) * MOBA_BLOCK], axis=0, keepdims=True)
    else:
        attn_group(896, mqg_ref, mkg_ref, att_refs[0:3])
        attn_group(1536, sqg_ref, skg_ref, att_refs[3:6])


def _inproj(x, lw, tm, seq=None):
    n = x.shape[0]
    grid = n // tm
    row = lambda w: pl.BlockSpec((tm, w), lambda i: (i, 0))
    full = lambda a: pl.BlockSpec(a.shape, lambda i: (0,) * a.ndim)
    sds = jax.ShapeDtypeStruct
    outs = [sds((n, 128), F32), sds((n, 128), F32), sds((n, 128), F32), sds((n, 256), F32), sds((n, 256), F32)]
    out_specs = [row(128), row(128), row(128), row(256), row(256)]
    if seq is None:
        group = [(sds((n, ATT_Q_W), BF16), row(ATT_Q_W)), (sds((n, ATT_KV_W), F32), row(ATT_KV_W)),
                 (sds((n, ATT_KV_W), BF16), row(ATT_KV_W))]
        att = group + group
    else:
        tiles = seq // tm
        nblk = tm // ATT_TILE
        group = [(sds((n, ATT_Q_W), BF16), row(ATT_Q_W)),
                 (sds((n // seq, ATT_KV_W, seq), F32),
                  pl.BlockSpec((1, ATT_KV_W, tm), lambda i: (i // tiles, 0, i % tiles))),
                 (sds((n // seq, seq // ATT_TILE, 128, ATT_TILE), BF16),
                  pl.BlockSpec((1, nblk, 128, ATT_TILE), lambda i: (i // tiles, i % tiles, 0, 0))),
                 (sds((n, 128), BF16), row(128))]
        means = (sds((grid, nblk, 128), F32), pl.BlockSpec((1, nblk, 128), lambda i: (i, 0, 0)))
        att = group + [means] + group
    outs += [a for a, _ in att]
    out_specs += [b for _, b in att]
    ins = [x, lw['g1'], lw['w_in'], lw['w_alpha'], lw['b_alpha'], lw['mqg'], lw['mkg'], lw['sqg'], lw['skg']]
    in_specs = [row(D_MODEL)] + [full(a) for a in ins[1:]]
    return pl.pallas_call(
        functools.partial(_inproj_kernel, prompt=seq is not None), grid=(grid,),
        in_specs=in_specs, out_specs=out_specs, out_shape=outs,
        compiler_params=pltpu.CompilerParams(dimension_semantics=("arbitrary",), vmem_limit_bytes=VMEM_LIMIT),
        name=f"inproj_{tm}")(*ins)


def _gla_kernel(q_ref, k_ref, gl_ref, v_ref, gg_ref, gn_ref, s0_ref, o_ref, sT_ref,
                st_scr, p_scr, w_scr, *, chunk, n_chunks):
    c = chunk
    step = pl.program_id(1)

    @pl.when(step == 0)
    def _():
        st_scr[...] = s0_ref[0]

    ri = lax.broadcasted_iota(jnp.int32, (c, c), 0)
    ci = lax.broadcasted_iota(jnp.int32, (c, c), 1)
    ltri = jnp.where(ri >= ci, 1.0, 0.0).astype(BF16)
    kh = lax.broadcasted_iota(jnp.int32, (GLA_QK_W, GLA_V_W), 0) // GLA_DK
    vh = lax.broadcasted_iota(jnp.int32, (GLA_QK_W, GLA_V_W), 1) // GLA_DV
    head_ones = jnp.where(kh == vh, 1.0, 0.0).astype(BF16)
    vh2 = lax.broadcasted_iota(jnp.int32, (GLA_V_W, GLA_QK_W), 0) // GLA_DV
    kh2 = lax.broadcasted_iota(jnp.int32, (GLA_V_W, GLA_QK_W), 1) // GLA_DK
    bd_mask = vh2 == kh2
    sc = GLA_SUB
    n_sub = c // sc
    trow = lax.broadcasted_iota(jnp.int32, (sc, GLA_QK_W), 0)
    qk_head = lax.broadcasted_iota(jnp.int32, (sc, GLA_QK_W), 1) // GLA_DK
    v_head = lax.broadcasted_iota(jnp.int32, (sc, GLA_V_W), 1) // GLA_DV
    key_col = lax.broadcasted_iota(jnp.int32, (GLA_HEADS * sc, c), 1)

    def one_chunk(ic, carry):
        r0 = pl.multiple_of(ic * c, c)
        q = q_ref[pl.ds(r0, c), :]
        k = k_ref[pl.ds(r0, c), :]
        g = gl_ref[pl.ds(r0, c), :]
        v = v_ref[pl.ds(r0, c), :]
        g_hi, g_lo = _split_hilo(g)
        b = _dot(ltri, g_hi) + _dot(ltri, g_lo)
        st = st_scr[...]
        o_inter = _dot_nt((q * jnp.exp(b)).astype(BF16), st.astype(BF16))
        v16 = v.astype(BF16)

        for i_sub in range(n_sub):
            rows = slice(i_sub * sc, (i_sub + 1) * sc)
            q_i, b_i, k_i = q[rows], b[rows], k[rows]
            for s in range(sc):
                sl = q_i * jnp.exp(jnp.minimum(b_i - b_i[s:s + 1], 0.0)) * k_i[s:s + 1]
                off = (i_sub * sc + s) * sc
                p_scr[off:off + sc, :] = jnp.where(trow >= s, sl, 0.0)
        w_scr[...] = _dot_hilo(p_scr[...], head_ones)

        b_last = b[c - 1:c, :]
        kt = (k * jnp.exp(b_last - b)).astype(BF16)
        upd = _dot(v.T.astype(BF16), kt)
        st_scr[...] = st * jnp.exp(b_last) + jnp.where(bd_mask, upd, 0.0)

        for i_sub in range(n_sub):
            rows = slice(i_sub * sc, (i_sub + 1) * sc)
            o_i = o_inter[rows]
            for s in range(sc):
                off = (i_sub * sc + s) * sc
                o_i = o_i + w_scr[off:off + sc, :] * v[i_sub * sc + s:i_sub * sc + s + 1, :]
            if i_sub > 0:
                e = b[i_sub * sc - 1:i_sub * sc, :]
                a = q[rows] * jnp.exp(b[rows] - e)
                a4 = jnp.concatenate([jnp.where(qk_head == h, a, 0.0) for h in range(GLA_HEADS)], axis=0)
                kd = (k * jnp.exp(jnp.minimum(e - b, 0.0))).astype(BF16)
                s4 = jnp.where(key_col < i_sub * sc, _dot_nt(a4.astype(BF16), kd), 0.0)
                o4 = _dot(s4.astype(BF16), v16)
                for h in range(GLA_HEADS):
                    o_i = o_i + jnp.where(v_head == h, o4[h * sc:(h + 1) * sc], 0.0)
            on = _head_rms(o_i, gn_ref[...], GLA_DV)
            dst = pl.ds(r0 + i_sub * sc, sc)
            o_ref[dst, :] = (on * _silu(gg_ref[dst, :])).astype(o_ref.dtype)
        return carry

    lax.fori_loop(0, n_chunks, one_chunk, 0)

    @pl.when(step == pl.num_programs(1) - 1)
    def _():
        sT_ref[0] = st_scr[...]


def _gla(gq, gk, gl, gv, gg, gnorm, s0T, *, batch, seq, chunk, chunks_per_step):
    rows = chunk * chunks_per_step
    steps = seq // rows
    row = lambda w: pl.BlockSpec((rows, w), lambda b, i: (b * steps + i, 0))
    kern = functools.partial(_gla_kernel, chunk=chunk, n_chunks=chunks_per_step)
    return pl.pallas_call(
        kern, grid=(batch, steps),
        in_specs=[row(128), row(128), row(128), row(256), row(256),
                  pl.BlockSpec((1, GLA_V_W), lambda b, i: (0, 0)),
                  pl.BlockSpec((1, GLA_V_W, GLA_QK_W), lambda b, i: (b, 0, 0))],
        out_specs=[row(256), pl.BlockSpec((1, GLA_V_W, GLA_QK_W), lambda b, i: (b, 0, 0))],
        out_shape=[jax.ShapeDtypeStruct((batch * seq, GLA_V_W), BF16),
                   jax.ShapeDtypeStruct((batch, GLA_V_W, GLA_QK_W), F32)],
        scratch_shapes=[pltpu.VMEM((GLA_V_W, GLA_QK_W), F32),
                        pltpu.VMEM((chunk * GLA_SUB, GLA_QK_W), F32),
                        pltpu.VMEM((chunk * GLA_SUB, GLA_V_W), F32)],
        compiler_params=pltpu.CompilerParams(dimension_semantics=("arbitrary", "arbitrary"),
                                             vmem_limit_bytes=VMEM_LIMIT),
        name="gla_prompt")(gq, gk, gl, gv, gg, gnorm, s0T)


def _gla_sample_kernel(q_ref, k_ref, gl_ref, v_ref, gg_ref, gn_ref, s0_ref, o_ref, sT_ref):
    rows = (8, GLA_QK_W)
    decay = jnp.exp(gl_ref[0])
    r = lax.broadcasted_iota(jnp.int32, (GLA_V_W, GLA_V_W), 0)
    cc = lax.broadcasted_iota(jnp.int32, (GLA_V_W, GLA_V_W), 1)
    v_diag = jnp.where(r == cc, jnp.broadcast_to(v_ref[0], (GLA_V_W, GLA_V_W)), 0.0).astype(BF16)
    k_rows = jnp.broadcast_to(k_ref[0], (GLA_V_W, GLA_QK_W)).astype(BF16)
    vh = lax.broadcasted_iota(jnp.int32, (GLA_V_W, GLA_QK_W), 0) // GLA_DV
    kh = lax.broadcasted_iota(jnp.int32, (GLA_V_W, GLA_QK_W), 1) // GLA_DK
    outer = jnp.where(vh == kh, _dot(v_diag, k_rows), 0.0)
    st = s0_ref[0] * decay + outer
    sT_ref[0] = st
    q8 = jnp.broadcast_to(q_ref[0], rows).astype(BF16)
    o = _dot_nt(q8, st.astype(BF16))
    on = _head_rms(o, gn_ref[...], GLA_DV)
    o_ref[0] = (on * _silu(gg_ref[0]))[0:1].astype(o_ref.dtype)


def _gla_sample(gq, gk, gl, gv, gg, gnorm, s0T):
    n = gq.shape[0]
    r3 = lambda a: a.reshape(n, 1, a.shape[-1])
    row = lambda w: pl.BlockSpec((1, 1, w), lambda b: (b, 0, 0))
    st_spec = pl.BlockSpec((1, GLA_V_W, GLA_QK_W), lambda b: (b, 0, 0))
    o, st = pl.pallas_call(
        _gla_sample_kernel, grid=(n,),
        in_specs=[row(128), row(128), row(128), row(256), row(256),
                  pl.BlockSpec((1, GLA_V_W), lambda b: (0, 0)), st_spec],
        out_specs=[row(256), st_spec],
        out_shape=[jax.ShapeDtypeStruct((n, 1, GLA_V_W), BF16), jax.ShapeDtypeStruct((n, GLA_V_W, GLA_QK_W), F32)],
        compiler_params=pltpu.CompilerParams(dimension_semantics=("arbitrary",), vmem_limit_bytes=VMEM_LIMIT),
        name="gla_sample")(r3(gq), r3(gk), r3(gl), r3(gv), r3(gg), gnorm, s0T)
    return o.reshape(n, GLA_V_W), st


def _stack_heads(q_cols):
    lane = lax.broadcasted_iota(jnp.int32, q_cols[0].shape, 1)
    low = lane < HEAD_DIM
    zero = jnp.zeros_like(q_cols[0])
    parts = [jnp.where(low, qc, zero) for qc in q_cols] + [jnp.where(low, zero, qc) for qc in q_cols]
    return jnp.concatenate(parts, axis=0)


def _unstack_heads(o, rows):
    lane = lax.broadcasted_iota(jnp.int32, (rows, LANES), 1)
    low = lane < HEAD_DIM
    return [jnp.where(low, o[j * rows:(j + 1) * rows], o[(GROUP + j) * rows:(GROUP + j + 1) * rows])
            for j in range(GROUP)]


def _suffix_matrix():
    r = lax.broadcasted_iota(jnp.int32, (ATT_TILE, ATT_TILE), 0)
    c = lax.broadcasted_iota(jnp.int32, (ATT_TILE, ATT_TILE), 1)
    u = jnp.where(r > c, 1.0, 0.0).astype(BF16)
    return jnp.concatenate([u, u], axis=0)


def _twice(a):
    return jnp.concatenate([a, a], axis=1)


def _sb_rows(q, kt, v, u2, carry, mask, v_feature_major=False):
    z = _dot(q, kt)
    l = _neg_softplus(z)
    if mask is not None:
        l = jnp.where(mask, l, 0.0)
    hi, lo = _split_hilo(l)
    c = _dot(jnp.concatenate([hi, lo], axis=1), u2)
    w = jnp.exp(z + l + c + _twice(carry))
    if mask is not None:
        w = jnp.where(mask, w, 0.0)
    pv = _dot_nt(w.astype(BF16), v) if v_feature_major else _dot(w.astype(BF16), v)
    return pv, carry + jnp.sum(l, axis=1, keepdims=True)


SB_STOP = -104.0


def _sb_prompt_kernel(q_ref, kt_ref, v_ref, o_ref, q_scr, carry_scr, acc_scr):
    i = pl.program_id(1)
    t = ATT_TILE
    q_scr[...] = _stack_heads([q_ref[0, :, 128 * j:128 * (j + 1)] for j in range(GROUP)])
    carry_scr[...] = jnp.zeros(carry_scr.shape, F32)
    acc_scr[...] = jnp.zeros(acc_scr.shape, F32)
    u2 = _suffix_matrix()
    rr = lax.broadcasted_iota(jnp.int32, (ROW_TILE, t), 0)
    cc = lax.broadcasted_iota(jnp.int32, (ROW_TILE, t), 1)

    def tile(kj, diagonal):
        kt = kt_ref[0, kj]
        v = v_ref[0, pl.ds(pl.multiple_of(kj * t, t), t), :]
        for r in range(ATT_ROWS // ROW_TILE):
            rows = slice(r * ROW_TILE, (r + 1) * ROW_TILE)
            mask = cc < rr + (r * ROW_TILE) % t if diagonal else None
            pv, carry = _sb_rows(q_scr[rows, :], kt, v, u2, carry_scr[rows, :], mask)
            acc_scr[rows, :] += pv
            carry_scr[rows, :] = carry

    def carry_max():
        return jnp.max(jnp.max(carry_scr[...], axis=0, keepdims=True), axis=1, keepdims=True)[0, 0]

    tile(i, True)

    def cond(state):
        kj, cmax = state
        return jnp.logical_and(kj >= 0, cmax > SB_STOP)

    def body(state):
        kj, _ = state
        tile(kj, False)
        return kj - 1, carry_max()
    lax.while_loop(cond, body, (i - 1, carry_max()))

    cols = _unstack_heads(acc_scr[...], t)
    for j in range(GROUP):
        o_ref[0, :, 128 * j:128 * (j + 1)] = cols[j].astype(o_ref.dtype)


def _sb_prompt(q, kt, v, *, batch, seq):
    t = ATT_TILE
    out = pl.pallas_call(
        _sb_prompt_kernel, grid=(batch, seq // t),
        in_specs=[pl.BlockSpec((1, t, ATT_Q_W), lambda b, i: (b, i, 0)),
                  pl.BlockSpec((1, seq // t, LANES, t), lambda b, i: (b, 0, 0, 0)),
                  pl.BlockSpec((1, seq, LANES), lambda b, i: (b, 0, 0))],
        out_specs=pl.BlockSpec((1, t, ATT_Q_W), lambda b, i: (b, i, 0)),
        out_shape=jax.ShapeDtypeStruct((batch, seq, ATT_Q_W), BF16),
        scratch_shapes=[pltpu.VMEM((ATT_ROWS, LANES), BF16), pltpu.VMEM((ATT_ROWS, LANES), F32),
                        pltpu.VMEM((ATT_ROWS, LANES), F32)],
        compiler_params=pltpu.CompilerParams(dimension_semantics=("arbitrary", "arbitrary"),
                                             vmem_limit_bytes=VMEM_LIMIT),
        name="sb_prompt")(q.reshape(batch, seq, ATT_Q_W), kt, v.reshape(batch, seq, LANES))
    return out.reshape(batch * seq, ATT_Q_W)


def _bias_of_dist(dist, rb_ref, h):
    bias = jnp.full(dist.shape, rb_ref[0, h], F32)
    for b in range(1, REL_BUCKETS):
        bias = jnp.where(dist >= _T5_THR[b], rb_ref[b, h], bias)
    return bias - rb_ref[REL_BUCKETS - 1, h]


def _bias_kernel(rb_ref, own_ref, prev_ref, samp_ref):
    t = ATT_TILE
    r = lax.broadcasted_iota(jnp.int32, (t, t), 0)
    c = lax.broadcasted_iota(jnp.int32, (t, t), 1)
    for h in range(ATT_HEADS):
        own = _bias_of_dist(jnp.maximum(r - c, 0), rb_ref, h)
        own_ref[h * t:(h + 1) * t, :] = jnp.where(c <= r, own, NEG_INF)
        prev_ref[h * t:(h + 1) * t, :] = _bias_of_dist(r - c + t, rb_ref, h)
    s = lax.broadcasted_iota(jnp.int32, (8, t), 1)
    hrow = lax.broadcasted_iota(jnp.int32, (8, t), 0)
    last = jnp.zeros((8, t), F32)
    self_b = jnp.zeros((8, LANES), F32)
    hrow2 = lax.broadcasted_iota(jnp.int32, (8, LANES), 0)
    for h in range(ATT_HEADS):
        last = jnp.where(hrow == h, _bias_of_dist(t - s, rb_ref, h), last)
        self_b = jnp.where(hrow2 == h, rb_ref[0, h] - rb_ref[REL_BUCKETS - 1, h], self_b)
    samp_ref[:, 0:t] = last
    samp_ref[:, t:t + LANES] = self_b


def _bias_tiles(rel_bias):
    t = ATT_TILE
    return pl.pallas_call(
        _bias_kernel,
        in_specs=[pl.BlockSpec(memory_space=pltpu.SMEM)],
        out_shape=[jax.ShapeDtypeStruct((ATT_ROWS, t), F32), jax.ShapeDtypeStruct((ATT_ROWS, t), F32),
                   jax.ShapeDtypeStruct((8, t + LANES), F32)],
        name="t5_bias_tiles")(rel_bias)


def _top3_select(gate, n_valid):
    lane_i = lax.broadcasted_iota(jnp.int32, gate.shape, 1)
    lane = lane_i.astype(F32)
    valid = lane_i < n_valid
    g = jnp.where(valid, gate, NEG_INF)
    sel = jnp.zeros(gate.shape, F32)
    for _ in range(MOBA_TOPK):
        m = jnp.max(g, axis=1, keepdims=True)
        idx = jnp.min(jnp.where(g == m, lane, float(LANES)), axis=1, keepdims=True)
        pick = lane == idx
        sel = jnp.where(pick, 1.0, sel)
        g = jnp.where(pick, -jnp.inf, g)
    return jnp.where(valid, sel, 0.0) > 0.5


def _moba_prompt_kernel(q_ref, kt_ref, v_ref, km_ref, own_ref, prev_ref, o_ref, qx_scr, m_scr, acc_scr):
    i = pl.program_id(1)
    t = ATT_TILE
    q = _stack_heads([q_ref[0, :, 128 * j:128 * (j + 1)] for j in range(GROUP)])
    gate = _dot_nt(q, km_ref[0].astype(BF16))
    sel = _top3_select(gate, i)
    qx_scr[:, 0:LANES] = q
    qx_scr[:, LANES:2 * LANES] = jnp.where(sel, 0.0, NEG_INF).astype(BF16)
    blk_row = lax.broadcasted_iota(jnp.int32, (LANES, t), 0)
    ones = jnp.ones((t, LANES), BF16)

    def tile(kj, bias_ref, first, past):
        kt = kt_ref[0, kj]
        if past:
            kt = jnp.concatenate([kt, jnp.where(blk_row == kj, 1.0, 0.0).astype(BF16)], axis=0)
        v = v_ref[0, pl.ds(pl.multiple_of(kj * t, t), t), :]
        vx = jnp.concatenate([v, ones], axis=1)
        for r in range(ATT_ROWS // ROW_TILE):
            rows = slice(r * ROW_TILE, (r + 1) * ROW_TILE)
            s = _dot(qx_scr[rows, :] if past else qx_scr[rows, 0:LANES], kt)
            if bias_ref is not None:
                s = s + bias_ref[rows, :]
            row_max = jnp.broadcast_to(jnp.max(s, axis=1, keepdims=True), (ROW_TILE, LANES))
            if first:
                m_new = row_max
                acc_scr[rows, :] = _dot(jnp.exp(s - _twice(m_new)).astype(BF16), vx)
            else:
                m_old = m_scr[rows, :]
                m_new = jnp.maximum(m_old, row_max)
                pv = _dot(jnp.exp(s - _twice(m_new)).astype(BF16), vx)
                acc_scr[rows, :] = _twice(jnp.exp(m_old - m_new)) * acc_scr[rows, :] + pv
            m_scr[rows, :] = m_new

    tile(i, own_ref, True, False)

    @pl.when(i >= 1)
    def _():
        tile(i - 1, prev_ref, False, True)

    def body(kj, carry):
        tile(kj, None, False, True)
        return carry
    lax.fori_loop(0, jnp.maximum(i - 1, 0), body, 0)

    o = acc_scr[:, 0:LANES] / acc_scr[:, LANES:2 * LANES]
    cols = _unstack_heads(o, t)
    for j in range(GROUP):
        o_ref[0, :, 128 * j:128 * (j + 1)] = cols[j].astype(o_ref.dtype)


def _moba_prompt(q, kt, v, kmean, bias_own, bias_prev, *, batch, seq):
    t = ATT_TILE
    const = lambda a: pl.BlockSpec(a.shape, lambda b, i: (0, 0))
    out = pl.pallas_call(
        _moba_prompt_kernel, grid=(batch, seq // t),
        in_specs=[pl.BlockSpec((1, t, ATT_Q_W), lambda b, i: (b, i, 0)),
                  pl.BlockSpec((1, seq // t, LANES, t), lambda b, i: (b, 0, 0, 0)),
                  pl.BlockSpec((1, seq, LANES), lambda b, i: (b, 0, 0)),
                  pl.BlockSpec((1, LANES, LANES), lambda b, i: (b, 0, 0)),
                  const(bias_own), const(bias_prev)],
        out_specs=pl.BlockSpec((1, t, ATT_Q_W), lambda b, i: (b, i, 0)),
        out_shape=jax.ShapeDtypeStruct((batch, seq, ATT_Q_W), BF16),
        scratch_shapes=[pltpu.VMEM((ATT_ROWS, 2 * LANES), BF16), pltpu.VMEM((ATT_ROWS, LANES), F32),
                        pltpu.VMEM((ATT_ROWS, 2 * LANES), F32)],
        compiler_params=pltpu.CompilerParams(dimension_semantics=("arbitrary", "arbitrary"),
                                             vmem_limit_bytes=VMEM_LIMIT),
        name="moba_prompt")(q.reshape(batch, seq, ATT_Q_W), kt, v.reshape(batch, seq, LANES), kmean,
                            bias_own, bias_prev)
    return out.reshape(batch * seq, ATT_Q_W)


def _sample_q_rows(q_row):
    row = lax.broadcasted_iota(jnp.int32, (8, LANES), 0)
    lane = lax.broadcasted_iota(jnp.int32, (8, LANES), 1)
    qf = q_row.astype(F32)
    out = jnp.zeros((8, LANES), F32)
    for h in range(ATT_HEADS):
        j, n = h % GROUP, h // GROUP
        col = jnp.broadcast_to(qf[:, 128 * j:128 * (j + 1)], (8, LANES))
        half = lane >= HEAD_DIM if n == 1 else lane < HEAD_DIM
        out = jnp.where(jnp.logical_and(row == h, half), col, out)
    return out.astype(BF16)


def _sample_o_row(o):
    lane = lax.broadcasted_iota(jnp.int32, (1, LANES), 1)
    return [jnp.where(lane < HEAD_DIM, o[j:j + 1], o[GROUP + j:GROUP + j + 1]) for j in range(GROUP)]


def _page_pair(p0_ref, p1_ref):
    k = jnp.concatenate([p0_ref[0, 0:128, :], p1_ref[0, 0:128, :]], axis=1).astype(BF16)
    v = jnp.concatenate([p0_ref[0, 128:256, :], p1_ref[0, 128:256, :]], axis=1).astype(BF16)
    return k, v


def _sb_sample_kernel(pt_ref, q_ref, *refs):
    del pt_ref
    pages = refs[:PAGES_PER_STEP]
    o_ref, carry_scr, acc_scr = refs[PAGES_PER_STEP:]
    step = pl.program_id(1)

    @pl.when(step == 0)
    def _():
        carry_scr[...] = jnp.zeros(carry_scr.shape, F32)
        acc_scr[...] = jnp.zeros(acc_scr.shape, F32)

    q = _sample_q_rows(q_ref[0])
    u2 = _suffix_matrix()
    carry, acc = carry_scr[...], acc_scr[...]
    for tl in reversed(range(PAGES_PER_STEP // 2)):
        kt, vt = _page_pair(pages[2 * tl], pages[2 * tl + 1])
        pv, carry = _sb_rows(q, kt, vt, u2, carry, None, v_feature_major=True)
        acc = acc + pv
    carry_scr[...] = carry
    acc_scr[...] = acc

    @pl.when(step == pl.num_programs(1) - 1)
    def _():
        cols = _sample_o_row(acc)
        for j in range(GROUP):
            o_ref[0, :, 128 * j:128 * (j + 1)] = cols[j].astype(o_ref.dtype)


def _page_specs(layer, n_pool, n_steps, reverse):
    specs = []
    for p in range(PAGES_PER_STEP):
        def imap(b, c, pt, p=p):
            cc = (n_steps - 1 - c) if reverse else c
            return (layer * n_pool + pt[b, cc * PAGES_PER_STEP + p], 0, 0)
        specs.append(pl.BlockSpec((1, ATT_KV_W, PAGE_SIZE), imap))
    return specs


def _cache_pages(cache):
    d, n_pool = cache.shape[:2]
    return cache.transpose(0, 1, 3, 4, 5, 2).reshape(d * n_pool, ATT_KV_W, PAGE_SIZE)


def _sb_sample(q, cache, page_table, layer):
    nseq, n_pages = page_table.shape
    n_pool = cache.shape[1]
    n_steps = n_pages // PAGES_PER_STEP
    cache2 = _cache_pages(cache)
    grid_spec = pltpu.PrefetchScalarGridSpec(
        num_scalar_prefetch=1, grid=(nseq, n_steps),
        in_specs=[pl.BlockSpec((1, 1, ATT_Q_W), lambda b, c, pt: (b, 0, 0))]
        + _page_specs(layer, n_pool, n_steps, reverse=True),
        out_specs=pl.BlockSpec((1, 1, ATT_Q_W), lambda b, c, pt: (b, 0, 0)),
        scratch_shapes=[pltpu.VMEM((8, LANES), F32), pltpu.VMEM((8, LANES), F32)])
    out = pl.pallas_call(
        _sb_sample_kernel, grid_spec=grid_spec,
        out_shape=jax.ShapeDtypeStruct((nseq, 1, ATT_Q_W), BF16),
        compiler_params=pltpu.CompilerParams(dimension_semantics=("arbitrary", "arbitrary"),
                                             vmem_limit_bytes=VMEM_LIMIT),
        name="sb_sample")(page_table, q.reshape(nseq, 1, ATT_Q_W), *([cache2] * PAGES_PER_STEP))
    return out.reshape(nseq, ATT_Q_W)


def _moba_sample_kernel(pt_ref, q_ref, kvn_ref, sb_ref, *refs, n_blocks):
    del pt_ref
    pages = refs[:PAGES_PER_STEP]
    o_ref, gate_scr, m_scr, l_scr, acc_scr = refs[PAGES_PER_STEP:]
    step = pl.program_id(1)
    bps = PAGES_PER_STEP // 2
    lane = lax.broadcasted_iota(jnp.int32, (8, LANES), 1)

    @pl.when(step == 0)
    def _():
        gate_scr[...] = jnp.zeros(gate_scr.shape, F32)
        m_scr[...] = jnp.zeros(m_scr.shape, F32)
        l_scr[...] = jnp.zeros(l_scr.shape, F32)

    q = _sample_q_rows(q_ref[0])
    qf = q.astype(F32)
    for tl in range(bps):
        blk = step * bps + tl
        k, v = _page_pair(pages[2 * tl], pages[2 * tl + 1])
        s = _dot(q, k)
        g = jnp.mean(s, axis=1, keepdims=True)
        s = s + jnp.where(blk == n_blocks - 1, 1.0, 0.0) * sb_ref[:, 0:ATT_TILE]
        m = jnp.max(s, axis=1, keepdims=True)
        p = jnp.exp(s - m)
        hit = lane == blk
        gate_scr[...] = jnp.where(hit, g, gate_scr[...])
        m_scr[...] = jnp.where(hit, m, m_scr[...])
        l_scr[...] = jnp.where(hit, jnp.sum(p, axis=1, keepdims=True), l_scr[...])
        acc_scr[blk] = _dot_nt(p.astype(BF16), v)

    @pl.when(step == pl.num_programs(1) - 1)
    def _():
        sel = _top3_select(gate_scr[...], n_blocks)
        kvn = kvn_ref[0]
        s_self = jnp.sum(qf * kvn[:, 0:128].astype(F32), axis=1, keepdims=True) + sb_ref[:, ATT_TILE:ATT_TILE + 1]
        m_all = m_scr[...]
        m_tot = jnp.maximum(jnp.max(jnp.where(sel, m_all, NEG_INF), axis=1, keepdims=True), s_self)
        coef = jnp.where(sel, jnp.exp(m_all - m_tot), 0.0)
        p_self = jnp.exp(s_self - m_tot)
        denom = jnp.sum(coef * l_scr[...], axis=1, keepdims=True) + p_self
        o = p_self * kvn[:, 128:256].astype(F32)
        for j in range(n_blocks):
            o = o + coef[:, j:j + 1] * acc_scr[j]
        cols = _sample_o_row(o / denom)
        for j in range(GROUP):
            o_ref[0, :, 128 * j:128 * (j + 1)] = cols[j].astype(o_ref.dtype)


def _moba_sample(q, kv_new, cache, page_table, bias_samp, layer):
    nseq, n_pages = page_table.shape
    n_pool = cache.shape[1]
    n_steps = n_pages // PAGES_PER_STEP
    n_blocks = n_pages * PAGE_SIZE // MOBA_BLOCK
    cache2 = _cache_pages(cache)
    grid_spec = pltpu.PrefetchScalarGridSpec(
        num_scalar_prefetch=1, grid=(nseq, n_steps),
        in_specs=[pl.BlockSpec((1, 1, ATT_Q_W), lambda b, c, pt: (b, 0, 0)),
                  pl.BlockSpec((1, 1, ATT_KV_W), lambda b, c, pt: (b, 0, 0)),
                  pl.BlockSpec(bias_samp.shape, lambda b, c, pt: (0, 0))]
        + _page_specs(layer, n_pool, n_steps, reverse=False),
        out_specs=pl.BlockSpec((1, 1, ATT_Q_W), lambda b, c, pt: (b, 0, 0)),
        scratch_shapes=[pltpu.VMEM((8, LANES), F32)] * 3 + [pltpu.VMEM((n_blocks, 8, LANES), F32)])
    out = pl.pallas_call(
        functools.partial(_moba_sample_kernel, n_blocks=n_blocks), grid_spec=grid_spec,
        out_shape=jax.ShapeDtypeStruct((nseq, 1, ATT_Q_W), BF16),
        compiler_params=pltpu.CompilerParams(dimension_semantics=("arbitrary", "arbitrary"),
                                             vmem_limit_bytes=VMEM_LIMIT),
        name="moba_sample")(page_table, q.reshape(nseq, 1, ATT_Q_W), kv_new.reshape(nseq, 1, ATT_KV_W),
                            bias_samp, *([cache2] * PAGES_PER_STEP))
    return out.reshape(nseq, ATT_Q_W)


FF_CHUNK = 1408


def _ffn_kernel(x_ref, og_ref, om_ref, os_ref, wog_ref, wom_ref, wos_ref, g2_ref, wg_ref, wu_ref, wd_ref, y_ref,
                h2_scr):
    @pl.when(pl.program_id(1) == 0)
    def _():
        x1 = (x_ref[...] + _dot(og_ref[...], wog_ref[...]) + _dot(om_ref[...], wom_ref[...])
              + _dot(os_ref[...], wos_ref[...]))
        ms = jnp.mean(x1 * x1, axis=-1, keepdims=True)
        h2_scr[...] = (x1 * lax.rsqrt(ms + RMS_EPS) * g2_ref[...]).astype(BF16)
        y_ref[...] = x1

    h2 = h2_scr[...]
    a = _silu(_dot(h2, wg_ref[...])) * _dot(h2, wu_ref[...])
    y_ref[...] += _dot(a.astype(BF16), wd_ref[...])


def _ffn(x, og, om, osb, lw, tm):
    n = x.shape[0]
    row = lambda w: pl.BlockSpec((tm, w), lambda i, f: (i, 0))
    full = lambda a: pl.BlockSpec(a.shape, lambda i, f: (0, 0))
    ws = [lw['wo_g'], lw['wo_m'], lw['wo_s'], lw['g2'], lw['w_gate'], lw['w_up'], lw['w_down']]
    w_specs = [full(w) for w in ws[:4]] + [pl.BlockSpec((D_MODEL, FF_CHUNK), lambda i, f: (0, f)),
                                           pl.BlockSpec((D_MODEL, FF_CHUNK), lambda i, f: (0, f)),
                                           pl.BlockSpec((FF_CHUNK, D_MODEL), lambda i, f: (f, 0))]
    return pl.pallas_call(
        _ffn_kernel, grid=(n // tm, D_FF // FF_CHUNK),
        in_specs=[row(D_MODEL), row(GLA_V_W), row(ATT_Q_W), row(ATT_Q_W)] + w_specs,
        out_specs=row(D_MODEL), out_shape=jax.ShapeDtypeStruct((n, D_MODEL), F32),
        scratch_shapes=[pltpu.VMEM((tm, D_MODEL), BF16)],
        compiler_params=pltpu.CompilerParams(dimension_semantics=("arbitrary", "arbitrary"),
                                             vmem_limit_bytes=VMEM_LIMIT),
        name=f"ffn_{tm}")(x, og, om, osb, *ws)


def _prep_layer(l, norm1, w_in, w_alpha, b_alpha, gla_norm, moba_q_norm, moba_k_norm, sb_q_norm, sb_k_norm,
                w_out, norm2, w_gate_up, w_down):
    wi = w_in[l]
    o = _OFF
    cols = [wi[:, o['gq']:o['ga']],
            jnp.pad(wi[:, o['ga']:o['mq']], ((0, 0), (0, LANES - GLA_RANK))),
            wi[:, o['mq']:o['mk']][:, _HEAD_PERM], wi[:, o['mk']:o['sq']],
            wi[:, o['sq']:o['sk']][:, _HEAD_PERM], wi[:, o['sk']:]]
    wo = w_out[l]
    tile = lambda g, reps: jnp.tile(g[l], reps).reshape(1, -1)
    return dict(
        g1=norm1[l].reshape(1, -1), w_in=jnp.concatenate(cols, axis=1).astype(BF16),
        w_alpha=jnp.pad(w_alpha[l], ((0, LANES - GLA_RANK), (0, 0))).astype(BF16),
        b_alpha=b_alpha[l].reshape(1, -1),
        gnorm=tile(gla_norm, GLA_HEADS), mqg=tile(moba_q_norm, ATT_HEADS), mkg=tile(moba_k_norm, KV_HEADS),
        sqg=tile(sb_q_norm, ATT_HEADS), skg=tile(sb_k_norm, KV_HEADS),
        wo_g=wo[0:256].astype(BF16), wo_m=wo[256:640][_HEAD_PERM].astype(BF16),
        wo_s=wo[640:1024][_HEAD_PERM].astype(BF16), g2=norm2[l].reshape(1, -1),
        w_gate=w_gate_up[l][:, :D_FF].astype(BF16), w_up=w_gate_up[l][:, D_FF:].astype(BF16),
        w_down=w_down[l].astype(BF16))


def _state_to_blockdiag_T(s):
    b = s.shape[0]
    eye = jnp.eye(GLA_HEADS, dtype=s.dtype)
    return jnp.einsum('bhkv,hg->bhvgk', s, eye).reshape(b, GLA_V_W, GLA_QK_W)


def _blockdiag_T_to_state(st):
    b = st.shape[0]
    s5 = st.reshape(b, GLA_HEADS, GLA_DV, GLA_HEADS, GLA_DK)
    return jnp.stack([s5[:, h, :, h, :] for h in range(GLA_HEADS)], axis=1).transpose(0, 1, 3, 2)


def kernel(x_prompt, x_sample, cache_moba_kv, cache_sb_kv, state_gla, page_table, rel_bias, norm1, w_in, w_alpha,
           b_alpha, gla_norm, moba_q_norm, moba_k_norm, sb_q_norm, sb_k_norm, w_out, norm2, w_gate_up, w_down):
    nb, seq, _ = x_prompt.shape
    ns = x_sample.shape[0]
    n_prompt = nb * seq
    tm = 512
    assert seq % (2 * MOBA_BLOCK) == 0 and x_sample.shape[1] == 1 and seq // MOBA_BLOCK <= LANES
    bias_own, bias_prev, bias_samp = _bias_tiles(rel_bias)
    xp = x_prompt.reshape(n_prompt, D_MODEL)
    xs = x_sample.reshape(ns, D_MODEL)
    zero_state = jnp.zeros((nb, GLA_V_W, GLA_QK_W), F32)
    outs = dict(pm=[], ps=[], pg=[], sm=[], ss=[], sg=[])
    for l in range(DEPTH):
        lw = _prep_layer(l, norm1, w_in, w_alpha, b_alpha, gla_norm, moba_q_norm, moba_k_norm, sb_q_norm,
                         sb_k_norm, w_out, norm2, w_gate_up, w_down)
        gq, gk, gl, gv, gg, mq, mkv32, mkt, mv, mkm, sq, skv32, skt, sv = _inproj(xp, lw, tm, seq=seq)
        og, st = _gla(gq, gk, gl, gv, gg, lw['gnorm'], zero_state, batch=nb, seq=seq, chunk=GLA_CHUNK,
                      chunks_per_step=8)
        osb = _sb_prompt(sq, skt, sv, batch=nb, seq=seq)
        kmean = jnp.pad(mkm.reshape(nb, seq // MOBA_BLOCK, LANES), ((0, 0), (0, LANES - seq // MOBA_BLOCK), (0, 0)))
        om = _moba_prompt(mq, mkt, mv, kmean, bias_own, bias_prev, batch=nb, seq=seq)
        xp = _ffn(xp, og, om, osb, lw, tm)
        leaf = lambda a: a.reshape(nb, 2, KV_HEADS, HEAD_DIM, seq).transpose(0, 4, 1, 2, 3)
        outs['pm'].append(leaf(mkv32))
        outs['ps'].append(leaf(skv32))
        outs['pg'].append(_blockdiag_T_to_state(st))
        gq, gk, gl, gv, gg, mq, mkv32, mkv16, sq, skv32, _ = _inproj(xs, lw, ns)
        og, st = _gla_sample(gq, gk, gl, gv, gg, lw['gnorm'], _state_to_blockdiag_T(state_gla[l]))
        osb = _sb_sample(sq, cache_sb_kv, page_table, l)
        om = _moba_sample(mq, mkv16, cache_moba_kv, page_table, bias_samp, l)
        xs = _ffn(xs, og, om, osb, lw, ns)
        outs['sm'].append(mkv32.reshape(ns, 1, 2, KV_HEADS, HEAD_DIM))
        outs['ss'].append(skv32.reshape(ns, 1, 2, KV_HEADS, HEAD_DIM))
        outs['sg'].append(_blockdiag_T_to_state(st))
    return (xp.reshape(nb, seq, D_MODEL), xs.reshape(ns, 1, D_MODEL), jnp.stack(outs['pm']), jnp.stack(outs['ps']),
            jnp.stack(outs['pg']), jnp.stack(outs['sm']), jnp.stack(outs['ss']), jnp.stack(outs['sg']))
```

```python
import functools
import math

import jax
import jax.numpy as jnp
import numpy as np
from jax import lax
from jax.experimental import pallas as pl
from jax.experimental.pallas import tpu as pltpu

F32 = jnp.float32
BF16 = jnp.bfloat16

D_MODEL = 1024
DEPTH = 4
HEAD_DIM = 64
GLA_HEADS = 4
GLA_DK = 32
GLA_DV = 64
GLA_RANK = 16
GLA_TAU = 16.0
GLA_CHUNK = 64
GLA_SUB = 16
ATT_HEADS = 6
KV_HEADS = 2
GROUP = ATT_HEADS // KV_HEADS
MOBA_BLOCK = 256
MOBA_TOPK = 3
REL_BUCKETS = 32
REL_MAX_DIST = 128
RMS_EPS = 1e-6
NEG_INF = -1e30
PAGE_SIZE = 128
D_FF = 2816
GLA_QK_W = GLA_HEADS * GLA_DK
GLA_V_W = GLA_HEADS * GLA_DV
ATT_Q_W = ATT_HEADS * HEAD_DIM
ATT_KV_W = 2 * KV_HEADS * HEAD_DIM
LANES = 128
ATT_TILE = 256
ATT_ROWS = ATT_HEADS * ATT_TILE
SB_ROW_TILE = 1536
MOBA_ROW_TILE = 128
VMEM_LIMIT = 56 * 1024 * 1024
PAGES_PER_STEP = 16

_OFF = dict(gq=0, gk=128, gv=256, gg=512, ga=768, mq=784, mk=1168, mv=1296, sq=1424, sk=1808, sv=1936)
IN_W_PAD = 2176
_HEAD_PERM = np.concatenate([np.concatenate([np.arange(64) + 64 * j, np.arange(64) + 64 * (GROUP + j)])
                             for j in range(GROUP)])


def _t5_thresholds():
    n = np.arange(0, 4 * REL_MAX_DIST, dtype=np.int64)
    max_exact = REL_BUCKETS // 2
    nf = np.maximum(n, 1).astype(np.float32)
    large = max_exact + (np.log(nf / np.float32(max_exact)) / np.float32(math.log(REL_MAX_DIST / max_exact))
                         * np.float32(REL_BUCKETS - max_exact)).astype(np.int32)
    bucket = np.where(n < max_exact, n, np.minimum(large, REL_BUCKETS - 1))
    return [int(np.argmax(bucket >= b)) for b in range(REL_BUCKETS)]


_T5_THR = _t5_thresholds()


def _dot(a, b):
    return jnp.dot(a, b, preferred_element_type=F32)


def _dot_nt(a, b):
    return lax.dot_general(a, b, (((1,), (1,)), ((), ())), preferred_element_type=F32)


def _split_hilo(a):
    hi = a.astype(BF16)
    lo = (a - hi.astype(F32)).astype(BF16)
    return hi, lo


def _dot_hilo(a, b_bf16):
    hi, lo = _split_hilo(a)
    return _dot(hi, b_bf16) + _dot(lo, b_bf16)


def _group_mean_matrix(width, group):
    r = lax.broadcasted_iota(jnp.int32, (width, width), 0) // group
    c = lax.broadcasted_iota(jnp.int32, (width, width), 1) // group
    return jnp.where(r == c, 1.0 / group, 0.0).astype(BF16)


def _neg_softplus(z):
    return -(jnp.maximum(z, 0.0) + jnp.log(1.0 + jnp.exp(-jnp.abs(z))))


def _log_sigmoid(x):
    return jnp.minimum(x, 0.0) - jnp.log(1.0 + jnp.exp(-jnp.abs(x)))


def _silu(x):
    return x / (1.0 + jnp.exp(-x))


def _head_rms(x, gain, group):
    ms = _dot_hilo(x * x, _group_mean_matrix(x.shape[1], group))
    return x * lax.rsqrt(ms + RMS_EPS) * gain


def _inproj_kernel(x_ref, g1_ref, w_ref, wa_ref, ba_ref, mqg_ref, mkg_ref, sqg_ref, skg_ref,
                   gq_ref, gk_ref, gl_ref, gv_ref, gg_ref, *att_refs, prompt):
    x = x_ref[...]
    ms = jnp.mean(x * x, axis=-1, keepdims=True)
    h = (x * lax.rsqrt(ms + RMS_EPS) * g1_ref[...]).astype(BF16)
    p = _dot(h, w_ref[...])
    gq_ref[...] = p[:, 0:128] * (GLA_DK ** -0.5)
    gk_ref[...] = p[:, 128:256]
    gv_ref[...] = p[:, 256:512]
    gg_ref[...] = p[:, 512:768]
    alpha = _dot(p[:, 768:896].astype(BF16), wa_ref[...]) + ba_ref[...]
    gl_ref[...] = _log_sigmoid(alpha) / GLA_TAU

    def attn_group(base, qg_ref, kg_ref, refs):
        q = _head_rms(p[:, base:base + ATT_Q_W], qg_ref[...], HEAD_DIM) * (HEAD_DIM ** -0.5)
        refs[0][...] = q.astype(BF16)
        k = _head_rms(p[:, base + 384:base + 512], kg_ref[...], HEAD_DIM)
        v = p[:, base + 512:base + 640]
        if prompt:
            _, kv32t_ref, kt16_ref, v16_ref = refs
            kt = k.T
            kv32t_ref[0, 0:128, :] = kt
            kv32t_ref[0, 128:256, :] = v.T
            kt16 = kt.astype(BF16)
            for c in range(kt16_ref.shape[1]):
                kt16_ref[0, c] = kt16[:, c * ATT_TILE:(c + 1) * ATT_TILE]
            v16_ref[...] = v.astype(BF16)
        else:
            _, kv32_ref, kv16_ref = refs
            kv32_ref[:, 0:128] = k
            kv32_ref[:, 128:256] = v
            kv16_ref[:, 0:128] = k.astype(BF16)
            kv16_ref[:, 128:256] = v.astype(BF16)
        return k

    if prompt:
        mk = attn_group(896, mqg_ref, mkg_ref, att_refs[0:4])
        attn_group(1536, sqg_ref, skg_ref, att_refs[5:9])
        mkm_ref = att_refs[4]
        for j in range(mkm_ref.shape[1]):
            mkm_ref[0, j:j + 1, :] = jnp.mean(mk[j * MOBA_BLOCK:(j + 1) * MOBA_BLOCK], axis=0, keepdims=True)
    else:
        attn_group(896, mqg_ref, mkg_ref, att_refs[0:3])
        attn_group(1536, sqg_ref, skg_ref, att_refs[3:6])


def _inproj(x, lw, tm, seq=None):
    n = x.shape[0]
    grid = n // tm
    row = lambda w: pl.BlockSpec((tm, w), lambda i: (i, 0))
    full = lambda a: pl.BlockSpec(a.shape, lambda i: (0,) * a.ndim)
    sds = jax.ShapeDtypeStruct
    outs = [sds((n, 128), F32), sds((n, 128), F32), sds((n, 128), F32), sds((n, 256), F32), sds((n, 256), F32)]
    out_specs = [row(128), row(128), row(128), row(256), row(256)]
    if seq is None:
        group = [(sds((n, ATT_Q_W), BF16), row(ATT_Q_W)), (sds((n, ATT_KV_W), F32), row(ATT_KV_W)),
                 (sds((n, ATT_KV_W), BF16), row(ATT_KV_W))]
        att = group + group
    else:
        tiles = seq // tm
        nblk = tm // ATT_TILE
        group = [(sds((n, ATT_Q_W), BF16), row(ATT_Q_W)),
                 (sds((n // seq, ATT_KV_W, seq), F32),
                  pl.BlockSpec((1, ATT_KV_W, tm), lambda i: (i // tiles, 0, i % tiles))),
                 (sds((n // seq, seq // ATT_TILE, 128, ATT_TILE), BF16),
                  pl.BlockSpec((1, nblk, 128, ATT_TILE), lambda i: (i // tiles, i % tiles, 0, 0))),
                 (sds((n, 128), BF16), row(128))]
        means = (sds((grid, nblk, 128), F32), pl.BlockSpec((1, nblk, 128), lambda i: (i, 0, 0)))
        att = group + [means] + group
    outs += [a for a, _ in att]
    out_specs += [b for _, b in att]
    ins = [x, lw['g1'], lw['w_in'], lw['w_alpha'], lw['b_alpha'], lw['mqg'], lw['mkg'], lw['sqg'], lw['skg']]
    in_specs = [row(D_MODEL)] + [full(a) for a in ins[1:]]
    return pl.pallas_call(
        functools.partial(_inproj_kernel, prompt=seq is not None), grid=(grid,),
        in_specs=in_specs, out_specs=out_specs, out_shape=outs,
        compiler_params=pltpu.CompilerParams(dimension_semantics=("arbitrary",), vmem_limit_bytes=VMEM_LIMIT),
        name=f"inproj_{tm}")(*ins)


def _gla_kernel(q_ref, k_ref, gl_ref, v_ref, gg_ref, gn_ref, s0_ref, o_ref, sT_ref,
                st_scr, p_scr, w_scr, *, chunk, n_chunks):
    c = chunk
    step = pl.program_id(0)
    n_seq = q_ref.shape[0]

    @pl.when(step == 0)
    def _():
        st_scr[...] = s0_ref[...]

    ri = lax.broadcasted_iota(jnp.int32, (c, c), 0)
    ci = lax.broadcasted_iota(jnp.int32, (c, c), 1)
    ltri = jnp.where(ri >= ci, 1.0, 0.0).astype(BF16)
    kh = lax.broadcasted_iota(jnp.int32, (GLA_QK_W, GLA_V_W), 0) // GLA_DK
    vh = lax.broadcasted_iota(jnp.int32, (GLA_QK_W, GLA_V_W), 1) // GLA_DV
    head_ones = jnp.where(kh == vh, 1.0, 0.0).astype(BF16)
    vh2 = lax.broadcasted_iota(jnp.int32, (GLA_V_W, GLA_QK_W), 0) // GLA_DV
    kh2 = lax.broadcasted_iota(jnp.int32, (GLA_V_W, GLA_QK_W), 1) // GLA_DK
    bd_mask = vh2 == kh2
    sc = GLA_SUB
    n_sub = c // sc
    trow = lax.broadcasted_iota(jnp.int32, (sc, GLA_QK_W), 0)
    qk_head = lax.broadcasted_iota(jnp.int32, (sc, GLA_QK_W), 1) // GLA_DK
    v_head = lax.broadcasted_iota(jnp.int32, (sc, GLA_V_W), 1) // GLA_DV
    key_col = lax.broadcasted_iota(jnp.int32, (GLA_HEADS * sc, c), 1)

    def chunks(ic, carry):
        r0 = pl.multiple_of(ic * c, c)
        seqs = range(n_seq)
        ld = lambda ref: [ref[bi, pl.ds(r0, c), :] for bi in seqs]
        q, k, g, v = ld(q_ref), ld(k_ref), ld(gl_ref), ld(v_ref)
        b = []
        for bi in seqs:
            g_hi, g_lo = _split_hilo(g[bi])
            b.append(_dot(ltri, g_hi) + _dot(ltri, g_lo))
        st = [st_scr[bi] for bi in seqs]
        o_inter = [_dot_nt((q[bi] * jnp.exp(b[bi])).astype(BF16), st[bi].astype(BF16)) for bi in seqs]
        v16 = [v[bi].astype(BF16) for bi in seqs]

        s4 = {}
        for i_sub in range(1, n_sub):
            rows = slice(i_sub * sc, (i_sub + 1) * sc)
            for bi in seqs:
                e = b[bi][i_sub * sc - 1:i_sub * sc, :]
                a = q[bi][rows] * jnp.exp(b[bi][rows] - e)
                a4 = jnp.concatenate([jnp.where(qk_head == h, a, 0.0) for h in range(GLA_HEADS)], axis=0)
                kd = (k[bi] * jnp.exp(jnp.minimum(e - b[bi], 0.0))).astype(BF16)
                s4[bi, i_sub] = jnp.where(key_col < i_sub * sc, _dot_nt(a4.astype(BF16), kd), 0.0)

        for i_sub in range(n_sub):
            rows = slice(i_sub * sc, (i_sub + 1) * sc)
            for bi in seqs:
                q_i, b_i, k_i = q[bi][rows], b[bi][rows], k[bi][rows]
                for s in range(sc):
                    sl = q_i * jnp.exp(jnp.minimum(b_i - b_i[s:s + 1], 0.0)) * k_i[s:s + 1]
                    off = (i_sub * sc + s) * sc
                    p_scr[bi, off:off + sc, :] = jnp.where(trow >= s, sl, 0.0)
        for bi in seqs:
            w_scr[bi] = _dot_hilo(p_scr[bi], head_ones)

        for bi in seqs:
            b_last = b[bi][c - 1:c, :]
            kt = (k[bi] * jnp.exp(b_last - b[bi])).astype(BF16)
            upd = _dot(v[bi].T.astype(BF16), kt)
            st_scr[bi] = st[bi] * jnp.exp(b_last) + jnp.where(bd_mask, upd, 0.0)

        o4 = {key: _dot(val.astype(BF16), v16[key[0]]) for key, val in s4.items()}

        for i_sub in range(n_sub):
            rows = slice(i_sub * sc, (i_sub + 1) * sc)
            for bi in seqs:
                o_i = o_inter[bi][rows]
                for s in range(sc):
                    off = (i_sub * sc + s) * sc
                    o_i = o_i + w_scr[bi, off:off + sc, :] * v[bi][i_sub * sc + s:i_sub * sc + s + 1, :]
                if i_sub > 0:
                    for h in range(GLA_HEADS):
                        o_i = o_i + jnp.where(v_head == h, o4[bi, i_sub][h * sc:(h + 1) * sc], 0.0)
                on = _head_rms(o_i, gn_ref[...], GLA_DV)
                dst = pl.ds(r0 + i_sub * sc, sc)
                o_ref[bi, dst, :] = (on * _silu(gg_ref[bi, dst, :])).astype(o_ref.dtype)
        return carry
    lax.fori_loop(0, n_chunks, chunks, 0)

    @pl.when(step == pl.num_programs(0) - 1)
    def _():
        sT_ref[...] = st_scr[...]


def _gla(gq, gk, gl, gv, gg, gnorm, s0T, *, batch, seq, chunk, chunks_per_step):
    rows = chunk * chunks_per_step
    row = lambda w: pl.BlockSpec((batch, rows, w), lambda i: (0, i, 0))
    state = pl.BlockSpec((batch, GLA_V_W, GLA_QK_W), lambda i: (0, 0, 0))
    r3 = lambda a: a.reshape(batch, seq, a.shape[-1])
    kern = functools.partial(_gla_kernel, chunk=chunk, n_chunks=chunks_per_step)
    o, st = pl.pallas_call(
        kern, grid=(seq // rows,),
        in_specs=[row(128), row(128), row(128), row(256), row(256),
                  pl.BlockSpec((1, GLA_V_W), lambda i: (0, 0)), state],
        out_specs=[row(256), state],
        out_shape=[jax.ShapeDtypeStruct((batch, seq, GLA_V_W), BF16),
                   jax.ShapeDtypeStruct((batch, GLA_V_W, GLA_QK_W), F32)],
        scratch_shapes=[pltpu.VMEM((batch, GLA_V_W, GLA_QK_W), F32),
                        pltpu.VMEM((batch, chunk * GLA_SUB, GLA_QK_W), F32),
                        pltpu.VMEM((batch, chunk * GLA_SUB, GLA_V_W), F32)],
        compiler_params=pltpu.CompilerParams(dimension_semantics=("arbitrary",), vmem_limit_bytes=VMEM_LIMIT),
        name="gla_prompt")(r3(gq), r3(gk), r3(gl), r3(gv), r3(gg), gnorm, s0T)
    return o.reshape(batch * seq, GLA_V_W), st


def _gla_sample_kernel(q_ref, k_ref, gl_ref, v_ref, gg_ref, gn_ref, s0_ref, o_ref, sT_ref):
    rows = (8, GLA_QK_W)
    decay = jnp.exp(gl_ref[0])
    r = lax.broadcasted_iota(jnp.int32, (GLA_V_W, GLA_V_W), 0)
    cc = lax.broadcasted_iota(jnp.int32, (GLA_V_W, GLA_V_W), 1)
    v_diag = jnp.where(r == cc, jnp.broadcast_to(v_ref[0], (GLA_V_W, GLA_V_W)), 0.0).astype(BF16)
    k_rows = jnp.broadcast_to(k_ref[0], (GLA_V_W, GLA_QK_W)).astype(BF16)
    vh = lax.broadcasted_iota(jnp.int32, (GLA_V_W, GLA_QK_W), 0) // GLA_DV
    kh = lax.broadcasted_iota(jnp.int32, (GLA_V_W, GLA_QK_W), 1) // GLA_DK
    outer = jnp.where(vh == kh, _dot(v_diag, k_rows), 0.0)
    st = s0_ref[0] * decay + outer
    sT_ref[0] = st
    q8 = jnp.broadcast_to(q_ref[0], rows).astype(BF16)
    o = _dot_nt(q8, st.astype(BF16))
    on = _head_rms(o, gn_ref[...], GLA_DV)
    o_ref[0] = (on * _silu(gg_ref[0]))[0:1].astype(o_ref.dtype)


def _gla_sample(gq, gk, gl, gv, gg, gnorm, s0T):
    n = gq.shape[0]
    r3 = lambda a: a.reshape(n, 1, a.shape[-1])
    row = lambda w: pl.BlockSpec((1, 1, w), lambda b: (b, 0, 0))
    st_spec = pl.BlockSpec((1, GLA_V_W, GLA_QK_W), lambda b: (b, 0, 0))
    o, st = pl.pallas_call(
        _gla_sample_kernel, grid=(n,),
        in_specs=[row(128), row(128), row(128), row(256), row(256),
                  pl.BlockSpec((1, GLA_V_W), lambda b: (0, 0)), st_spec],
        out_specs=[row(256), st_spec],
        out_shape=[jax.ShapeDtypeStruct((n, 1, GLA_V_W), BF16), jax.ShapeDtypeStruct((n, GLA_V_W, GLA_QK_W), F32)],
        compiler_params=pltpu.CompilerParams(dimension_semantics=("arbitrary",), vmem_limit_bytes=VMEM_LIMIT),
        name="gla_sample")(r3(gq), r3(gk), r3(gl), r3(gv), r3(gg), gnorm, s0T)
    return o.reshape(n, GLA_V_W), st


def _stack_heads(q_cols):
    lane = lax.broadcasted_iota(jnp.int32, q_cols[0].shape, 1)
    low = lane < HEAD_DIM
    zero = jnp.zeros_like(q_cols[0])
    parts = [jnp.where(low, qc, zero) for qc in q_cols] + [jnp.where(low, zero, qc) for qc in q_cols]
    return jnp.concatenate(parts, axis=0)


def _unstack_heads(o, rows):
    lane = lax.broadcasted_iota(jnp.int32, (rows, LANES), 1)
    low = lane < HEAD_DIM
    return [jnp.where(low, o[j * rows:(j + 1) * rows], o[(GROUP + j) * rows:(GROUP + j + 1) * rows])
            for j in range(GROUP)]


def _suffix_matrix():
    r = lax.broadcasted_iota(jnp.int32, (ATT_TILE, ATT_TILE), 0)
    c = lax.broadcasted_iota(jnp.int32, (ATT_TILE, ATT_TILE), 1)
    u = jnp.where(r > c, 1.0, 0.0).astype(BF16)
    return jnp.concatenate([u, u], axis=0)


def _twice(a):
    return jnp.concatenate([a, a], axis=1)


def _sb_rows(q, kt, v, u2, carry, mask):
    z = _dot(q, kt)
    l = _neg_softplus(z)
    if mask is not None:
        l = jnp.where(mask, l, 0.0)
    hi, lo = _split_hilo(l)
    c = _dot(jnp.concatenate([hi, lo], axis=1), u2)
    w = jnp.exp(z + l + c + _twice(carry))
    if mask is not None:
        w = jnp.where(mask, w, 0.0)
    return _dot(w.astype(BF16), v), carry + jnp.sum(l, axis=1, keepdims=True)


SB_STOP = -104.0


def _sb_prompt_kernel(q_ref, kt_ref, v_ref, o_ref, q_scr, carry_scr, acc_scr):
    i = pl.program_id(1)
    t = ATT_TILE
    q_scr[...] = _stack_heads([q_ref[0, :, 128 * j:128 * (j + 1)] for j in range(GROUP)])
    carry_scr[...] = jnp.zeros(carry_scr.shape, F32)
    acc_scr[...] = jnp.zeros(acc_scr.shape, F32)
    u2 = _suffix_matrix()
    rt = SB_ROW_TILE
    rr = lax.broadcasted_iota(jnp.int32, (rt, t), 0) & (t - 1)
    cc = lax.broadcasted_iota(jnp.int32, (rt, t), 1)

    def tile(kj, diagonal):
        kt = kt_ref[0, kj]
        v = v_ref[0, pl.ds(pl.multiple_of(kj * t, t), t), :]
        for r in range(ATT_ROWS // rt):
            rows = slice(r * rt, (r + 1) * rt)
            mask = cc < rr + (r * rt) % t if diagonal else None
            pv, carry = _sb_rows(q_scr[rows, :], kt, v, u2, carry_scr[rows, :], mask)
            acc_scr[rows, :] += pv
            carry_scr[rows, :] = carry

    def carry_max():
        return jnp.max(jnp.max(carry_scr[...], axis=0, keepdims=True), axis=1, keepdims=True)[0, 0]

    tile(i, True)

    def cond(state):
        kj, cmax = state
        return jnp.logical_and(kj >= 0, cmax > SB_STOP)

    def body(state):
        kj, _ = state
        tile(kj, False)
        return kj - 1, carry_max()
    lax.while_loop(cond, body, (i - 1, carry_max()))

    cols = _unstack_heads(acc_scr[...], t)
    for j in range(GROUP):
        o_ref[0, :, 128 * j:128 * (j + 1)] = cols[j].astype(o_ref.dtype)


def _sb_prompt(q, kt, v, *, batch, seq):
    t = ATT_TILE
    out = pl.pallas_call(
        _sb_prompt_kernel, grid=(batch, seq // t),
        in_specs=[pl.BlockSpec((1, t, ATT_Q_W), lambda b, i: (b, i, 0)),
                  pl.BlockSpec((1, seq // t, LANES, t), lambda b, i: (b, 0, 0, 0)),
                  pl.BlockSpec((1, seq, LANES), lambda b, i: (b, 0, 0))],
        out_specs=pl.BlockSpec((1, t, ATT_Q_W), lambda b, i: (b, i, 0)),
        out_shape=jax.ShapeDtypeStruct((batch, seq, ATT_Q_W), BF16),
        scratch_shapes=[pltpu.VMEM((ATT_ROWS, LANES), BF16), pltpu.VMEM((ATT_ROWS, LANES), F32),
                        pltpu.VMEM((ATT_ROWS, LANES), F32)],
        compiler_params=pltpu.CompilerParams(dimension_semantics=("arbitrary", "arbitrary"),
                                             vmem_limit_bytes=VMEM_LIMIT),
        name="sb_prompt")(q.reshape(batch, seq, ATT_Q_W), kt, v.reshape(batch, seq, LANES))
    return out.reshape(batch * seq, ATT_Q_W)


def _bias_of_dist(dist, rb_ref, h):
    bias = jnp.full(dist.shape, rb_ref[0, h], F32)
    for b in range(1, REL_BUCKETS):
        bias = jnp.where(dist >= _T5_THR[b], rb_ref[b, h], bias)
    return bias - rb_ref[REL_BUCKETS - 1, h]


def _bias_kernel(rb_ref, own_ref, prev_ref, samp_ref):
    t = ATT_TILE
    r = lax.broadcasted_iota(jnp.int32, (t, t), 0)
    c = lax.broadcasted_iota(jnp.int32, (t, t), 1)
    for h in range(ATT_HEADS):
        own = _bias_of_dist(jnp.maximum(r - c, 0), rb_ref, h)
        own_ref[h * t:(h + 1) * t, :] = jnp.where(c <= r, own, NEG_INF)
        prev_ref[h * t:(h + 1) * t, :] = _bias_of_dist(r - c + t, rb_ref, h)
    s = lax.broadcasted_iota(jnp.int32, (8, t), 1)
    hrow = lax.broadcasted_iota(jnp.int32, (8, t), 0)
    last = jnp.zeros((8, t), F32)
    self_b = jnp.zeros((8, LANES), F32)
    hrow2 = lax.broadcasted_iota(jnp.int32, (8, LANES), 0)
    for h in range(ATT_HEADS):
        last = jnp.where(hrow == h, _bias_of_dist(t - s, rb_ref, h), last)
        self_b = jnp.where(hrow2 == h, rb_ref[0, h] - rb_ref[REL_BUCKETS - 1, h], self_b)
    samp_ref[:, 0:t] = last
    samp_ref[:, t:t + LANES] = self_b


def _bias_tiles(rel_bias):
    t = ATT_TILE
    return pl.pallas_call(
        _bias_kernel,
        in_specs=[pl.BlockSpec(memory_space=pltpu.SMEM)],
        out_shape=[jax.ShapeDtypeStruct((ATT_ROWS, t), F32), jax.ShapeDtypeStruct((ATT_ROWS, t), F32),
                   jax.ShapeDtypeStruct((8, t + LANES), F32)],
        name="t5_bias_tiles")(rel_bias)


def _top3_select(gate, n_valid):
    lane_i = lax.broadcasted_iota(jnp.int32, gate.shape, 1)
    lane = lane_i.astype(F32)
    valid = lane_i < n_valid
    g = jnp.where(valid, gate, NEG_INF)
    sel = jnp.zeros(gate.shape, F32)
    for _ in range(MOBA_TOPK):
        m = jnp.max(g, axis=1, keepdims=True)
        idx = jnp.min(jnp.where(g == m, lane, float(LANES)), axis=1, keepdims=True)
        pick = lane == idx
        sel = jnp.where(pick, 1.0, sel)
        g = jnp.where(pick, -jnp.inf, g)
    return jnp.where(valid, sel, 0.0) > 0.5


def _moba_prompt_kernel(q_ref, kt_ref, v_ref, km_ref, own_ref, prev_ref, o_ref,
                        qx_scr, p_scr, al_scr, m_scr, acc_scr):
    i = pl.program_id(1)
    t = ATT_TILE
    q = _stack_heads([q_ref[0, :, 128 * j:128 * (j + 1)] for j in range(GROUP)])
    gate = _dot_nt(q, km_ref[0].astype(BF16))
    sel = _top3_select(gate, i)
    qx_scr[:, 0:LANES] = q
    qx_scr[:, LANES:2 * LANES] = jnp.where(sel, 0.0, NEG_INF).astype(BF16)
    blk_row = lax.broadcasted_iota(jnp.int32, (LANES, t), 0)
    ones = jnp.ones((t, LANES), BF16)
    rt = MOBA_ROW_TILE
    row_tiles = [slice(r * rt, (r + 1) * rt) for r in range(ATT_ROWS // rt)]

    def past_keys(kj):
        blk = jnp.where(kj >= 0, kj, LANES - 1)
        return jnp.concatenate([kt_ref[0, jnp.maximum(kj, 0)],
                                jnp.where(blk_row == blk, 1.0, 0.0).astype(BF16)], axis=0)

    def values(kj):
        v = v_ref[0, pl.ds(pl.multiple_of(jnp.maximum(kj, 0) * t, t), t), :]
        return jnp.concatenate([v, ones], axis=1)

    vx = values(i)
    for rows in row_tiles:
        s = _dot(qx_scr[rows, 0:LANES], kt_ref[0, i]) + own_ref[rows, :]
        m0 = jnp.broadcast_to(jnp.max(s, axis=1, keepdims=True), (rt, LANES))
        acc_scr[rows, :] = _dot(jnp.exp(s - _twice(m0)).astype(BF16), vx)
        m_scr[rows, :] = m0

    def pair_keys(kj):
        return jnp.concatenate([past_keys(kj), past_keys(kj - 1)], axis=1)

    def pair_values(kj):
        return jnp.concatenate([values(kj), values(kj - 1)], axis=0)

    def probabilities(kx, rows, with_prev_bias):
        s = _dot(qx_scr[rows, :], kx)
        if with_prev_bias:
            s = jnp.concatenate([s[:, 0:t] + prev_ref[rows, :], s[:, t:2 * t]], axis=1)
        m_old = m_scr[rows, :]
        m_new = jnp.maximum(m_old, jnp.broadcast_to(jnp.max(s, axis=1, keepdims=True), (rt, LANES)))
        p_scr[rows, :] = jnp.exp(s - _twice(_twice(m_new))).astype(BF16)
        al_scr[rows, :] = jnp.exp(m_old - m_new)
        m_scr[rows, :] = m_new

    def accumulate(vx, rows):
        acc_scr[rows, :] = _twice(al_scr[rows, :]) * acc_scr[rows, :] + _dot(p_scr[rows, :], vx)

    @pl.when(i >= 1)
    def _():
        kx = pair_keys(i - 1)
        for rows in row_tiles:
            probabilities(kx, rows, True)

    n_pairs = jnp.maximum(i - 1, 0) // 2

    def body(n, carry):
        kj = i - 3 - 2 * n
        vx, kx = pair_values(kj + 2), pair_keys(kj)
        for rows in row_tiles:
            accumulate(vx, rows)
            probabilities(kx, rows, False)
        return carry
    lax.fori_loop(0, n_pairs, body, 0)

    @pl.when(i >= 1)
    def _():
        vx = pair_values(i - 1 - 2 * n_pairs)
        for rows in row_tiles:
            accumulate(vx, rows)

    o = acc_scr[:, 0:LANES] / acc_scr[:, LANES:2 * LANES]
    cols = _unstack_heads(o, t)
    for j in range(GROUP):
        o_ref[0, :, 128 * j:128 * (j + 1)] = cols[j].astype(o_ref.dtype)


def _moba_prompt(q, kt, v, kmean, bias_own, bias_prev, *, batch, seq):
    t = ATT_TILE
    const = lambda a: pl.BlockSpec(a.shape, lambda b, i: (0, 0))
    out = pl.pallas_call(
        _moba_prompt_kernel, grid=(batch, seq // t),
        in_specs=[pl.BlockSpec((1, t, ATT_Q_W), lambda b, i: (b, i, 0)),
                  pl.BlockSpec((1, seq // t, LANES, t), lambda b, i: (b, 0, 0, 0)),
                  pl.BlockSpec((1, seq, LANES), lambda b, i: (b, 0, 0)),
                  pl.BlockSpec((1, LANES, LANES), lambda b, i: (b, 0, 0)),
                  const(bias_own), const(bias_prev)],
        out_specs=pl.BlockSpec((1, t, ATT_Q_W), lambda b, i: (b, i, 0)),
        out_shape=jax.ShapeDtypeStruct((batch, seq, ATT_Q_W), BF16),
        scratch_shapes=[pltpu.VMEM((ATT_ROWS, 2 * LANES), BF16), pltpu.VMEM((ATT_ROWS, 2 * t), BF16),
                        pltpu.VMEM((ATT_ROWS, LANES), F32), pltpu.VMEM((ATT_ROWS, LANES), F32),
                        pltpu.VMEM((ATT_ROWS, 2 * LANES), F32)],
        compiler_params=pltpu.CompilerParams(dimension_semantics=("arbitrary", "arbitrary"),
                                             vmem_limit_bytes=VMEM_LIMIT),
        name="moba_prompt")(q.reshape(batch, seq, ATT_Q_W), kt, v.reshape(batch, seq, LANES), kmean,
                            bias_own, bias_prev)
    return out.reshape(batch * seq, ATT_Q_W)


def _sample_q_rows(q_row):
    row = lax.broadcasted_iota(jnp.int32, (8, LANES), 0)
    lane = lax.broadcasted_iota(jnp.int32, (8, LANES), 1)
    qf = q_row.astype(F32)
    out = jnp.zeros((8, LANES), F32)
    for h in range(ATT_HEADS):
        j, n = h % GROUP, h // GROUP
        col = jnp.broadcast_to(qf[:, 128 * j:128 * (j + 1)], (8, LANES))
        half = lane >= HEAD_DIM if n == 1 else lane < HEAD_DIM
        out = jnp.where(jnp.logical_and(row == h, half), col, out)
    return out.astype(BF16)


def _sample_o_row(o):
    lane = lax.broadcasted_iota(jnp.int32, (1, LANES), 1)
    return [jnp.where(lane < HEAD_DIM, o[j:j + 1], o[GROUP + j:GROUP + j + 1]) for j in range(GROUP)]


TILES_PER_STEP = PAGES_PER_STEP * PAGE_SIZE // ATT_TILE


def _step_keys_values(pages):
    kt = jnp.concatenate([p[0, 0:128, :] for p in pages], axis=1).astype(BF16)
    vt = jnp.concatenate([p[0, 128:256, :] for p in pages], axis=1).astype(BF16)
    return kt, vt


def _tiles_to_rows(a):
    return jnp.concatenate([a[:, c * ATT_TILE:(c + 1) * ATT_TILE] for c in range(TILES_PER_STEP)], axis=0)


def _rows_to_tiles(a):
    return jnp.concatenate([a[8 * c:8 * (c + 1), :] for c in range(TILES_PER_STEP)], axis=1)


def _sb_sample_kernel(pt_ref, q_ref, *refs):
    del pt_ref
    pages = refs[:PAGES_PER_STEP]
    o_ref, carry_scr, acc_scr = refs[PAGES_PER_STEP:]
    step = pl.program_id(1)

    @pl.when(step == 0)
    def _():
        carry_scr[...] = jnp.zeros(carry_scr.shape, F32)
        acc_scr[...] = jnp.zeros(acc_scr.shape, F32)

    q = _sample_q_rows(q_ref[0])
    kt, vt = _step_keys_values(pages)
    z = _tiles_to_rows(_dot(q, kt))
    l = _neg_softplus(z)
    hi, lo = _split_hilo(l)
    c = _dot(jnp.concatenate([hi, lo], axis=1), _suffix_matrix())
    tile_sum = jnp.sum(l, axis=1, keepdims=True)
    carry = carry_scr[...]
    carries = [None] * TILES_PER_STEP
    for tl in reversed(range(TILES_PER_STEP)):
        carries[tl] = carry
        carry = carry + tile_sum[8 * tl:8 * (tl + 1)]
    w = jnp.exp(z + l + c + _twice(jnp.concatenate(carries, axis=0)))
    acc = acc_scr[...] + _dot_nt(_rows_to_tiles(w).astype(BF16), vt)
    carry_scr[...] = carry
    acc_scr[...] = acc

    @pl.when(step == pl.num_programs(1) - 1)
    def _():
        cols = _sample_o_row(acc)
        for j in range(GROUP):
            o_ref[0, :, 128 * j:128 * (j + 1)] = cols[j].astype(o_ref.dtype)


def _page_specs(layer, n_pool, n_steps, reverse):
    specs = []
    for p in range(PAGES_PER_STEP):
        def imap(b, c, pt, p=p):
            cc = (n_steps - 1 - c) if reverse else c
            return (layer * n_pool + pt[b, cc * PAGES_PER_STEP + p], 0, 0)
        specs.append(pl.BlockSpec((1, ATT_KV_W, PAGE_SIZE), imap))
    return specs


def _cache_pages(cache):
    d, n_pool = cache.shape[:2]
    return cache.transpose(0, 1, 3, 4, 5, 2).reshape(d * n_pool, ATT_KV_W, PAGE_SIZE)


def _sb_sample(q, cache, page_table, layer):
    nseq, n_pages = page_table.shape
    n_pool = cache.shape[1]
    n_steps = n_pages // PAGES_PER_STEP
    cache2 = _cache_pages(cache)
    grid_spec = pltpu.PrefetchScalarGridSpec(
        num_scalar_prefetch=1, grid=(nseq, n_steps),
        in_specs=[pl.BlockSpec((1, 1, ATT_Q_W), lambda b, c, pt: (b, 0, 0))]
        + _page_specs(layer, n_pool, n_steps, reverse=True),
        out_specs=pl.BlockSpec((1, 1, ATT_Q_W), lambda b, c, pt: (b, 0, 0)),
        scratch_shapes=[pltpu.VMEM((8, LANES), F32), pltpu.VMEM((8, LANES), F32)])
    out = pl.pallas_call(
        _sb_sample_kernel, grid_spec=grid_spec,
        out_shape=jax.ShapeDtypeStruct((nseq, 1, ATT_Q_W), BF16),
        compiler_params=pltpu.CompilerParams(dimension_semantics=("arbitrary", "arbitrary"),
                                             vmem_limit_bytes=VMEM_LIMIT),
        name="sb_sample")(page_table, q.reshape(nseq, 1, ATT_Q_W), *([cache2] * PAGES_PER_STEP))
    return out.reshape(nseq, ATT_Q_W)


def _moba_sample_kernel(pt_ref, q_ref, kvn_ref, sb_ref, *refs, n_blocks):
    del pt_ref
    pages = refs[:PAGES_PER_STEP]
    o_ref, gate_scr, m_scr, l_scr, acc_scr = refs[PAGES_PER_STEP:]
    step = pl.program_id(1)
    bps = TILES_PER_STEP
    rows = bps * 8

    q = _sample_q_rows(q_ref[0])
    kt, vt = _step_keys_values(pages)
    s = _tiles_to_rows(_dot(q, kt))
    g = jnp.mean(s, axis=1, keepdims=True)
    row_blk = lax.broadcasted_iota(jnp.int32, (rows, ATT_TILE), 0) // 8
    last = jnp.logical_and(step == pl.num_programs(1) - 1, row_blk == bps - 1)
    s = s + jnp.where(last, jnp.concatenate([sb_ref[:, 0:ATT_TILE]] * bps, axis=0), 0.0)
    m = jnp.max(s, axis=1, keepdims=True)
    p = jnp.exp(s - m)
    p_bd = jnp.concatenate([jnp.where(row_blk == c, p, 0.0) for c in range(bps)], axis=1).astype(BF16)
    dst = pl.ds(step * bps, bps)
    wide = lambda a: jnp.broadcast_to(a, (rows, LANES)).reshape(bps, 8, LANES)
    gate_scr[dst] = wide(g)
    m_scr[dst] = wide(m)
    l_scr[dst] = wide(jnp.sum(p, axis=1, keepdims=True))
    acc_scr[dst] = _dot_nt(p_bd, vt).reshape(bps, 8, LANES)

    @pl.when(step == pl.num_programs(1) - 1)
    def _():
        gates = [gate_scr[j] for j in range(n_blocks)]
        sel = [jnp.zeros((8, LANES), jnp.bool_)] * n_blocks
        for _ in range(MOBA_TOPK):
            best = functools.reduce(jnp.maximum, gates)
            idx = functools.reduce(jnp.minimum, [jnp.where(gates[j] == best, float(j), float(n_blocks))
                                                 for j in range(n_blocks)])
            for j in range(n_blocks):
                pick = idx == float(j)
                sel[j] = jnp.logical_or(sel[j], pick)
                gates[j] = jnp.where(pick, -jnp.inf, gates[j])
        kvn = kvn_ref[0]
        s_self = (jnp.sum(q.astype(F32) * kvn[:, 0:128].astype(F32), axis=1, keepdims=True)
                  + sb_ref[:, ATT_TILE:ATT_TILE + 1])
        m_tot = jnp.maximum(functools.reduce(jnp.maximum, [jnp.where(sel[j], m_scr[j], NEG_INF)
                                                           for j in range(n_blocks)]), s_self)
        p_self = jnp.exp(s_self - m_tot)
        denom = p_self
        o = p_self * kvn[:, 128:256].astype(F32)
        for j in range(n_blocks):
            coef = jnp.where(sel[j], jnp.exp(m_scr[j] - m_tot), 0.0)
            denom = denom + coef * l_scr[j]
            o = o + coef * acc_scr[j]
        cols = _sample_o_row(o / denom)
        for j in range(GROUP):
            o_ref[0, :, 128 * j:128 * (j + 1)] = cols[j].astype(o_ref.dtype)


def _moba_sample(q, kv_new, cache, page_table, bias_samp, layer):
    nseq, n_pages = page_table.shape
    n_pool = cache.shape[1]
    n_steps = n_pages // PAGES_PER_STEP
    n_blocks = n_pages * PAGE_SIZE // MOBA_BLOCK
    cache2 = _cache_pages(cache)
    grid_spec = pltpu.PrefetchScalarGridSpec(
        num_scalar_prefetch=1, grid=(nseq, n_steps),
        in_specs=[pl.BlockSpec((1, 1, ATT_Q_W), lambda b, c, pt: (b, 0, 0)),
                  pl.BlockSpec((1, 1, ATT_KV_W), lambda b, c, pt: (b, 0, 0)),
                  pl.BlockSpec(bias_samp.shape, lambda b, c, pt: (0, 0))]
        + _page_specs(layer, n_pool, n_steps, reverse=False),
        out_specs=pl.BlockSpec((1, 1, ATT_Q_W), lambda b, c, pt: (b, 0, 0)),
        scratch_shapes=[pltpu.VMEM((n_blocks, 8, LANES), F32)] * 4)
    out = pl.pallas_call(
        functools.partial(_moba_sample_kernel, n_blocks=n_blocks), grid_spec=grid_spec,
        out_shape=jax.ShapeDtypeStruct((nseq, 1, ATT_Q_W), BF16),
        compiler_params=pltpu.CompilerParams(dimension_semantics=("arbitrary", "arbitrary"),
                                             vmem_limit_bytes=VMEM_LIMIT),
        name="moba_sample")(page_table, q.reshape(nseq, 1, ATT_Q_W), kv_new.reshape(nseq, 1, ATT_KV_W),
                            bias_samp, *([cache2] * PAGES_PER_STEP))
    return out.reshape(nseq, ATT_Q_W)


FF_CHUNK = 1408


def _ffn_kernel(x_ref, og_ref, om_ref, os_ref, wog_ref, wom_ref, wos_ref, g2_ref, wg_ref, wu_ref, wd_ref, y_ref,
                h2_scr):
    @pl.when(pl.program_id(1) == 0)
    def _():
        x1 = (x_ref[...] + _dot(og_ref[...], wog_ref[...]) + _dot(om_ref[...], wom_ref[...])
              + _dot(os_ref[...], wos_ref[...]))
        ms = jnp.mean(x1 * x1, axis=-1, keepdims=True)
        h2_scr[...] = (x1 * lax.rsqrt(ms + RMS_EPS) * g2_ref[...]).astype(BF16)
        y_ref[...] = x1

    h2 = h2_scr[...]
    a = _silu(_dot(h2, wg_ref[...])) * _dot(h2, wu_ref[...])
    y_ref[...] += _dot(a.astype(BF16), wd_ref[...])


def _ffn(x, og, om, osb, lw, tm):
    n = x.shape[0]
    row = lambda w: pl.BlockSpec((tm, w), lambda i, f: (i, 0))
    full = lambda a: pl.BlockSpec(a.shape, lambda i, f: (0, 0))
    ws = [lw['wo_g'], lw['wo_m'], lw['wo_s'], lw['g2'], lw['w_gate'], lw['w_up'], lw['w_down']]
    w_specs = [full(w) for w in ws[:4]] + [pl.BlockSpec((D_MODEL, FF_CHUNK), lambda i, f: (0, f)),
                                           pl.BlockSpec((D_MODEL, FF_CHUNK), lambda i, f: (0, f)),
                                           pl.BlockSpec((FF_CHUNK, D_MODEL), lambda i, f: (f, 0))]
    return pl.pallas_call(
        _ffn_kernel, grid=(n // tm, D_FF // FF_CHUNK),
        in_specs=[row(D_MODEL), row(GLA_V_W), row(ATT_Q_W), row(ATT_Q_W)] + w_specs,
        out_specs=row(D_MODEL), out_shape=jax.ShapeDtypeStruct((n, D_MODEL), F32),
        scratch_shapes=[pltpu.VMEM((tm, D_MODEL), BF16)],
        compiler_params=pltpu.CompilerParams(dimension_semantics=("arbitrary", "arbitrary"),
                                             vmem_limit_bytes=VMEM_LIMIT),
        name=f"ffn_{tm}")(x, og, om, osb, *ws)


def _prep_layer(l, norm1, w_in, w_alpha, b_alpha, gla_norm, moba_q_norm, moba_k_norm, sb_q_norm, sb_k_norm,
                w_out, norm2, w_gate_up, w_down):
    wi = w_in[l]
    o = _OFF
    cols = [wi[:, o['gq']:o['ga']],
            jnp.pad(wi[:, o['ga']:o['mq']], ((0, 0), (0, LANES - GLA_RANK))),
            wi[:, o['mq']:o['mk']][:, _HEAD_PERM], wi[:, o['mk']:o['sq']],
            wi[:, o['sq']:o['sk']][:, _HEAD_PERM], wi[:, o['sk']:]]
    wo = w_out[l]
    tile = lambda g, reps: jnp.tile(g[l], reps).reshape(1, -1)
    return dict(
        g1=norm1[l].reshape(1, -1), w_in=jnp.concatenate(cols, axis=1).astype(BF16),
        w_alpha=jnp.pad(w_alpha[l], ((0, LANES - GLA_RANK), (0, 0))).astype(BF16),
        b_alpha=b_alpha[l].reshape(1, -1),
        gnorm=tile(gla_norm, GLA_HEADS), mqg=tile(moba_q_norm, ATT_HEADS), mkg=tile(moba_k_norm, KV_HEADS),
        sqg=tile(sb_q_norm, ATT_HEADS), skg=tile(sb_k_norm, KV_HEADS),
        wo_g=wo[0:256].astype(BF16), wo_m=wo[256:640][_HEAD_PERM].astype(BF16),
        wo_s=wo[640:1024][_HEAD_PERM].astype(BF16), g2=norm2[l].reshape(1, -1),
        w_gate=w_gate_up[l][:, :D_FF].astype(BF16), w_up=w_gate_up[l][:, D_FF:].astype(BF16),
        w_down=w_down[l].astype(BF16))


def _state_to_blockdiag_T(s):
    b = s.shape[0]
    eye = jnp.eye(GLA_HEADS, dtype=s.dtype)
    return jnp.einsum('bhkv,hg->bhvgk', s, eye).reshape(b, GLA_V_W, GLA_QK_W)


def _blockdiag_T_to_state(st):
    b = st.shape[0]
    s5 = st.reshape(b, GLA_HEADS, GLA_DV, GLA_HEADS, GLA_DK)
    return jnp.stack([s5[:, h, :, h, :] for h in range(GLA_HEADS)], axis=1).transpose(0, 1, 3, 2)


def kernel(x_prompt, x_sample, cache_moba_kv, cache_sb_kv, state_gla, page_table, rel_bias, norm1, w_in, w_alpha,
           b_alpha, gla_norm, moba_q_norm, moba_k_norm, sb_q_norm, sb_k_norm, w_out, norm2, w_gate_up, w_down):
    nb, seq, _ = x_prompt.shape
    ns = x_sample.shape[0]
    n_prompt = nb * seq
    tm = 512
    assert seq % (2 * MOBA_BLOCK) == 0 and x_sample.shape[1] == 1 and seq // MOBA_BLOCK <= LANES
    bias_own, bias_prev, bias_samp = _bias_tiles(rel_bias)
    xp = x_prompt.reshape(n_prompt, D_MODEL)
    xs = x_sample.reshape(ns, D_MODEL)
    zero_state = jnp.zeros((nb, GLA_V_W, GLA_QK_W), F32)
    outs = dict(pm=[], ps=[], pg=[], sm=[], ss=[], sg=[])
    for l in range(DEPTH):
        lw = _prep_layer(l, norm1, w_in, w_alpha, b_alpha, gla_norm, moba_q_norm, moba_k_norm, sb_q_norm,
                         sb_k_norm, w_out, norm2, w_gate_up, w_down)
        gq, gk, gl, gv, gg, mq, mkv32, mkt, mv, mkm, sq, skv32, skt, sv = _inproj(xp, lw, tm, seq=seq)
        og, st = _gla(gq, gk, gl, gv, gg, lw['gnorm'], zero_state, batch=nb, seq=seq, chunk=GLA_CHUNK,
                      chunks_per_step=8)
        osb = _sb_prompt(sq, skt, sv, batch=nb, seq=seq)
        kmean = jnp.pad(mkm.reshape(nb, seq // MOBA_BLOCK, LANES), ((0, 0), (0, LANES - seq // MOBA_BLOCK), (0, 0)))
        om = _moba_prompt(mq, mkt, mv, kmean, bias_own, bias_prev, batch=nb, seq=seq)
        xp = _ffn(xp, og, om, osb, lw, tm)
        leaf = lambda a: a.reshape(nb, 2, KV_HEADS, HEAD_DIM, seq).transpose(0, 4, 1, 2, 3)
        outs['pm'].append(leaf(mkv32))
        outs['ps'].append(leaf(skv32))
        outs['pg'].append(_blockdiag_T_to_state(st))
        gq, gk, gl, gv, gg, mq, mkv32, mkv16, sq, skv32, _ = _inproj(xs, lw, ns)
        og, st = _gla_sample(gq, gk, gl, gv, gg, lw['gnorm'], _state_to_blockdiag_T(state_gla[l]))
        osb = _sb_sample(sq, cache_sb_kv, page_table, l)
        om = _moba_sample(mq, mkv16, cache_moba_kv, page_table, bias_samp, l)
        xs = _ffn(xs, og, om, osb, lw, ns)
        outs['sm'].append(mkv32.reshape(ns, 1, 2, KV_HEADS, HEAD_DIM))
        outs['ss'].append(skv32.reshape(ns, 1, 2, KV_HEADS, HEAD_DIM))
        outs['sg'].append(_blockdiag_T_to_state(st))
    return (xp.reshape(nb, seq, D_MODEL), xs.reshape(ns, 1, D_MODEL), jnp.stack(outs['pm']), jnp.stack(outs['ps']),
            jnp.stack(outs['pg']), jnp.stack(outs['sm']), jnp.stack(outs['ss']), jnp.stack(outs['sg']))
```

```python
import functools
import math

import jax
import jax.numpy as jnp
import numpy as np
from jax import lax
from jax.experimental import pallas as pl
from jax.experimental.pallas import tpu as pltpu

F32 = jnp.float32
BF16 = jnp.bfloat16

D_MODEL = 1024
DEPTH = 4
HEAD_DIM = 64
GLA_HEADS = 4
GLA_DK = 32
GLA_DV = 64
GLA_RANK = 16
GLA_TAU = 16.0
GLA_CHUNK = 64
GLA_SUB = 16
ATT_HEADS = 6
KV_HEADS = 2
GROUP = ATT_HEADS // KV_HEADS
MOBA_BLOCK = 256
MOBA_TOPK = 3
REL_BUCKETS = 32
REL_MAX_DIST = 128
RMS_EPS = 1e-6
NEG_INF = -1e30
PAGE_SIZE = 128
D_FF = 2816
GLA_QK_W = GLA_HEADS * GLA_DK
GLA_V_W = GLA_HEADS * GLA_DV
ATT_Q_W = ATT_HEADS * HEAD_DIM
ATT_KV_W = 2 * KV_HEADS * HEAD_DIM
LANES = 128
ATT_TILE = 256
ATT_ROWS = ATT_HEADS * ATT_TILE
SB_ROW_TILE = 1536
MOBA_ROW_TILE = 128
VMEM_LIMIT = 56 * 1024 * 1024
PAGES_PER_STEP = 16

_OFF = dict(gq=0, gk=128, gv=256, gg=512, ga=768, mq=784, mk=1168, mv=1296, sq=1424, sk=1808, sv=1936)
IN_W_PAD = 2176
_HEAD_PERM = np.concatenate([np.concatenate([np.arange(64) + 64 * j, np.arange(64) + 64 * (GROUP + j)])
                             for j in range(GROUP)])


def _t5_thresholds():
    n = np.arange(0, 4 * REL_MAX_DIST, dtype=np.int64)
    max_exact = REL_BUCKETS // 2
    nf = np.maximum(n, 1).astype(np.float32)
    large = max_exact + (np.log(nf / np.float32(max_exact)) / np.float32(math.log(REL_MAX_DIST / max_exact))
                         * np.float32(REL_BUCKETS - max_exact)).astype(np.int32)
    bucket = np.where(n < max_exact, n, np.minimum(large, REL_BUCKETS - 1))
    return [int(np.argmax(bucket >= b)) for b in range(REL_BUCKETS)]


_T5_THR = _t5_thresholds()


def _dot(a, b):
    return jnp.dot(a, b, preferred_element_type=F32)


def _dot_nt(a, b):
    return lax.dot_general(a, b, (((1,), (1,)), ((), ())), preferred_element_type=F32)


def _split_hilo(a):
    hi = a.astype(BF16)
    lo = (a - hi.astype(F32)).astype(BF16)
    return hi, lo


def _dot_hilo(a, b_bf16):
    hi, lo = _split_hilo(a)
    return _dot(hi, b_bf16) + _dot(lo, b_bf16)


def _group_mean_matrix(width, group):
    r = lax.broadcasted_iota(jnp.int32, (width, width), 0) // group
    c = lax.broadcasted_iota(jnp.int32, (width, width), 1) // group
    return jnp.where(r == c, 1.0 / group, 0.0).astype(BF16)


def _neg_softplus(z):
    return -(jnp.maximum(z, 0.0) + jnp.log(1.0 + jnp.exp(-jnp.abs(z))))


def _log_sigmoid(x):
    return jnp.minimum(x, 0.0) - jnp.log(1.0 + jnp.exp(-jnp.abs(x)))


def _silu(x):
    return x / (1.0 + jnp.exp(-x))


def _head_rms(x, gain, group):
    ms = _dot_hilo(x * x, _group_mean_matrix(x.shape[1], group))
    return x * lax.rsqrt(ms + RMS_EPS) * gain


def _inproj_kernel(x_ref, g1_ref, w_ref, wa_ref, ba_ref, mqg_ref, mkg_ref, sqg_ref, skg_ref,
                   gq_ref, gk_ref, gl_ref, gv_ref, gg_ref, *att_refs, prompt):
    x = x_ref[...]
    ms = jnp.mean(x * x, axis=-1, keepdims=True)
    h = (x * lax.rsqrt(ms + RMS_EPS) * g1_ref[...]).astype(BF16)
    p = _dot(h, w_ref[...])
    gq_ref[...] = p[:, 0:128] * (GLA_DK ** -0.5)
    gk_ref[...] = p[:, 128:256]
    gv_ref[...] = p[:, 256:512]
    gg_ref[...] = p[:, 512:768]
    alpha = _dot(p[:, 768:896].astype(BF16), wa_ref[...]) + ba_ref[...]
    gl_ref[...] = _log_sigmoid(alpha) / GLA_TAU

    def attn_group(base, qg_ref, kg_ref, refs, tiles_of_k=True):
        q = _head_rms(p[:, base:base + ATT_Q_W], qg_ref[...], HEAD_DIM) * (HEAD_DIM ** -0.5)
        refs[0][...] = q.astype(BF16)
        k = _head_rms(p[:, base + 384:base + 512], kg_ref[...], HEAD_DIM)
        v = p[:, base + 512:base + 640]
        if prompt:
            _, kv32t_ref, tiles_ref, rows_ref = refs
            kt, vt = k.T, v.T
            kv32t_ref[0, 0:128, :] = kt
            kv32t_ref[0, 128:256, :] = vt
            tiled = (kt if tiles_of_k else vt).astype(BF16)
            for c in range(tiles_ref.shape[1]):
                tiles_ref[0, c] = tiled[:, c * ATT_TILE:(c + 1) * ATT_TILE]
            rows_ref[...] = (v if tiles_of_k else k).astype(BF16)
        else:
            _, kv32_ref, kv16_ref = refs
            kv32_ref[:, 0:128] = k
            kv32_ref[:, 128:256] = v
            kv16_ref[:, 0:128] = k.astype(BF16)
            kv16_ref[:, 128:256] = v.astype(BF16)
        return k

    if prompt:
        mk = attn_group(896, mqg_ref, mkg_ref, att_refs[0:4], tiles_of_k=False)
        attn_group(1536, sqg_ref, skg_ref, att_refs[5:9])
        mkm_ref = att_refs[4]
        for j in range(mkm_ref.shape[1]):
            mkm_ref[0, j:j + 1, :] = jnp.mean(mk[j * MOBA_BLOCK:(j + 1) * MOBA_BLOCK], axis=0, keepdims=True)
    else:
        attn_group(896, mqg_ref, mkg_ref, att_refs[0:3])
        attn_group(1536, sqg_ref, skg_ref, att_refs[3:6])


def _inproj(x, lw, tm, seq=None):
    n = x.shape[0]
    grid = n // tm
    row = lambda w: pl.BlockSpec((tm, w), lambda i: (i, 0))
    full = lambda a: pl.BlockSpec(a.shape, lambda i: (0,) * a.ndim)
    sds = jax.ShapeDtypeStruct
    outs = [sds((n, 128), F32), sds((n, 128), F32), sds((n, 128), F32), sds((n, 256), F32), sds((n, 256), F32)]
    out_specs = [row(128), row(128), row(128), row(256), row(256)]
    if seq is None:
        group = [(sds((n, ATT_Q_W), BF16), row(ATT_Q_W)), (sds((n, ATT_KV_W), F32), row(ATT_KV_W)),
                 (sds((n, ATT_KV_W), BF16), row(ATT_KV_W))]
        att = group + group
    else:
        tiles = seq // tm
        nblk = tm // ATT_TILE
        group = [(sds((n, ATT_Q_W), BF16), row(ATT_Q_W)),
                 (sds((n // seq, ATT_KV_W, seq), F32),
                  pl.BlockSpec((1, ATT_KV_W, tm), lambda i: (i // tiles, 0, i % tiles))),
                 (sds((n // seq, seq // ATT_TILE, 128, ATT_TILE), BF16),
                  pl.BlockSpec((1, nblk, 128, ATT_TILE), lambda i: (i // tiles, i % tiles, 0, 0))),
                 (sds((n, 128), BF16), row(128))]
        means = (sds((grid, nblk, 128), F32), pl.BlockSpec((1, nblk, 128), lambda i: (i, 0, 0)))
        att = group + [means] + group
    outs += [a for a, _ in att]
    out_specs += [b for _, b in att]
    ins = [x, lw['g1'], lw['w_in'], lw['w_alpha'], lw['b_alpha'], lw['mqg'], lw['mkg'], lw['sqg'], lw['skg']]
    in_specs = [row(D_MODEL)] + [full(a) for a in ins[1:]]
    return pl.pallas_call(
        functools.partial(_inproj_kernel, prompt=seq is not None), grid=(grid,),
        in_specs=in_specs, out_specs=out_specs, out_shape=outs,
        compiler_params=pltpu.CompilerParams(dimension_semantics=("arbitrary",), vmem_limit_bytes=VMEM_LIMIT),
        name=f"inproj_{tm}")(*ins)


def _gla_kernel(q_ref, k_ref, gl_ref, v_ref, gg_ref, gn_ref, s0_ref, o_ref, sT_ref,
                st_scr, p_scr, w_scr, *, chunk, n_chunks):
    c = chunk
    step = pl.program_id(0)
    n_seq = q_ref.shape[0]

    @pl.when(step == 0)
    def _():
        st_scr[...] = s0_ref[...]

    ri = lax.broadcasted_iota(jnp.int32, (c, c), 0)
    ci = lax.broadcasted_iota(jnp.int32, (c, c), 1)
    ltri = jnp.where(ri >= ci, 1.0, 0.0).astype(BF16)
    kh = lax.broadcasted_iota(jnp.int32, (GLA_QK_W, GLA_V_W), 0) // GLA_DK
    vh = lax.broadcasted_iota(jnp.int32, (GLA_QK_W, GLA_V_W), 1) // GLA_DV
    head_ones = jnp.where(kh == vh, 1.0, 0.0).astype(BF16)
    vh2 = lax.broadcasted_iota(jnp.int32, (GLA_V_W, GLA_QK_W), 0) // GLA_DV
    kh2 = lax.broadcasted_iota(jnp.int32, (GLA_V_W, GLA_QK_W), 1) // GLA_DK
    bd_mask = vh2 == kh2
    sc = GLA_SUB
    n_sub = c // sc
    trow = lax.broadcasted_iota(jnp.int32, (sc, GLA_QK_W), 0)
    qk_head = lax.broadcasted_iota(jnp.int32, (sc, GLA_QK_W), 1) // GLA_DK
    v_head = lax.broadcasted_iota(jnp.int32, (sc, GLA_V_W), 1) // GLA_DV
    key_col = lax.broadcasted_iota(jnp.int32, (GLA_HEADS * sc, c), 1)

    def chunks(ic, carry):
        r0 = pl.multiple_of(ic * c, c)
        seqs = range(n_seq)
        ld = lambda ref: [ref[bi, pl.ds(r0, c), :] for bi in seqs]
        q, k, g, v = ld(q_ref), ld(k_ref), ld(gl_ref), ld(v_ref)
        b = []
        for bi in seqs:
            g_hi, g_lo = _split_hilo(g[bi])
            b.append(_dot(ltri, g_hi) + _dot(ltri, g_lo))
        st = [st_scr[bi] for bi in seqs]
        o_inter = [_dot_nt((q[bi] * jnp.exp(b[bi])).astype(BF16), st[bi].astype(BF16)) for bi in seqs]
        v16 = [v[bi].astype(BF16) for bi in seqs]

        s4 = {}
        for i_sub in range(1, n_sub):
            rows = slice(i_sub * sc, (i_sub + 1) * sc)
            for bi in seqs:
                e = b[bi][i_sub * sc - 1:i_sub * sc, :]
                a = q[bi][rows] * jnp.exp(b[bi][rows] - e)
                a4 = jnp.concatenate([jnp.where(qk_head == h, a, 0.0) for h in range(GLA_HEADS)], axis=0)
                kd = (k[bi] * jnp.exp(jnp.minimum(e - b[bi], 0.0))).astype(BF16)
                s4[bi, i_sub] = jnp.where(key_col < i_sub * sc, _dot_nt(a4.astype(BF16), kd), 0.0)

        for i_sub in range(n_sub):
            rows = slice(i_sub * sc, (i_sub + 1) * sc)
            for bi in seqs:
                q_i, b_i, k_i = q[bi][rows], b[bi][rows], k[bi][rows]
                for s in range(sc):
                    sl = q_i * jnp.exp(jnp.minimum(b_i - b_i[s:s + 1], 0.0)) * k_i[s:s + 1]
                    off = (i_sub * sc + s) * sc
                    p_scr[bi, off:off + sc, :] = jnp.where(trow >= s, sl, 0.0)
        for bi in seqs:
            w_scr[bi] = _dot_hilo(p_scr[bi], head_ones)

        for bi in seqs:
            b_last = b[bi][c - 1:c, :]
            kt = (k[bi] * jnp.exp(b_last - b[bi])).astype(BF16)
            upd = _dot(v[bi].T.astype(BF16), kt)
            st_scr[bi] = st[bi] * jnp.exp(b_last) + jnp.where(bd_mask, upd, 0.0)

        o4 = {key: _dot(val.astype(BF16), v16[key[0]]) for key, val in s4.items()}

        for i_sub in range(n_sub):
            rows = slice(i_sub * sc, (i_sub + 1) * sc)
            for bi in seqs:
                o_i = o_inter[bi][rows]
                for s in range(sc):
                    off = (i_sub * sc + s) * sc
                    o_i = o_i + w_scr[bi, off:off + sc, :] * v[bi][i_sub * sc + s:i_sub * sc + s + 1, :]
                if i_sub > 0:
                    for h in range(GLA_HEADS):
                        o_i = o_i + jnp.where(v_head == h, o4[bi, i_sub][h * sc:(h + 1) * sc], 0.0)
                on = _head_rms(o_i, gn_ref[...], GLA_DV)
                dst = pl.ds(r0 + i_sub * sc, sc)
                o_ref[bi, dst, :] = (on * _silu(gg_ref[bi, dst, :])).astype(o_ref.dtype)
        return carry
    lax.fori_loop(0, n_chunks, chunks, 0)

    @pl.when(step == pl.num_programs(0) - 1)
    def _():
        sT_ref[...] = st_scr[...]


def _gla(gq, gk, gl, gv, gg, gnorm, s0T, *, batch, seq, chunk, chunks_per_step):
    rows = chunk * chunks_per_step
    row = lambda w: pl.BlockSpec((batch, rows, w), lambda i: (0, i, 0))
    state = pl.BlockSpec((batch, GLA_V_W, GLA_QK_W), lambda i: (0, 0, 0))
    r3 = lambda a: a.reshape(batch, seq, a.shape[-1])
    kern = functools.partial(_gla_kernel, chunk=chunk, n_chunks=chunks_per_step)
    o, st = pl.pallas_call(
        kern, grid=(seq // rows,),
        in_specs=[row(128), row(128), row(128), row(256), row(256),
                  pl.BlockSpec((1, GLA_V_W), lambda i: (0, 0)), state],
        out_specs=[row(256), state],
        out_shape=[jax.ShapeDtypeStruct((batch, seq, GLA_V_W), BF16),
                   jax.ShapeDtypeStruct((batch, GLA_V_W, GLA_QK_W), F32)],
        scratch_shapes=[pltpu.VMEM((batch, GLA_V_W, GLA_QK_W), F32),
                        pltpu.VMEM((batch, chunk * GLA_SUB, GLA_QK_W), F32),
                        pltpu.VMEM((batch, chunk * GLA_SUB, GLA_V_W), F32)],
        compiler_params=pltpu.CompilerParams(dimension_semantics=("arbitrary",), vmem_limit_bytes=VMEM_LIMIT),
        name="gla_prompt")(r3(gq), r3(gk), r3(gl), r3(gv), r3(gg), gnorm, s0T)
    return o.reshape(batch * seq, GLA_V_W), st


def _gla_sample_kernel(q_ref, k_ref, gl_ref, v_ref, gg_ref, gn_ref, s0_ref, o_ref, sT_ref):
    rows = (8, GLA_QK_W)
    decay = jnp.exp(gl_ref[0])
    r = lax.broadcasted_iota(jnp.int32, (GLA_V_W, GLA_V_W), 0)
    cc = lax.broadcasted_iota(jnp.int32, (GLA_V_W, GLA_V_W), 1)
    v_diag = jnp.where(r == cc, jnp.broadcast_to(v_ref[0], (GLA_V_W, GLA_V_W)), 0.0).astype(BF16)
    k_rows = jnp.broadcast_to(k_ref[0], (GLA_V_W, GLA_QK_W)).astype(BF16)
    vh = lax.broadcasted_iota(jnp.int32, (GLA_V_W, GLA_QK_W), 0) // GLA_DV
    kh = lax.broadcasted_iota(jnp.int32, (GLA_V_W, GLA_QK_W), 1) // GLA_DK
    outer = jnp.where(vh == kh, _dot(v_diag, k_rows), 0.0)
    st = s0_ref[0] * decay + outer
    sT_ref[0] = st
    q8 = jnp.broadcast_to(q_ref[0], rows).astype(BF16)
    o = _dot_nt(q8, st.astype(BF16))
    on = _head_rms(o, gn_ref[...], GLA_DV)
    o_ref[0] = (on * _silu(gg_ref[0]))[0:1].astype(o_ref.dtype)


def _gla_sample(gq, gk, gl, gv, gg, gnorm, s0T):
    n = gq.shape[0]
    r3 = lambda a: a.reshape(n, 1, a.shape[-1])
    row = lambda w: pl.BlockSpec((1, 1, w), lambda b: (b, 0, 0))
    st_spec = pl.BlockSpec((1, GLA_V_W, GLA_QK_W), lambda b: (b, 0, 0))
    o, st = pl.pallas_call(
        _gla_sample_kernel, grid=(n,),
        in_specs=[row(128), row(128), row(128), row(256), row(256),
                  pl.BlockSpec((1, GLA_V_W), lambda b: (0, 0)), st_spec],
        out_specs=[row(256), st_spec],
        out_shape=[jax.ShapeDtypeStruct((n, 1, GLA_V_W), BF16), jax.ShapeDtypeStruct((n, GLA_V_W, GLA_QK_W), F32)],
        compiler_params=pltpu.CompilerParams(dimension_semantics=("arbitrary",), vmem_limit_bytes=VMEM_LIMIT),
        name="gla_sample")(r3(gq), r3(gk), r3(gl), r3(gv), r3(gg), gnorm, s0T)
    return o.reshape(n, GLA_V_W), st


def _stack_heads(q_cols):
    lane = lax.broadcasted_iota(jnp.int32, q_cols[0].shape, 1)
    low = lane < HEAD_DIM
    zero = jnp.zeros_like(q_cols[0])
    parts = [jnp.where(low, qc, zero) for qc in q_cols] + [jnp.where(low, zero, qc) for qc in q_cols]
    return jnp.concatenate(parts, axis=0)


def _unstack_heads(o, rows):
    lane = lax.broadcasted_iota(jnp.int32, (rows, LANES), 1)
    low = lane < HEAD_DIM
    return [jnp.where(low, o[j * rows:(j + 1) * rows], o[(GROUP + j) * rows:(GROUP + j + 1) * rows])
            for j in range(GROUP)]


def _suffix_matrix():
    r = lax.broadcasted_iota(jnp.int32, (ATT_TILE, ATT_TILE), 0)
    c = lax.broadcasted_iota(jnp.int32, (ATT_TILE, ATT_TILE), 1)
    u = jnp.where(r > c, 1.0, 0.0).astype(BF16)
    return jnp.concatenate([u, u], axis=0)


def _twice(a):
    return jnp.concatenate([a, a], axis=1)


def _sb_rows(q, kt, v, u2, carry, mask):
    z = _dot(q, kt)
    l = _neg_softplus(z)
    if mask is not None:
        l = jnp.where(mask, l, 0.0)
    hi, lo = _split_hilo(l)
    c = _dot(jnp.concatenate([hi, lo], axis=1), u2)
    w = jnp.exp(z + l + c + _twice(carry))
    if mask is not None:
        w = jnp.where(mask, w, 0.0)
    return _dot(w.astype(BF16), v), carry + jnp.sum(l, axis=1, keepdims=True)


SB_STOP = -104.0


def _sb_prompt_kernel(q_ref, kt_ref, v_ref, o_ref, q_scr, carry_scr, acc_scr):
    i = pl.program_id(1)
    t = ATT_TILE
    q_scr[...] = _stack_heads([q_ref[0, :, 128 * j:128 * (j + 1)] for j in range(GROUP)])
    carry_scr[...] = jnp.zeros(carry_scr.shape, F32)
    acc_scr[...] = jnp.zeros(acc_scr.shape, F32)
    u2 = _suffix_matrix()
    rt = SB_ROW_TILE
    rr = lax.broadcasted_iota(jnp.int32, (rt, t), 0) & (t - 1)
    cc = lax.broadcasted_iota(jnp.int32, (rt, t), 1)

    def tile(kj, diagonal):
        kt = kt_ref[0, kj]
        v = v_ref[0, pl.ds(pl.multiple_of(kj * t, t), t), :]
        for r in range(ATT_ROWS // rt):
            rows = slice(r * rt, (r + 1) * rt)
            mask = cc < rr + (r * rt) % t if diagonal else None
            pv, carry = _sb_rows(q_scr[rows, :], kt, v, u2, carry_scr[rows, :], mask)
            acc_scr[rows, :] += pv
            carry_scr[rows, :] = carry

    def carry_max():
        return jnp.max(jnp.max(carry_scr[...], axis=0, keepdims=True), axis=1, keepdims=True)[0, 0]

    tile(i, True)

    def cond(state):
        kj, cmax = state
        return jnp.logical_and(kj >= 0, cmax > SB_STOP)

    def body(state):
        kj, _ = state
        tile(kj, False)
        return kj - 1, carry_max()
    lax.while_loop(cond, body, (i - 1, carry_max()))

    cols = _unstack_heads(acc_scr[...], t)
    for j in range(GROUP):
        o_ref[0, :, 128 * j:128 * (j + 1)] = cols[j].astype(o_ref.dtype)


def _sb_prompt(q, kt, v, *, batch, seq):
    t = ATT_TILE
    out = pl.pallas_call(
        _sb_prompt_kernel, grid=(batch, seq // t),
        in_specs=[pl.BlockSpec((1, t, ATT_Q_W), lambda b, i: (b, i, 0)),
                  pl.BlockSpec((1, seq // t, LANES, t), lambda b, i: (b, 0, 0, 0)),
                  pl.BlockSpec((1, seq, LANES), lambda b, i: (b, 0, 0))],
        out_specs=pl.BlockSpec((1, t, ATT_Q_W), lambda b, i: (b, i, 0)),
        out_shape=jax.ShapeDtypeStruct((batch, seq, ATT_Q_W), BF16),
        scratch_shapes=[pltpu.VMEM((ATT_ROWS, LANES), BF16), pltpu.VMEM((ATT_ROWS, LANES), F32),
                        pltpu.VMEM((ATT_ROWS, LANES), F32)],
        compiler_params=pltpu.CompilerParams(dimension_semantics=("arbitrary", "arbitrary"),
                                             vmem_limit_bytes=VMEM_LIMIT),
        name="sb_prompt")(q.reshape(batch, seq, ATT_Q_W), kt, v.reshape(batch, seq, LANES))
    return out.reshape(batch * seq, ATT_Q_W)


def _bias_of_dist(dist, rb_ref, h):
    bias = jnp.full(dist.shape, rb_ref[0, h], F32)
    for b in range(1, REL_BUCKETS):
        bias = jnp.where(dist >= _T5_THR[b], rb_ref[b, h], bias)
    return bias - rb_ref[REL_BUCKETS - 1, h]


def _bias_kernel(rb_ref, own_ref, prev_ref, samp_ref):
    t = ATT_TILE
    key = lax.broadcasted_iota(jnp.int32, (t, t), 0)
    qry = lax.broadcasted_iota(jnp.int32, (t, t), 1)
    for h in range(ATT_HEADS):
        own = _bias_of_dist(jnp.maximum(qry - key, 0), rb_ref, h)
        own_ref[:, h * t:(h + 1) * t] = jnp.where(key <= qry, own, NEG_INF)
        prev_ref[:, h * t:(h + 1) * t] = _bias_of_dist(qry - key + t, rb_ref, h)
    s = lax.broadcasted_iota(jnp.int32, (8, t), 1)
    hrow = lax.broadcasted_iota(jnp.int32, (8, t), 0)
    last = jnp.zeros((8, t), F32)
    self_b = jnp.zeros((8, LANES), F32)
    hrow2 = lax.broadcasted_iota(jnp.int32, (8, LANES), 0)
    for h in range(ATT_HEADS):
        last = jnp.where(hrow == h, _bias_of_dist(t - s, rb_ref, h), last)
        self_b = jnp.where(hrow2 == h, rb_ref[0, h] - rb_ref[REL_BUCKETS - 1, h], self_b)
    samp_ref[:, 0:t] = last
    samp_ref[:, t:t + LANES] = self_b


def _bias_tiles(rel_bias):
    t = ATT_TILE
    return pl.pallas_call(
        _bias_kernel,
        in_specs=[pl.BlockSpec(memory_space=pltpu.SMEM)],
        out_shape=[jax.ShapeDtypeStruct((t, ATT_ROWS), F32), jax.ShapeDtypeStruct((t, ATT_ROWS), F32),
                   jax.ShapeDtypeStruct((8, t + LANES), F32)],
        name="t5_bias_tiles")(rel_bias)


def _top3_select(gate, n_valid, axis):
    blk_i = lax.broadcasted_iota(jnp.int32, gate.shape, axis)
    blk = blk_i.astype(F32)
    valid = blk_i < n_valid
    g = jnp.where(valid, gate, NEG_INF)
    sel = jnp.zeros(gate.shape, F32)
    for _ in range(MOBA_TOPK):
        m = jnp.max(g, axis=axis, keepdims=True)
        idx = jnp.min(jnp.where(g == m, blk, float(LANES)), axis=axis, keepdims=True)
        pick = blk == idx
        sel = jnp.where(pick, 1.0, sel)
        g = jnp.where(pick, -jnp.inf, g)
    return jnp.where(valid, sel, 0.0) > 0.5


MOBA_COL_TILE = 256
MOBA_ACC_ROWS = LANES + 16


def _moba_prompt_kernel(q_ref, k_ref, vt_ref, km_ref, own_ref, prev_ref, o_ref,
                        qx_scr, s_scr, p_scr, al_scr, m_scr, acc_scr):
    i = pl.program_id(1)
    t = ATT_TILE
    q = _stack_heads([q_ref[0, :, 128 * j:128 * (j + 1)] for j in range(GROUP)])
    qt = q.astype(F32).T.astype(BF16)
    n_blk = k_ref.shape[1] // t
    gate = _dot(km_ref[0, 0:n_blk, :].astype(BF16), qt)
    sel = _top3_select(gate, i, 0)
    qx_scr[0:LANES, :] = qt
    qx_scr[LANES:LANES + n_blk, :] = jnp.where(sel, 0.0, NEG_INF).astype(BF16)
    qx_scr[LANES + n_blk:2 * LANES, :] = jnp.full((LANES - n_blk, ATT_ROWS), NEG_INF, BF16)
    blk_col = lax.broadcasted_iota(jnp.int32, (t, LANES), 1)
    ones = jnp.ones((MOBA_ACC_ROWS - LANES, 2 * t), BF16)
    ct = MOBA_COL_TILE
    col_tiles = [slice(c * ct, (c + 1) * ct) for c in range(ATT_ROWS // ct)]

    def keys(kj):
        return k_ref[0, pl.ds(pl.multiple_of(jnp.maximum(kj, 0) * t, t), t), :]

    def past_keys(kj):
        blk = jnp.where(kj >= 0, kj, LANES - 1)
        return jnp.concatenate([keys(kj), jnp.where(blk_col == blk, 1.0, 0.0).astype(BF16)], axis=1)

    def pair_keys(kj):
        return jnp.concatenate([past_keys(kj), past_keys(kj - 1)], axis=0)

    def pair_values(kj):
        clamp = lambda j: jnp.clip(j, 0, n_blk - 1)
        vt = jnp.concatenate([vt_ref[0, clamp(kj)], vt_ref[0, clamp(kj - 1)]], axis=1)
        return jnp.concatenate([vt, ones], axis=0)

    k_own = keys(i)
    vx = jnp.concatenate([vt_ref[0, i], ones[:, 0:t]], axis=0)
    for cols in col_tiles:
        s = _dot(k_own, qx_scr[0:LANES, cols]) + own_ref[:, cols]
        m0 = jnp.max(s, axis=0, keepdims=True)
        acc_scr[:, cols] = _dot(vx, jnp.exp(s - m0).astype(BF16))
        m_scr[:, cols] = jnp.broadcast_to(m0, (8, ct))

    kx = pair_keys(i - 1)
    for cols in col_tiles:
        s = _dot(kx, qx_scr[:, cols])
        s_scr[:, cols] = jnp.concatenate([s[0:t] + prev_ref[:, cols], s[t:2 * t]], axis=0)
    p_scr[...] = jnp.zeros(p_scr.shape, BF16)
    al_scr[...] = jnp.ones(al_scr.shape, F32)
    n_pairs = (i + 1) // 2

    def body(n, carry):
        kj = i - 1 - 2 * n
        kx = pair_keys(jnp.where(n < n_pairs, kj, -1))
        vx = pair_values(kj + 4)
        for cols in col_tiles:
            acc_scr[:, cols] = al_scr[0:1, cols] * acc_scr[:, cols] + _dot(vx, p_scr[:, cols])
            s = s_scr[:, cols]
            s_scr[:, cols] = _dot(kx, qx_scr[:, cols])
            m_old = m_scr[0:1, cols]
            m_new = jnp.maximum(m_old, jnp.max(s, axis=0, keepdims=True))
            p_scr[:, cols] = jnp.exp(s - m_new).astype(BF16)
            al_scr[:, cols] = jnp.broadcast_to(jnp.exp(m_old - m_new), (8, ct))
            m_scr[:, cols] = jnp.broadcast_to(m_new, (8, ct))
        return carry
    lax.fori_loop(1, n_pairs + 2, body, 0)

    o = (acc_scr[0:LANES, :] / acc_scr[LANES:LANES + 1, :]).T
    cols = _unstack_heads(o, t)
    for j in range(GROUP):
        o_ref[0, :, 128 * j:128 * (j + 1)] = cols[j].astype(o_ref.dtype)


def _moba_prompt(q, k, vt, kmean, bias_own, bias_prev, *, batch, seq):
    t = ATT_TILE
    const = lambda a: pl.BlockSpec(a.shape, lambda b, i: (0, 0))
    out = pl.pallas_call(
        _moba_prompt_kernel, grid=(batch, seq // t),
        in_specs=[pl.BlockSpec((1, t, ATT_Q_W), lambda b, i: (b, i, 0)),
                  pl.BlockSpec((1, seq, LANES), lambda b, i: (b, 0, 0)),
                  pl.BlockSpec((1, seq // t, LANES, t), lambda b, i: (b, 0, 0, 0)),
                  pl.BlockSpec((1, LANES, LANES), lambda b, i: (b, 0, 0)),
                  const(bias_own), const(bias_prev)],
        out_specs=pl.BlockSpec((1, t, ATT_Q_W), lambda b, i: (b, i, 0)),
        out_shape=jax.ShapeDtypeStruct((batch, seq, ATT_Q_W), BF16),
        scratch_shapes=[pltpu.VMEM((2 * LANES, ATT_ROWS), BF16), pltpu.VMEM((2 * t, ATT_ROWS), F32),
                        pltpu.VMEM((2 * t, ATT_ROWS), BF16),
                        pltpu.VMEM((8, ATT_ROWS), F32), pltpu.VMEM((8, ATT_ROWS), F32),
                        pltpu.VMEM((MOBA_ACC_ROWS, ATT_ROWS), F32)],
        compiler_params=pltpu.CompilerParams(dimension_semantics=("arbitrary", "arbitrary"),
                                             vmem_limit_bytes=VMEM_LIMIT),
        name="moba_prompt")(q.reshape(batch, seq, ATT_Q_W), k.reshape(batch, seq, LANES), vt, kmean,
                            bias_own, bias_prev)
    return out.reshape(batch * seq, ATT_Q_W)


def _sample_q_rows(q_row):
    row = lax.broadcasted_iota(jnp.int32, (8, LANES), 0)
    lane = lax.broadcasted_iota(jnp.int32, (8, LANES), 1)
    qf = q_row.astype(F32)
    out = jnp.zeros((8, LANES), F32)
    for h in range(ATT_HEADS):
        j, n = h % GROUP, h // GROUP
        col = jnp.broadcast_to(qf[:, 128 * j:128 * (j + 1)], (8, LANES))
        half = lane >= HEAD_DIM if n == 1 else lane < HEAD_DIM
        out = jnp.where(jnp.logical_and(row == h, half), col, out)
    return out.astype(BF16)


def _sample_o_row(o):
    lane = lax.broadcasted_iota(jnp.int32, (1, LANES), 1)
    return [jnp.where(lane < HEAD_DIM, o[j:j + 1], o[GROUP + j:GROUP + j + 1]) for j in range(GROUP)]


TILES_PER_STEP = PAGES_PER_STEP * PAGE_SIZE // ATT_TILE


def _step_keys_values(pages):
    kt = jnp.concatenate([p[0, 0:128, :] for p in pages], axis=1).astype(BF16)
    vt = jnp.concatenate([p[0, 128:256, :] for p in pages], axis=1).astype(BF16)
    return kt, vt


def _tiles_to_rows(a):
    return jnp.concatenate([a[:, c * ATT_TILE:(c + 1) * ATT_TILE] for c in range(TILES_PER_STEP)], axis=0)


def _rows_to_tiles(a):
    return jnp.concatenate([a[8 * c:8 * (c + 1), :] for c in range(TILES_PER_STEP)], axis=1)


def _sb_sample_kernel(pt_ref, q_ref, *refs):
    del pt_ref
    pages = refs[:PAGES_PER_STEP]
    o_ref, carry_ref, carry_scr, acc_scr = refs[PAGES_PER_STEP:]
    step = pl.program_id(1)

    @pl.when(step == 0)
    def _():
        carry_scr[...] = jnp.zeros(carry_scr.shape, F32)
        acc_scr[...] = jnp.zeros(acc_scr.shape, F32)

    q = _sample_q_rows(q_ref[0])
    kt, vt = _step_keys_values(pages)
    z = _tiles_to_rows(_dot(q, kt))
    l = _neg_softplus(z)
    hi, lo = _split_hilo(l)
    c = _dot(jnp.concatenate([hi, lo], axis=1), _suffix_matrix())
    tile_sum = jnp.sum(l, axis=1, keepdims=True)
    carry = carry_scr[...]
    carries = [None] * TILES_PER_STEP
    for tl in reversed(range(TILES_PER_STEP)):
        carries[tl] = carry
        carry = carry + tile_sum[8 * tl:8 * (tl + 1)]
    w = jnp.exp(z + l + c + _twice(jnp.concatenate(carries, axis=0)))
    acc = acc_scr[...] + _dot_nt(_rows_to_tiles(w).astype(BF16), vt)
    carry_scr[...] = carry
    acc_scr[...] = acc

    @pl.when(step == pl.num_programs(1) - 1)
    def _():
        cols = _sample_o_row(acc)
        for j in range(GROUP):
            o_ref[0, :, 128 * j:128 * (j + 1)] = cols[j].astype(o_ref.dtype)
        carry_ref[0] = carry


def _page_specs(layer, n_pool, n_steps, reverse):
    specs = []
    for p in range(PAGES_PER_STEP):
        def imap(b, c, pt, p=p):
            cc = (n_steps - 1 - c) if reverse else c
            return (layer * n_pool + pt[b, cc * PAGES_PER_STEP + p], 0, 0)
        specs.append(pl.BlockSpec((1, ATT_KV_W, PAGE_SIZE), imap))
    return specs


def _cache_pages(cache):
    d, n_pool = cache.shape[:2]
    return cache.transpose(0, 1, 3, 4, 5, 2).reshape(d * n_pool, ATT_KV_W, PAGE_SIZE)


def _sb_sample(q, cache, page_table, layer):
    nseq, n_pages = page_table.shape
    n_pool = cache.shape[1]
    n_steps = n_pages // PAGES_PER_STEP
    cache2 = _cache_pages(cache)
    q3 = q.reshape(nseq, 1, ATT_Q_W)

    def walk(steps):
        grid_spec = pltpu.PrefetchScalarGridSpec(
            num_scalar_prefetch=1, grid=(nseq, steps),
            in_specs=[pl.BlockSpec((1, 1, ATT_Q_W), lambda b, c, pt: (b, 0, 0))]
            + _page_specs(layer, n_pool, n_steps, reverse=True),
            out_specs=[pl.BlockSpec((1, 1, ATT_Q_W), lambda b, c, pt: (b, 0, 0)),
                       pl.BlockSpec((1, 8, LANES), lambda b, c, pt: (b, 0, 0))],
            scratch_shapes=[pltpu.VMEM((8, LANES), F32), pltpu.VMEM((8, LANES), F32)])
        return pl.pallas_call(
            _sb_sample_kernel, grid_spec=grid_spec,
            out_shape=[jax.ShapeDtypeStruct((nseq, 1, ATT_Q_W), BF16),
                       jax.ShapeDtypeStruct((nseq, 8, LANES), F32)],
            compiler_params=pltpu.CompilerParams(dimension_semantics=("arbitrary", "arbitrary"),
                                                 vmem_limit_bytes=VMEM_LIMIT),
            name=f"sb_sample_{steps}")(page_table, q3, *([cache2] * PAGES_PER_STEP))

    out, carry = walk(1)
    out = lax.cond(jnp.max(carry) < SB_STOP, lambda: out, lambda: walk(n_steps)[0])
    return out.reshape(nseq, ATT_Q_W)


def _moba_sample_kernel(pt_ref, q_ref, kvn_ref, sb_ref, *refs, n_blocks):
    del pt_ref
    pages = refs[:PAGES_PER_STEP]
    o_ref, gate_scr, m_scr, l_scr, acc_scr = refs[PAGES_PER_STEP:]
    step = pl.program_id(1)
    bps = TILES_PER_STEP
    rows = bps * 8

    q = _sample_q_rows(q_ref[0])
    kt, vt = _step_keys_values(pages)
    s = _tiles_to_rows(_dot(q, kt))
    g = jnp.mean(s, axis=1, keepdims=True)
    row_blk = lax.broadcasted_iota(jnp.int32, (rows, ATT_TILE), 0) // 8
    last = jnp.logical_and(step == pl.num_programs(1) - 1, row_blk == bps - 1)
    s = s + jnp.where(last, jnp.concatenate([sb_ref[:, 0:ATT_TILE]] * bps, axis=0), 0.0)
    m = jnp.max(s, axis=1, keepdims=True)
    p = jnp.exp(s - m)
    p_bd = jnp.concatenate([jnp.where(row_blk == c, p, 0.0) for c in range(bps)], axis=1).astype(BF16)
    dst = pl.ds(step * bps, bps)
    wide = lambda a: jnp.broadcast_to(a, (rows, LANES)).reshape(bps, 8, LANES)
    gate_scr[dst] = wide(g)
    m_scr[dst] = wide(m)
    l_scr[dst] = wide(jnp.sum(p, axis=1, keepdims=True))
    acc_scr[dst] = _dot_nt(p_bd, vt).reshape(bps, 8, LANES)

    @pl.when(step == pl.num_programs(1) - 1)
    def _():
        gates = [gate_scr[j] for j in range(n_blocks)]
        sel = [jnp.zeros((8, LANES), jnp.bool_)] * n_blocks
        for _ in range(MOBA_TOPK):
            best = functools.reduce(jnp.maximum, gates)
            idx = functools.reduce(jnp.minimum, [jnp.where(gates[j] == best, float(j), float(n_blocks))
                                                 for j in range(n_blocks)])
            for j in range(n_blocks):
                pick = idx == float(j)
                sel[j] = jnp.logical_or(sel[j], pick)
                gates[j] = jnp.where(pick, -jnp.inf, gates[j])
        kvn = kvn_ref[0]
        s_self = (jnp.sum(q.astype(F32) * kvn[:, 0:128].astype(F32), axis=1, keepdims=True)
                  + sb_ref[:, ATT_TILE:ATT_TILE + 1])
        m_tot = jnp.maximum(functools.reduce(jnp.maximum, [jnp.where(sel[j], m_scr[j], NEG_INF)
                                                           for j in range(n_blocks)]), s_self)
        p_self = jnp.exp(s_self - m_tot)
        denom = p_self
        o = p_self * kvn[:, 128:256].astype(F32)
        for j in range(n_blocks):
            coef = jnp.where(sel[j], jnp.exp(m_scr[j] - m_tot), 0.0)
            denom = denom + coef * l_scr[j]
            o = o + coef * acc_scr[j]
        cols = _sample_o_row(o / denom)
        for j in range(GROUP):
            o_ref[0, :, 128 * j:128 * (j + 1)] = cols[j].astype(o_ref.dtype)


def _moba_sample(q, kv_new, cache, page_table, bias_samp, layer):
    nseq, n_pages = page_table.shape
    n_pool = cache.shape[1]
    n_steps = n_pages // PAGES_PER_STEP
    n_blocks = n_pages * PAGE_SIZE // MOBA_BLOCK
    cache2 = _cache_pages(cache)
    grid_spec = pltpu.PrefetchScalarGridSpec(
        num_scalar_prefetch=1, grid=(nseq, n_steps),
        in_specs=[pl.BlockSpec((1, 1, ATT_Q_W), lambda b, c, pt: (b, 0, 0)),
                  pl.BlockSpec((1, 1, ATT_KV_W), lambda b, c, pt: (b, 0, 0)),
                  pl.BlockSpec(bias_samp.shape, lambda b, c, pt: (0, 0))]
        + _page_specs(layer, n_pool, n_steps, reverse=False),
        out_specs=pl.BlockSpec((1, 1, ATT_Q_W), lambda b, c, pt: (b, 0, 0)),
        scratch_shapes=[pltpu.VMEM((n_blocks, 8, LANES), F32)] * 4)
    out = pl.pallas_call(
        functools.partial(_moba_sample_kernel, n_blocks=n_blocks), grid_spec=grid_spec,
        out_shape=jax.ShapeDtypeStruct((nseq, 1, ATT_Q_W), BF16),
        compiler_params=pltpu.CompilerParams(dimension_semantics=("arbitrary", "arbitrary"),
                                             vmem_limit_bytes=VMEM_LIMIT),
        name="moba_sample")(page_table, q.reshape(nseq, 1, ATT_Q_W), kv_new.reshape(nseq, 1, ATT_KV_W),
                            bias_samp, *([cache2] * PAGES_PER_STEP))
    return out.reshape(nseq, ATT_Q_W)


FF_CHUNK = 1408


def _ffn_kernel(x_ref, og_ref, om_ref, os_ref, wog_ref, wom_ref, wos_ref, g2_ref, wg_ref, wu_ref, wd_ref, y_ref,
                h2_scr):
    @pl.when(pl.program_id(1) == 0)
    def _():
        x1 = (x_ref[...] + _dot(og_ref[...], wog_ref[...]) + _dot(om_ref[...], wom_ref[...])
              + _dot(os_ref[...], wos_ref[...]))
        ms = jnp.mean(x1 * x1, axis=-1, keepdims=True)
        h2_scr[...] = (x1 * lax.rsqrt(ms + RMS_EPS) * g2_ref[...]).astype(BF16)
        y_ref[...] = x1

    h2 = h2_scr[...]
    a = _silu(_dot(h2, wg_ref[...])) * _dot(h2, wu_ref[...])
    y_ref[...] += _dot(a.astype(BF16), wd_ref[...])


def _ffn(x, og, om, osb, lw, tm):
    n = x.shape[0]
    row = lambda w: pl.BlockSpec((tm, w), lambda i, f: (i, 0))
    full = lambda a: pl.BlockSpec(a.shape, lambda i, f: (0, 0))
    ws = [lw['wo_g'], lw['wo_m'], lw['wo_s'], lw['g2'], lw['w_gate'], lw['w_up'], lw['w_down']]
    w_specs = [full(w) for w in ws[:4]] + [pl.BlockSpec((D_MODEL, FF_CHUNK), lambda i, f: (0, f)),
                                           pl.BlockSpec((D_MODEL, FF_CHUNK), lambda i, f: (0, f)),
                                           pl.BlockSpec((FF_CHUNK, D_MODEL), lambda i, f: (f, 0))]
    return pl.pallas_call(
        _ffn_kernel, grid=(n // tm, D_FF // FF_CHUNK),
        in_specs=[row(D_MODEL), row(GLA_V_W), row(ATT_Q_W), row(ATT_Q_W)] + w_specs,
        out_specs=row(D_MODEL), out_shape=jax.ShapeDtypeStruct((n, D_MODEL), F32),
        scratch_shapes=[pltpu.VMEM((tm, D_MODEL), BF16)],
        compiler_params=pltpu.CompilerParams(dimension_semantics=("arbitrary", "arbitrary"),
                                             vmem_limit_bytes=VMEM_LIMIT),
        name=f"ffn_{tm}")(x, og, om, osb, *ws)


def _prep_layer(l, norm1, w_in, w_alpha, b_alpha, gla_norm, moba_q_norm, moba_k_norm, sb_q_norm, sb_k_norm,
                w_out, norm2, w_gate_up, w_down):
    wi = w_in[l]
    o = _OFF
    cols = [wi[:, o['gq']:o['ga']],
            jnp.pad(wi[:, o['ga']:o['mq']], ((0, 0), (0, LANES - GLA_RANK))),
            wi[:, o['mq']:o['mk']][:, _HEAD_PERM], wi[:, o['mk']:o['sq']],
            wi[:, o['sq']:o['sk']][:, _HEAD_PERM], wi[:, o['sk']:]]
    wo = w_out[l]
    tile = lambda g, reps: jnp.tile(g[l], reps).reshape(1, -1)
    return dict(
        g1=norm1[l].reshape(1, -1), w_in=jnp.concatenate(cols, axis=1).astype(BF16),
        w_alpha=jnp.pad(w_alpha[l], ((0, LANES - GLA_RANK), (0, 0))).astype(BF16),
        b_alpha=b_alpha[l].reshape(1, -1),
        gnorm=tile(gla_norm, GLA_HEADS), mqg=tile(moba_q_norm, ATT_HEADS), mkg=tile(moba_k_norm, KV_HEADS),
        sqg=tile(sb_q_norm, ATT_HEADS), skg=tile(sb_k_norm, KV_HEADS),
        wo_g=wo[0:256].astype(BF16), wo_m=wo[256:640][_HEAD_PERM].astype(BF16),
        wo_s=wo[640:1024][_HEAD_PERM].astype(BF16), g2=norm2[l].reshape(1, -1),
        w_gate=w_gate_up[l][:, :D_FF].astype(BF16), w_up=w_gate_up[l][:, D_FF:].astype(BF16),
        w_down=w_down[l].astype(BF16))


def _state_to_blockdiag_T(s):
    b = s.shape[0]
    eye = jnp.eye(GLA_HEADS, dtype=s.dtype)
    return jnp.einsum('bhkv,hg->bhvgk', s, eye).reshape(b, GLA_V_W, GLA_QK_W)


def _blockdiag_T_to_state(st):
    b = st.shape[0]
    s5 = st.reshape(b, GLA_HEADS, GLA_DV, GLA_HEADS, GLA_DK)
    return jnp.stack([s5[:, h, :, h, :] for h in range(GLA_HEADS)], axis=1).transpose(0, 1, 3, 2)


def kernel(x_prompt, x_sample, cache_moba_kv, cache_sb_kv, state_gla, page_table, rel_bias, norm1, w_in, w_alpha,
           b_alpha, gla_norm, moba_q_norm, moba_k_norm, sb_q_norm, sb_k_norm, w_out, norm2, w_gate_up, w_down):
    nb, seq, _ = x_prompt.shape
    ns = x_sample.shape[0]
    n_prompt = nb * seq
    tm = 512
    assert seq % (2 * MOBA_BLOCK) == 0 and x_sample.shape[1] == 1 and seq // MOBA_BLOCK <= LANES
    bias_own, bias_prev, bias_samp = _bias_tiles(rel_bias)
    xp = x_prompt.reshape(n_prompt, D_MODEL)
    xs = x_sample.reshape(ns, D_MODEL)
    zero_state = jnp.zeros((nb, GLA_V_W, GLA_QK_W), F32)
    outs = dict(pm=[], ps=[], pg=[], sm=[], ss=[], sg=[])
    for l in range(DEPTH):
        lw = _prep_layer(l, norm1, w_in, w_alpha, b_alpha, gla_norm, moba_q_norm, moba_k_norm, sb_q_norm,
                         sb_k_norm, w_out, norm2, w_gate_up, w_down)
        gq, gk, gl, gv, gg, mq, mkv32, mvt, mk, mkm, sq, skv32, skt, sv = _inproj(xp, lw, tm, seq=seq)
        og, st = _gla(gq, gk, gl, gv, gg, lw['gnorm'], zero_state, batch=nb, seq=seq, chunk=GLA_CHUNK,
                      chunks_per_step=8)
        osb = _sb_prompt(sq, skt, sv, batch=nb, seq=seq)
        kmean = jnp.pad(mkm.reshape(nb, seq // MOBA_BLOCK, LANES), ((0, 0), (0, LANES - seq // MOBA_BLOCK), (0, 0)))
        om = _moba_prompt(mq, mk, mvt, kmean, bias_own, bias_prev, batch=nb, seq=seq)
        xp = _ffn(xp, og, om, osb, lw, tm)
        leaf = lambda a: a.reshape(nb, 2, KV_HEADS, HEAD_DIM, seq).transpose(0, 4, 1, 2, 3)
        outs['pm'].append(leaf(mkv32))
        outs['ps'].append(leaf(skv32))
        outs['pg'].append(_blockdiag_T_to_state(st))
        gq, gk, gl, gv, gg, mq, mkv32, mkv16, sq, skv32, _ = _inproj(xs, lw, ns)
        og, st = _gla_sample(gq, gk, gl, gv, gg, lw['gnorm'], _state_to_blockdiag_T(state_gla[l]))
        osb = _sb_sample(sq, cache_sb_kv, page_table, l)
        om = _moba_sample(mq, mkv16, cache_moba_kv, page_table, bias_samp, l)
        xs = _ffn(xs, og, om, osb, lw, ns)
        outs['sm'].append(mkv32.reshape(ns, 1, 2, KV_HEADS, HEAD_DIM))
        outs['ss'].append(skv32.reshape(ns, 1, 2, KV_HEADS, HEAD_DIM))
        outs['sg'].append(_blockdiag_T_to_state(st))
    return (xp.reshape(nb, seq, D_MODEL), xs.reshape(ns, 1, D_MODEL), jnp.stack(outs['pm']), jnp.stack(outs['ps']),
            jnp.stack(outs['pg']), jnp.stack(outs['sm']), jnp.stack(outs['ss']), jnp.stack(outs['sg']))
```

```python
import functools
import math

import jax
import jax.numpy as jnp
import numpy as np
from jax import lax
from jax.experimental import pallas as pl
from jax.experimental.pallas import tpu as pltpu

F32 = jnp.float32
BF16 = jnp.bfloat16

D_MODEL = 1024
DEPTH = 4
HEAD_DIM = 64
GLA_HEADS = 4
GLA_DK = 32
GLA_DV = 64
GLA_RANK = 16
GLA_TAU = 16.0
GLA_CHUNK = 64
GLA_SUB = 16
ATT_HEADS = 6
KV_HEADS = 2
GROUP = ATT_HEADS // KV_HEADS
MOBA_BLOCK = 256
MOBA_TOPK = 3
REL_BUCKETS = 32
REL_MAX_DIST = 128
RMS_EPS = 1e-6
NEG_INF = -1e30
PAGE_SIZE = 128
D_FF = 2816
GLA_QK_W = GLA_HEADS * GLA_DK
GLA_V_W = GLA_HEADS * GLA_DV
ATT_Q_W = ATT_HEADS * HEAD_DIM
ATT_KV_W = 2 * KV_HEADS * HEAD_DIM
LANES = 128
ATT_TILE = 256
ATT_ROWS = ATT_HEADS * ATT_TILE
SB_ROW_TILE = 1536
MOBA_ROW_TILE = 128
VMEM_LIMIT = 56 * 1024 * 1024
PAGES_PER_STEP = 16
MOBA_PAGES_PER_STEP = 32

_OFF = dict(gq=0, gk=128, gv=256, gg=512, ga=768, mq=784, mk=1168, mv=1296, sq=1424, sk=1808, sv=1936)
IN_W_PAD = 2176
_HEAD_PERM = np.concatenate([np.concatenate([np.arange(64) + 64 * j, np.arange(64) + 64 * (GROUP + j)])
                             for j in range(GROUP)])


def _t5_thresholds():
    n = np.arange(0, 4 * REL_MAX_DIST, dtype=np.int64)
    max_exact = REL_BUCKETS // 2
    nf = np.maximum(n, 1).astype(np.float32)
    large = max_exact + (np.log(nf / np.float32(max_exact)) / np.float32(math.log(REL_MAX_DIST / max_exact))
                         * np.float32(REL_BUCKETS - max_exact)).astype(np.int32)
    bucket = np.where(n < max_exact, n, np.minimum(large, REL_BUCKETS - 1))
    return [int(np.argmax(bucket >= b)) for b in range(REL_BUCKETS)]


_T5_THR = _t5_thresholds()


def _dot(a, b):
    return jnp.dot(a, b, preferred_element_type=F32)


def _dot_nt(a, b):
    return lax.dot_general(a, b, (((1,), (1,)), ((), ())), preferred_element_type=F32)


def _split_hilo(a):
    hi = a.astype(BF16)
    lo = (a - hi.astype(F32)).astype(BF16)
    return hi, lo


def _dot_hilo(a, b_bf16):
    hi, lo = _split_hilo(a)
    return _dot(hi, b_bf16) + _dot(lo, b_bf16)


def _group_mean_matrix(width, group):
    r = lax.broadcasted_iota(jnp.int32, (width, width), 0) // group
    c = lax.broadcasted_iota(jnp.int32, (width, width), 1) // group
    return jnp.where(r == c, 1.0 / group, 0.0).astype(BF16)


def _neg_softplus(z):
    return -(jnp.maximum(z, 0.0) + jnp.log(1.0 + jnp.exp(-jnp.abs(z))))


def _log_sigmoid(x):
    return jnp.minimum(x, 0.0) - jnp.log(1.0 + jnp.exp(-jnp.abs(x)))


def _silu(x):
    return x / (1.0 + jnp.exp(-x))


def _head_rms(x, gain, group):
    ms = _dot_hilo(x * x, _group_mean_matrix(x.shape[1], group))
    return x * lax.rsqrt(ms + RMS_EPS) * gain


def _head_rms_lanes(x, gain):
    lane = lax.broadcasted_iota(jnp.int32, (x.shape[0], LANES), 1)
    low = lane < HEAD_DIM
    cols = []
    for c in range(x.shape[1] // LANES):
        xc = x[:, c * LANES:(c + 1) * LANES]
        x2 = xc * xc
        s_low = jnp.sum(jnp.where(low, x2, 0.0), axis=1, keepdims=True)
        s_high = jnp.sum(jnp.where(low, 0.0, x2), axis=1, keepdims=True)
        ms = jnp.where(low, s_low, s_high) * (1.0 / HEAD_DIM)
        cols.append(xc * lax.rsqrt(ms + RMS_EPS))
    return jnp.concatenate(cols, axis=1) * gain


def _inproj_kernel(x_ref, g1_ref, w_ref, wa_ref, ba_ref, mqg_ref, mkg_ref, sqg_ref, skg_ref,
                   gq_ref, gk_ref, gl_ref, gv_ref, gg_ref, *att_refs, prompt):
    x = x_ref[...]
    ms = jnp.mean(x * x, axis=-1, keepdims=True)
    h = (x * lax.rsqrt(ms + RMS_EPS) * g1_ref[0]).astype(BF16)
    p = _dot(h, w_ref[0])
    gq_ref[...] = p[:, 0:128] * (GLA_DK ** -0.5)
    gk_ref[...] = p[:, 128:256]
    gv_ref[...] = p[:, 256:512]
    gg_ref[...] = p[:, 512:768]
    alpha = _dot(p[:, 768:896].astype(BF16), wa_ref[0]) + ba_ref[0]
    gl_ref[...] = _log_sigmoid(alpha) / GLA_TAU

    def attn_group(base, qg_ref, kg_ref, refs, tiles_of_k=True):
        q = _head_rms_lanes(p[:, base:base + ATT_Q_W], qg_ref[0]) * (HEAD_DIM ** -0.5)
        refs[0][...] = q.astype(BF16)
        k = _head_rms_lanes(p[:, base + 384:base + 512], kg_ref[0])
        v = p[:, base + 512:base + 640]
        if prompt:
            _, kv32t_ref, tiles_ref, rows_ref = refs
            kt, vt = k.T, v.T
            kv32t_ref[0, 0:128, :] = kt
            kv32t_ref[0, 128:256, :] = vt
            tiled = (kt if tiles_of_k else vt).astype(BF16)
            for c in range(tiles_ref.shape[1]):
                tiles_ref[0, c] = tiled[:, c * ATT_TILE:(c + 1) * ATT_TILE]
            rows_ref[...] = (v if tiles_of_k else k).astype(BF16)
        else:
            _, kv32_ref, kv16_ref = refs
            kv32_ref[:, 0:128] = k
            kv32_ref[:, 128:256] = v
            kv16_ref[:, 0:128] = k.astype(BF16)
            kv16_ref[:, 128:256] = v.astype(BF16)
        return k

    if prompt:
        mk = attn_group(896, mqg_ref, mkg_ref, att_refs[0:4], tiles_of_k=False)
        attn_group(1536, sqg_ref, skg_ref, att_refs[5:9])
        mkm_ref = att_refs[4]
        for j in range(mkm_ref.shape[1]):
            mkm_ref[0, j:j + 1, :] = jnp.mean(mk[j * MOBA_BLOCK:(j + 1) * MOBA_BLOCK], axis=0, keepdims=True)
    else:
        attn_group(896, mqg_ref, mkg_ref, att_refs[0:3])
        attn_group(1536, sqg_ref, skg_ref, att_refs[3:6])


def _layer_spec(a, layer):
    return pl.BlockSpec((1,) + a.shape[1:], lambda *_: (layer, 0, 0))


def _inproj(x, wts, layer, tm, seq=None):
    n = x.shape[0]
    grid = n // tm
    row = lambda w: pl.BlockSpec((tm, w), lambda i: (i, 0))
    sds = jax.ShapeDtypeStruct
    outs = [sds((n, 128), F32), sds((n, 128), F32), sds((n, 128), F32), sds((n, 256), F32), sds((n, 256), F32)]
    out_specs = [row(128), row(128), row(128), row(256), row(256)]
    if seq is None:
        group = [(sds((n, ATT_Q_W), BF16), row(ATT_Q_W)), (sds((n, ATT_KV_W), F32), row(ATT_KV_W)),
                 (sds((n, ATT_KV_W), BF16), row(ATT_KV_W))]
        att = group + group
    else:
        tiles = seq // tm
        nblk = tm // ATT_TILE
        group = [(sds((n, ATT_Q_W), BF16), row(ATT_Q_W)),
                 (sds((n // seq, ATT_KV_W, seq), F32),
                  pl.BlockSpec((1, ATT_KV_W, tm), lambda i: (i // tiles, 0, i % tiles))),
                 (sds((n // seq, seq // ATT_TILE, 128, ATT_TILE), BF16),
                  pl.BlockSpec((1, nblk, 128, ATT_TILE), lambda i: (i // tiles, i % tiles, 0, 0))),
                 (sds((n, 128), BF16), row(128))]
        means = (sds((grid, nblk, 128), F32), pl.BlockSpec((1, nblk, 128), lambda i: (i, 0, 0)))
        att = group + [means] + group
    outs += [a for a, _ in att]
    out_specs += [b for _, b in att]
    ins = [x] + [wts[name] for name in ('g1', 'w_in', 'w_alpha', 'b_alpha', 'mqg', 'mkg', 'sqg', 'skg')]
    in_specs = [row(D_MODEL)] + [_layer_spec(a, layer) for a in ins[1:]]
    return pl.pallas_call(
        functools.partial(_inproj_kernel, prompt=seq is not None), grid=(grid,),
        in_specs=in_specs, out_specs=out_specs, out_shape=outs,
        compiler_params=pltpu.CompilerParams(dimension_semantics=("arbitrary",), vmem_limit_bytes=VMEM_LIMIT),
        name=f"inproj_{tm}")(*ins)


def _gla_kernel(q_ref, k_ref, gl_ref, v_ref, gg_ref, gn_ref, s0_ref, o_ref, sT_ref,
                st_scr, p_scr, w_scr, *, chunk, n_chunks):
    c = chunk
    step = pl.program_id(0)
    n_seq = q_ref.shape[0]

    @pl.when(step == 0)
    def _():
        st_scr[...] = s0_ref[...]

    ri = lax.broadcasted_iota(jnp.int32, (c, c), 0)
    ci = lax.broadcasted_iota(jnp.int32, (c, c), 1)
    ltri = jnp.where(ri >= ci, 1.0, 0.0).astype(BF16)
    kh = lax.broadcasted_iota(jnp.int32, (GLA_QK_W, GLA_V_W), 0) // GLA_DK
    vh = lax.broadcasted_iota(jnp.int32, (GLA_QK_W, GLA_V_W), 1) // GLA_DV
    head_ones = jnp.where(kh == vh, 1.0, 0.0).astype(BF16)
    vh2 = lax.broadcasted_iota(jnp.int32, (GLA_V_W, GLA_QK_W), 0) // GLA_DV
    kh2 = lax.broadcasted_iota(jnp.int32, (GLA_V_W, GLA_QK_W), 1) // GLA_DK
    bd_mask = vh2 == kh2
    sc = GLA_SUB
    n_sub = c // sc
    trow = lax.broadcasted_iota(jnp.int32, (sc, GLA_QK_W), 0)
    qk_head = lax.broadcasted_iota(jnp.int32, (sc, GLA_QK_W), 1) // GLA_DK
    v_head = lax.broadcasted_iota(jnp.int32, (sc, GLA_V_W), 1) // GLA_DV
    key_col = lax.broadcasted_iota(jnp.int32, (GLA_HEADS * sc, c), 1)

    def chunks(ic, carry):
        r0 = pl.multiple_of(ic * c, c)
        seqs = range(n_seq)
        ld = lambda ref: [ref[bi, pl.ds(r0, c), :] for bi in seqs]
        q, k, g, v = ld(q_ref), ld(k_ref), ld(gl_ref), ld(v_ref)
        b = []
        for bi in seqs:
            g_hi, g_lo = _split_hilo(g[bi])
            b.append(_dot(ltri, g_hi) + _dot(ltri, g_lo))
        st = [st_scr[bi] for bi in seqs]
        o_inter = [_dot_nt((q[bi] * jnp.exp(b[bi])).astype(BF16), st[bi].astype(BF16)) for bi in seqs]
        v16 = [v[bi].astype(BF16) for bi in seqs]

        s4 = {}
        for i_sub in range(1, n_sub):
            rows = slice(i_sub * sc, (i_sub + 1) * sc)
            for bi in seqs:
                e = b[bi][i_sub * sc - 1:i_sub * sc, :]
                a = q[bi][rows] * jnp.exp(b[bi][rows] - e)
                a4 = jnp.concatenate([jnp.where(qk_head == h, a, 0.0) for h in range(GLA_HEADS)], axis=0)
                kd = (k[bi] * jnp.exp(jnp.minimum(e - b[bi], 0.0))).astype(BF16)
                s4[bi, i_sub] = jnp.where(key_col < i_sub * sc, _dot_nt(a4.astype(BF16), kd), 0.0)

        for i_sub in range(n_sub):
            rows = slice(i_sub * sc, (i_sub + 1) * sc)
            for bi in seqs:
                q_i, b_i, k_i = q[bi][rows], b[bi][rows], k[bi][rows]
                for s in range(sc):
                    sl = q_i * jnp.exp(jnp.minimum(b_i - b_i[s:s + 1], 0.0)) * k_i[s:s + 1]
                    off = (i_sub * sc + s) * sc
                    p_scr[bi, off:off + sc, :] = jnp.where(trow >= s, sl, 0.0)
        for bi in seqs:
            w_scr[bi] = _dot_hilo(p_scr[bi], head_ones)

        for bi in seqs:
            b_last = b[bi][c - 1:c, :]
            kt = (k[bi] * jnp.exp(b_last - b[bi])).astype(BF16)
            upd = _dot(v[bi].T.astype(BF16), kt)
            st_scr[bi] = st[bi] * jnp.exp(b_last) + jnp.where(bd_mask, upd, 0.0)

        o4 = {key: _dot(val.astype(BF16), v16[key[0]]) for key, val in s4.items()}

        for i_sub in range(n_sub):
            rows = slice(i_sub * sc, (i_sub + 1) * sc)
            for bi in seqs:
                o_i = o_inter[bi][rows]
                for s in range(sc):
                    off = (i_sub * sc + s) * sc
                    o_i = o_i + w_scr[bi, off:off + sc, :] * v[bi][i_sub * sc + s:i_sub * sc + s + 1, :]
                if i_sub > 0:
                    for h in range(GLA_HEADS):
                        o_i = o_i + jnp.where(v_head == h, o4[bi, i_sub][h * sc:(h + 1) * sc], 0.0)
                on = _head_rms(o_i, gn_ref[0], GLA_DV)
                dst = pl.ds(r0 + i_sub * sc, sc)
                o_ref[bi, dst, :] = (on * _silu(gg_ref[bi, dst, :])).astype(o_ref.dtype)
        return carry
    lax.fori_loop(0, n_chunks, chunks, 0)

    @pl.when(step == pl.num_programs(0) - 1)
    def _():
        sT_ref[...] = st_scr[...]


def _gla(gq, gk, gl, gv, gg, gnorm, layer, s0T, *, batch, seq, chunk, chunks_per_step):
    rows = chunk * chunks_per_step
    row = lambda w: pl.BlockSpec((batch, rows, w), lambda i: (0, i, 0))
    state = pl.BlockSpec((batch, GLA_V_W, GLA_QK_W), lambda i: (0, 0, 0))
    r3 = lambda a: a.reshape(batch, seq, a.shape[-1])
    kern = functools.partial(_gla_kernel, chunk=chunk, n_chunks=chunks_per_step)
    o, st = pl.pallas_call(
        kern, grid=(seq // rows,),
        in_specs=[row(128), row(128), row(128), row(256), row(256),
                  _layer_spec(gnorm, layer), state],
        out_specs=[row(256), state],
        out_shape=[jax.ShapeDtypeStruct((batch, seq, GLA_V_W), BF16),
                   jax.ShapeDtypeStruct((batch, GLA_V_W, GLA_QK_W), F32)],
        scratch_shapes=[pltpu.VMEM((batch, GLA_V_W, GLA_QK_W), F32),
                        pltpu.VMEM((batch, chunk * GLA_SUB, GLA_QK_W), F32),
                        pltpu.VMEM((batch, chunk * GLA_SUB, GLA_V_W), F32)],
        compiler_params=pltpu.CompilerParams(dimension_semantics=("arbitrary",), vmem_limit_bytes=VMEM_LIMIT),
        name="gla_prompt")(r3(gq), r3(gk), r3(gl), r3(gv), r3(gg), gnorm, s0T)
    return o.reshape(batch * seq, GLA_V_W), st


def _gla_sample_kernel(q_ref, k_ref, gl_ref, v_ref, gg_ref, gn_ref, s0_ref, o_ref, sT_ref):
    rows = (8, GLA_QK_W)
    decay = jnp.exp(gl_ref[0])
    r = lax.broadcasted_iota(jnp.int32, (GLA_V_W, GLA_V_W), 0)
    cc = lax.broadcasted_iota(jnp.int32, (GLA_V_W, GLA_V_W), 1)
    v_diag = jnp.where(r == cc, jnp.broadcast_to(v_ref[0], (GLA_V_W, GLA_V_W)), 0.0).astype(BF16)
    k_rows = jnp.broadcast_to(k_ref[0], (GLA_V_W, GLA_QK_W)).astype(BF16)
    vh = lax.broadcasted_iota(jnp.int32, (GLA_V_W, GLA_QK_W), 0) // GLA_DV
    kh = lax.broadcasted_iota(jnp.int32, (GLA_V_W, GLA_QK_W), 1) // GLA_DK
    outer = jnp.where(vh == kh, _dot(v_diag, k_rows), 0.0)
    st = s0_ref[0] * decay + outer
    sT_ref[0] = st
    q8 = jnp.broadcast_to(q_ref[0], rows).astype(BF16)
    o = _dot_nt(q8, st.astype(BF16))
    on = _head_rms(o, gn_ref[0], GLA_DV)
    o_ref[0] = (on * _silu(gg_ref[0]))[0:1].astype(o_ref.dtype)


def _gla_sample(gq, gk, gl, gv, gg, gnorm, layer, s0T):
    n = gq.shape[0]
    r3 = lambda a: a.reshape(n, 1, a.shape[-1])
    row = lambda w: pl.BlockSpec((1, 1, w), lambda b: (b, 0, 0))
    st_spec = pl.BlockSpec((1, GLA_V_W, GLA_QK_W), lambda b: (b, 0, 0))
    o, st = pl.pallas_call(
        _gla_sample_kernel, grid=(n,),
        in_specs=[row(128), row(128), row(128), row(256), row(256),
                  _layer_spec(gnorm, layer), st_spec],
        out_specs=[row(256), st_spec],
        out_shape=[jax.ShapeDtypeStruct((n, 1, GLA_V_W), BF16), jax.ShapeDtypeStruct((n, GLA_V_W, GLA_QK_W), F32)],
        compiler_params=pltpu.CompilerParams(dimension_semantics=("arbitrary",), vmem_limit_bytes=VMEM_LIMIT),
        name="gla_sample")(r3(gq), r3(gk), r3(gl), r3(gv), r3(gg), gnorm, s0T)
    return o.reshape(n, GLA_V_W), st


def _stack_heads(q_cols):
    lane = lax.broadcasted_iota(jnp.int32, q_cols[0].shape, 1)
    low = lane < HEAD_DIM
    zero = jnp.zeros_like(q_cols[0])
    parts = [jnp.where(low, qc, zero) for qc in q_cols] + [jnp.where(low, zero, qc) for qc in q_cols]
    return jnp.concatenate(parts, axis=0)


def _unstack_heads(o, rows):
    lane = lax.broadcasted_iota(jnp.int32, (rows, LANES), 1)
    low = lane < HEAD_DIM
    return [jnp.where(low, o[j * rows:(j + 1) * rows], o[(GROUP + j) * rows:(GROUP + j + 1) * rows])
            for j in range(GROUP)]


def _suffix_matrix():
    r = lax.broadcasted_iota(jnp.int32, (ATT_TILE, ATT_TILE), 0)
    c = lax.broadcasted_iota(jnp.int32, (ATT_TILE, ATT_TILE), 1)
    u = jnp.where(r > c, 1.0, 0.0).astype(BF16)
    return jnp.concatenate([u, u], axis=0)


def _twice(a):
    return jnp.concatenate([a, a], axis=1)


def _sb_rows(q, kt, v, u2, carry, mask):
    z = _dot(q, kt)
    l = _neg_softplus(z)
    if mask is not None:
        l = jnp.where(mask, l, 0.0)
    hi, lo = _split_hilo(l)
    c = _dot(jnp.concatenate([hi, lo], axis=1), u2)
    w = jnp.exp(z + l + c + _twice(carry))
    if mask is not None:
        w = jnp.where(mask, w, 0.0)
    return _dot(w.astype(BF16), v), carry + jnp.sum(l, axis=1, keepdims=True)


SB_STOP = -104.0


def _sb_prompt_kernel(q_ref, kt_ref, v_ref, o_ref, q_scr, carry_scr, acc_scr):
    i = pl.program_id(1)
    t = ATT_TILE
    q_scr[...] = _stack_heads([q_ref[0, :, 128 * j:128 * (j + 1)] for j in range(GROUP)])
    carry_scr[...] = jnp.zeros(carry_scr.shape, F32)
    acc_scr[...] = jnp.zeros(acc_scr.shape, F32)
    u2 = _suffix_matrix()
    rt = SB_ROW_TILE
    rr = lax.broadcasted_iota(jnp.int32, (rt, t), 0) & (t - 1)
    cc = lax.broadcasted_iota(jnp.int32, (rt, t), 1)

    def tile(kj, diagonal):
        kt = kt_ref[0, kj]
        v = v_ref[0, pl.ds(pl.multiple_of(kj * t, t), t), :]
        for r in range(ATT_ROWS // rt):
            rows = slice(r * rt, (r + 1) * rt)
            mask = cc < rr + (r * rt) % t if diagonal else None
            pv, carry = _sb_rows(q_scr[rows, :], kt, v, u2, carry_scr[rows, :], mask)
            acc_scr[rows, :] += pv
            carry_scr[rows, :] = carry

    def carry_max():
        return jnp.max(jnp.max(carry_scr[...], axis=0, keepdims=True), axis=1, keepdims=True)[0, 0]

    tile(i, True)

    def cond(state):
        kj, cmax = state
        return jnp.logical_and(kj >= 0, cmax > SB_STOP)

    def body(state):
        kj, _ = state
        tile(kj, False)
        return kj - 1, carry_max()
    lax.while_loop(cond, body, (i - 1, carry_max()))

    cols = _unstack_heads(acc_scr[...], t)
    for j in range(GROUP):
        o_ref[0, :, 128 * j:128 * (j + 1)] = cols[j].astype(o_ref.dtype)


def _sb_prompt(q, kt, v, *, batch, seq):
    t = ATT_TILE
    out = pl.pallas_call(
        _sb_prompt_kernel, grid=(batch, seq // t),
        in_specs=[pl.BlockSpec((1, t, ATT_Q_W), lambda b, i: (b, i, 0)),
                  pl.BlockSpec((1, seq // t, LANES, t), lambda b, i: (b, 0, 0, 0)),
                  pl.BlockSpec((1, seq, LANES), lambda b, i: (b, 0, 0))],
        out_specs=pl.BlockSpec((1, t, ATT_Q_W), lambda b, i: (b, i, 0)),
        out_shape=jax.ShapeDtypeStruct((batch, seq, ATT_Q_W), BF16),
        scratch_shapes=[pltpu.VMEM((ATT_ROWS, LANES), BF16), pltpu.VMEM((ATT_ROWS, LANES), F32),
                        pltpu.VMEM((ATT_ROWS, LANES), F32)],
        compiler_params=pltpu.CompilerParams(dimension_semantics=("arbitrary", "arbitrary"),
                                             vmem_limit_bytes=VMEM_LIMIT),
        name="sb_prompt")(q.reshape(batch, seq, ATT_Q_W), kt, v.reshape(batch, seq, LANES))
    return out.reshape(batch * seq, ATT_Q_W)


def _bias_of_dist(dist, rb_ref, h):
    bias = jnp.full(dist.shape, rb_ref[0, h], F32)
    for b in range(1, REL_BUCKETS):
        bias = jnp.where(dist >= _T5_THR[b], rb_ref[b, h], bias)
    return bias - rb_ref[REL_BUCKETS - 1, h]


def _bias_kernel(rb_ref, own_ref, prev_ref, samp_ref):
    t = ATT_TILE
    key = lax.broadcasted_iota(jnp.int32, (t, t), 0)
    qry = lax.broadcasted_iota(jnp.int32, (t, t), 1)
    for h in range(ATT_HEADS):
        own = _bias_of_dist(jnp.maximum(qry - key, 0), rb_ref, h)
        own_ref[:, h * t:(h + 1) * t] = jnp.where(key <= qry, own, NEG_INF)
        prev_ref[:, h * t:(h + 1) * t] = _bias_of_dist(qry - key + t, rb_ref, h)
    s = lax.broadcasted_iota(jnp.int32, (8, t), 1)
    hrow = lax.broadcasted_iota(jnp.int32, (8, t), 0)
    last = jnp.zeros((8, t), F32)
    self_b = jnp.zeros((8, LANES), F32)
    hrow2 = lax.broadcasted_iota(jnp.int32, (8, LANES), 0)
    for h in range(ATT_HEADS):
        last = jnp.where(hrow == h, _bias_of_dist(t - s, rb_ref, h), last)
        self_b = jnp.where(hrow2 == h, rb_ref[0, h] - rb_ref[REL_BUCKETS - 1, h], self_b)
    samp_ref[:, 0:t] = last
    samp_ref[:, t:t + LANES] = self_b


def _bias_tiles(rel_bias):
    t = ATT_TILE
    return pl.pallas_call(
        _bias_kernel,
        in_specs=[pl.BlockSpec(memory_space=pltpu.SMEM)],
        out_shape=[jax.ShapeDtypeStruct((t, ATT_ROWS), F32), jax.ShapeDtypeStruct((t, ATT_ROWS), F32),
                   jax.ShapeDtypeStruct((8, t + LANES), F32)],
        name="t5_bias_tiles")(rel_bias)


def _top3_select(gate, n_valid, axis):
    blk_i = lax.broadcasted_iota(jnp.int32, gate.shape, axis)
    blk = blk_i.astype(F32)
    valid = blk_i < n_valid
    g = jnp.where(valid, gate, NEG_INF)
    sel = jnp.zeros(gate.shape, F32)
    for _ in range(MOBA_TOPK):
        m = jnp.max(g, axis=axis, keepdims=True)
        idx = jnp.min(jnp.where(g == m, blk, float(LANES)), axis=axis, keepdims=True)
        pick = blk == idx
        sel = jnp.where(pick, 1.0, sel)
        g = jnp.where(pick, -jnp.inf, g)
    return jnp.where(valid, sel, 0.0) > 0.5


MOBA_COL_TILE = 256
MOBA_ACC_ROWS = LANES + 16


def _moba_prompt_kernel(q_ref, k_ref, vt_ref, km_ref, own_ref, prev_ref, o_ref,
                        qx_scr, s_scr, p_scr, al_scr, m_scr, acc_scr):
    i = pl.program_id(1)
    t = ATT_TILE
    q = _stack_heads([q_ref[0, :, 128 * j:128 * (j + 1)] for j in range(GROUP)])
    qt = q.astype(F32).T.astype(BF16)
    n_blk = k_ref.shape[1] // t
    gate = _dot(km_ref[0, 0:n_blk, :].astype(BF16), qt)
    sel = _top3_select(gate, i, 0)
    qx_scr[0:LANES, :] = qt
    qx_scr[LANES:LANES + n_blk, :] = jnp.where(sel, 0.0, NEG_INF).astype(BF16)
    qx_scr[LANES + n_blk:2 * LANES, :] = jnp.full((LANES - n_blk, ATT_ROWS), NEG_INF, BF16)
    blk_col = lax.broadcasted_iota(jnp.int32, (t, LANES), 1)
    ones = jnp.ones((MOBA_ACC_ROWS - LANES, 2 * t), BF16)
    ct = MOBA_COL_TILE
    col_tiles = [slice(c * ct, (c + 1) * ct) for c in range(ATT_ROWS // ct)]

    def keys(kj):
        return k_ref[0, pl.ds(pl.multiple_of(jnp.maximum(kj, 0) * t, t), t), :]

    def past_keys(kj):
        blk = jnp.where(kj >= 0, kj, LANES - 1)
        return jnp.concatenate([keys(kj), jnp.where(blk_col == blk, 1.0, 0.0).astype(BF16)], axis=1)

    def pair_keys(kj):
        return jnp.concatenate([past_keys(kj), past_keys(kj - 1)], axis=0)

    def pair_values(kj):
        clamp = lambda j: jnp.clip(j, 0, n_blk - 1)
        vt = jnp.concatenate([vt_ref[0, clamp(kj)], vt_ref[0, clamp(kj - 1)]], axis=1)
        return jnp.concatenate([vt, ones], axis=0)

    k_own = keys(i)
    vx = jnp.concatenate([vt_ref[0, i], ones[:, 0:t]], axis=0)
    for cols in col_tiles:
        s = _dot(k_own, qx_scr[0:LANES, cols]) + own_ref[:, cols]
        m0 = jnp.max(s, axis=0, keepdims=True)
        acc_scr[:, cols] = _dot(vx, jnp.exp(s - m0).astype(BF16))
        m_scr[:, cols] = jnp.broadcast_to(m0, (8, ct))

    kx = pair_keys(i - 1)
    for cols in col_tiles:
        s = _dot(kx, qx_scr[:, cols])
        s_scr[:, cols] = jnp.concatenate([s[0:t] + prev_ref[:, cols], s[t:2 * t]], axis=0)
    p_scr[...] = jnp.zeros(p_scr.shape, BF16)
    al_scr[...] = jnp.ones(al_scr.shape, F32)
    n_pairs = (i + 1) // 2

    def body(n, carry):
        kj = i - 1 - 2 * n
        kx = pair_keys(jnp.where(n < n_pairs, kj, -1))
        vx = pair_values(kj + 4)
        for cols in col_tiles:
            acc_scr[:, cols] = al_scr[0:1, cols] * acc_scr[:, cols] + _dot(vx, p_scr[:, cols])
            s = s_scr[:, cols]
            s_scr[:, cols] = _dot(kx, qx_scr[:, cols])
            m_old = m_scr[0:1, cols]
            m_new = jnp.maximum(m_old, jnp.max(s, axis=0, keepdims=True))
            p_scr[:, cols] = jnp.exp(s - m_new).astype(BF16)
            al_scr[:, cols] = jnp.broadcast_to(jnp.exp(m_old - m_new), (8, ct))
            m_scr[:, cols] = jnp.broadcast_to(m_new, (8, ct))
        return carry
    lax.fori_loop(1, n_pairs + 2, body, 0)

    o = (acc_scr[0:LANES, :] / acc_scr[LANES:LANES + 1, :]).T
    cols = _unstack_heads(o, t)
    for j in range(GROUP):
        o_ref[0, :, 128 * j:128 * (j + 1)] = cols[j].astype(o_ref.dtype)


def _moba_prompt(q, k, vt, kmean, bias_own, bias_prev, *, batch, seq):
    t = ATT_TILE
    const = lambda a: pl.BlockSpec(a.shape, lambda b, i: (0, 0))
    out = pl.pallas_call(
        _moba_prompt_kernel, grid=(batch, seq // t),
        in_specs=[pl.BlockSpec((1, t, ATT_Q_W), lambda b, i: (b, i, 0)),
                  pl.BlockSpec((1, seq, LANES), lambda b, i: (b, 0, 0)),
                  pl.BlockSpec((1, seq // t, LANES, t), lambda b, i: (b, 0, 0, 0)),
                  pl.BlockSpec((1, LANES, LANES), lambda b, i: (b, 0, 0)),
                  const(bias_own), const(bias_prev)],
        out_specs=pl.BlockSpec((1, t, ATT_Q_W), lambda b, i: (b, i, 0)),
        out_shape=jax.ShapeDtypeStruct((batch, seq, ATT_Q_W), BF16),
        scratch_shapes=[pltpu.VMEM((2 * LANES, ATT_ROWS), BF16), pltpu.VMEM((2 * t, ATT_ROWS), F32),
                        pltpu.VMEM((2 * t, ATT_ROWS), BF16),
                        pltpu.VMEM((8, ATT_ROWS), F32), pltpu.VMEM((8, ATT_ROWS), F32),
                        pltpu.VMEM((MOBA_ACC_ROWS, ATT_ROWS), F32)],
        compiler_params=pltpu.CompilerParams(dimension_semantics=("arbitrary", "arbitrary"),
                                             vmem_limit_bytes=VMEM_LIMIT),
        name="moba_prompt")(q.reshape(batch, seq, ATT_Q_W), k.reshape(batch, seq, LANES), vt, kmean,
                            bias_own, bias_prev)
    return out.reshape(batch * seq, ATT_Q_W)


def _sample_q_rows(q_row):
    row = lax.broadcasted_iota(jnp.int32, (8, LANES), 0)
    lane = lax.broadcasted_iota(jnp.int32, (8, LANES), 1)
    qf = q_row.astype(F32)
    out = jnp.zeros((8, LANES), F32)
    for h in range(ATT_HEADS):
        j, n = h % GROUP, h // GROUP
        col = jnp.broadcast_to(qf[:, 128 * j:128 * (j + 1)], (8, LANES))
        half = lane >= HEAD_DIM if n == 1 else lane < HEAD_DIM
        out = jnp.where(jnp.logical_and(row == h, half), col, out)
    return out.astype(BF16)


def _sample_o_row(o):
    lane = lax.broadcasted_iota(jnp.int32, (1, LANES), 1)
    return [jnp.where(lane < HEAD_DIM, o[j:j + 1], o[GROUP + j:GROUP + j + 1]) for j in range(GROUP)]


TILES_PER_STEP = PAGES_PER_STEP * PAGE_SIZE // ATT_TILE


def _step_keys_values(pages):
    kt = jnp.concatenate([p[0, 0:128, :] for p in pages], axis=1).astype(BF16)
    vt = jnp.concatenate([p[0, 128:256, :] for p in pages], axis=1).astype(BF16)
    return kt, vt


def _tiles_to_rows(a):
    return jnp.concatenate([a[:, c * ATT_TILE:(c + 1) * ATT_TILE] for c in range(a.shape[1] // ATT_TILE)], axis=0)


def _rows_to_tiles(a):
    return jnp.concatenate([a[8 * c:8 * (c + 1), :] for c in range(a.shape[0] // 8)], axis=1)


def _sb_sample_kernel(pt_ref, q_ref, *refs):
    del pt_ref
    pages = refs[:PAGES_PER_STEP]
    o_ref, carry_ref, carry_scr, acc_scr = refs[PAGES_PER_STEP:]
    step = pl.program_id(1)

    @pl.when(step == 0)
    def _():
        carry_scr[...] = jnp.zeros(carry_scr.shape, F32)
        acc_scr[...] = jnp.zeros(acc_scr.shape, F32)

    q = _sample_q_rows(q_ref[0])
    kt, vt = _step_keys_values(pages)
    z = _tiles_to_rows(_dot(q, kt))
    l = _neg_softplus(z)
    hi, lo = _split_hilo(l)
    c = _dot(jnp.concatenate([hi, lo], axis=1), _suffix_matrix())
    tile_sum = jnp.sum(l, axis=1, keepdims=True)
    carry = carry_scr[...]
    carries = [None] * TILES_PER_STEP
    for tl in reversed(range(TILES_PER_STEP)):
        carries[tl] = carry
        carry = carry + tile_sum[8 * tl:8 * (tl + 1)]
    w = jnp.exp(z + l + c + _twice(jnp.concatenate(carries, axis=0)))
    acc = acc_scr[...] + _dot_nt(_rows_to_tiles(w).astype(BF16), vt)
    carry_scr[...] = carry
    acc_scr[...] = acc

    @pl.when(step == pl.num_programs(1) - 1)
    def _():
        cols = _sample_o_row(acc)
        for j in range(GROUP):
            o_ref[0, :, 128 * j:128 * (j + 1)] = cols[j].astype(o_ref.dtype)
        carry_ref[0] = carry


def _page_specs(layer, n_pool, n_steps, reverse, pages_per_step):
    specs = []
    for p in range(pages_per_step):
        def imap(b, c, pt, p=p):
            cc = (n_steps - 1 - c) if reverse else c
            return (layer * n_pool + pt[b, cc * pages_per_step + p], 0, 0)
        specs.append(pl.BlockSpec((1, ATT_KV_W, PAGE_SIZE), imap))
    return specs


def _cache_pages(cache):
    d, n_pool = cache.shape[:2]
    return cache.transpose(0, 1, 3, 4, 5, 2).reshape(d * n_pool, ATT_KV_W, PAGE_SIZE)


def _sb_sample(q, cache, page_table, layer):
    nseq, n_pages = page_table.shape
    n_pool = cache.shape[1]
    n_steps = n_pages // PAGES_PER_STEP
    cache2 = _cache_pages(cache)
    q3 = q.reshape(nseq, 1, ATT_Q_W)

    def walk(steps):
        grid_spec = pltpu.PrefetchScalarGridSpec(
            num_scalar_prefetch=1, grid=(nseq, steps),
            in_specs=[pl.BlockSpec((1, 1, ATT_Q_W), lambda b, c, pt: (b, 0, 0))]
            + _page_specs(layer, n_pool, n_steps, True, PAGES_PER_STEP),
            out_specs=[pl.BlockSpec((1, 1, ATT_Q_W), lambda b, c, pt: (b, 0, 0)),
                       pl.BlockSpec((1, 8, LANES), lambda b, c, pt: (b, 0, 0))],
            scratch_shapes=[pltpu.VMEM((8, LANES), F32), pltpu.VMEM((8, LANES), F32)])
        return pl.pallas_call(
            _sb_sample_kernel, grid_spec=grid_spec,
            out_shape=[jax.ShapeDtypeStruct((nseq, 1, ATT_Q_W), BF16),
                       jax.ShapeDtypeStruct((nseq, 8, LANES), F32)],
            compiler_params=pltpu.CompilerParams(dimension_semantics=("arbitrary", "arbitrary"),
                                                 vmem_limit_bytes=VMEM_LIMIT),
            name=f"sb_sample_{steps}")(page_table, q3, *([cache2] * PAGES_PER_STEP))

    out, carry = walk(1)
    out = lax.cond(jnp.max(carry) < SB_STOP, lambda: out, lambda: walk(n_steps)[0])
    return out.reshape(nseq, ATT_Q_W)


def _moba_sample_kernel(pt_ref, q_ref, kvn_ref, sb_ref, *refs, n_blocks):
    del pt_ref
    pages = refs[:MOBA_PAGES_PER_STEP]
    o_ref, gate_scr, m_scr, l_scr, acc_scr = refs[MOBA_PAGES_PER_STEP:]
    step = pl.program_id(1)
    bps = MOBA_PAGES_PER_STEP * PAGE_SIZE // MOBA_BLOCK
    rows = bps * 8

    q = _sample_q_rows(q_ref[0])
    kt, vt = _step_keys_values(pages)
    s = _tiles_to_rows(_dot(q, kt))
    g = jnp.mean(s, axis=1, keepdims=True)
    row_blk = lax.broadcasted_iota(jnp.int32, (rows, ATT_TILE), 0) // 8
    last = jnp.logical_and(step == pl.num_programs(1) - 1, row_blk == bps - 1)
    s = s + jnp.where(last, jnp.concatenate([sb_ref[:, 0:ATT_TILE]] * bps, axis=0), 0.0)
    m = jnp.max(s, axis=1, keepdims=True)
    p = jnp.exp(s - m)
    p_bd = jnp.concatenate([jnp.where(row_blk == c, p, 0.0) for c in range(bps)], axis=1).astype(BF16)
    dst = pl.ds(step * bps, bps)
    wide = lambda a: jnp.broadcast_to(a, (rows, LANES)).reshape(bps, 8, LANES)
    gate_scr[dst] = wide(g)
    m_scr[dst] = wide(m)
    l_scr[dst] = wide(jnp.sum(p, axis=1, keepdims=True))
    acc_scr[dst] = _dot_nt(p_bd, vt).reshape(bps, 8, LANES)

    @pl.when(step == pl.num_programs(1) - 1)
    def _():
        gates = [gate_scr[j] for j in range(n_blocks)]
        sel = [jnp.zeros((8, LANES), jnp.bool_)] * n_blocks
        for _ in range(MOBA_TOPK):
            best = functools.reduce(jnp.maximum, gates)
            idx = functools.reduce(jnp.minimum, [jnp.where(gates[j] == best, float(j), float(n_blocks))
                                                 for j in range(n_blocks)])
            for j in range(n_blocks):
                pick = idx == float(j)
                sel[j] = jnp.logical_or(sel[j], pick)
                gates[j] = jnp.where(pick, -jnp.inf, gates[j])
        kvn = kvn_ref[0]
        s_self = (jnp.sum(q.astype(F32) * kvn[:, 0:128].astype(F32), axis=1, keepdims=True)
                  + sb_ref[:, ATT_TILE:ATT_TILE + 1])
        m_tot = jnp.maximum(functools.reduce(jnp.maximum, [jnp.where(sel[j], m_scr[j], NEG_INF)
                                                           for j in range(n_blocks)]), s_self)
        p_self = jnp.exp(s_self - m_tot)
        denom = p_self
        o = p_self * kvn[:, 128:256].astype(F32)
        for j in range(n_blocks):
            coef = jnp.where(sel[j], jnp.exp(m_scr[j] - m_tot), 0.0)
            denom = denom + coef * l_scr[j]
            o = o + coef * acc_scr[j]
        cols = _sample_o_row(o / denom)
        for j in range(GROUP):
            o_ref[0, :, 128 * j:128 * (j + 1)] = cols[j].astype(o_ref.dtype)


def _moba_sample(q, kv_new, cache, page_table, bias_samp, layer):
    nseq, n_pages = page_table.shape
    n_pool = cache.shape[1]
    n_steps = n_pages // MOBA_PAGES_PER_STEP
    n_blocks = n_pages * PAGE_SIZE // MOBA_BLOCK
    cache2 = _cache_pages(cache)
    grid_spec = pltpu.PrefetchScalarGridSpec(
        num_scalar_prefetch=1, grid=(nseq, n_steps),
        in_specs=[pl.BlockSpec((1, 1, ATT_Q_W), lambda b, c, pt: (b, 0, 0)),
                  pl.BlockSpec((1, 1, ATT_KV_W), lambda b, c, pt: (b, 0, 0)),
                  pl.BlockSpec(bias_samp.shape, lambda b, c, pt: (0, 0))]
        + _page_specs(layer, n_pool, n_steps, False, MOBA_PAGES_PER_STEP),
        out_specs=pl.BlockSpec((1, 1, ATT_Q_W), lambda b, c, pt: (b, 0, 0)),
        scratch_shapes=[pltpu.VMEM((n_blocks, 8, LANES), F32)] * 4)
    out = pl.pallas_call(
        functools.partial(_moba_sample_kernel, n_blocks=n_blocks), grid_spec=grid_spec,
        out_shape=jax.ShapeDtypeStruct((nseq, 1, ATT_Q_W), BF16),
        compiler_params=pltpu.CompilerParams(dimension_semantics=("arbitrary", "arbitrary"),
                                             vmem_limit_bytes=VMEM_LIMIT),
        name="moba_sample")(page_table, q.reshape(nseq, 1, ATT_Q_W), kv_new.reshape(nseq, 1, ATT_KV_W),
                            bias_samp, *([cache2] * MOBA_PAGES_PER_STEP))
    return out.reshape(nseq, ATT_Q_W)


FF_CHUNK = 1408


def _ffn_kernel(x_ref, og_ref, om_ref, os_ref, wog_ref, wom_ref, wos_ref, g2_ref, wg_ref, wu_ref, wd_ref, y_ref,
                h2_scr):
    @pl.when(pl.program_id(1) == 0)
    def _():
        x1 = (x_ref[...] + _dot(og_ref[...], wog_ref[0]) + _dot(om_ref[...], wom_ref[0])
              + _dot(os_ref[...], wos_ref[0]))
        ms = jnp.mean(x1 * x1, axis=-1, keepdims=True)
        h2_scr[...] = (x1 * lax.rsqrt(ms + RMS_EPS) * g2_ref[0]).astype(BF16)
        y_ref[...] = x1

    h2 = h2_scr[...]
    a = _silu(_dot(h2, wg_ref[0])) * _dot(h2, wu_ref[0])
    y_ref[...] += _dot(a.astype(BF16), wd_ref[0])


def _ffn(x, og, om, osb, wts, layer, tm):
    n = x.shape[0]
    row = lambda w: pl.BlockSpec((tm, w), lambda i, f: (i, 0))
    ws = [wts[name] for name in ('wo_g', 'wo_m', 'wo_s', 'g2', 'w_gate', 'w_up', 'w_down')]
    w_specs = [_layer_spec(w, layer) for w in ws[:4]] + [
        pl.BlockSpec((1, D_MODEL, FF_CHUNK), lambda i, f: (layer, 0, f)),
        pl.BlockSpec((1, D_MODEL, FF_CHUNK), lambda i, f: (layer, 0, f)),
        pl.BlockSpec((1, FF_CHUNK, D_MODEL), lambda i, f: (layer, f, 0))]
    return pl.pallas_call(
        _ffn_kernel, grid=(n // tm, D_FF // FF_CHUNK),
        in_specs=[row(D_MODEL), row(GLA_V_W), row(ATT_Q_W), row(ATT_Q_W)] + w_specs,
        out_specs=row(D_MODEL), out_shape=jax.ShapeDtypeStruct((n, D_MODEL), F32),
        scratch_shapes=[pltpu.VMEM((tm, D_MODEL), BF16)],
        compiler_params=pltpu.CompilerParams(dimension_semantics=("arbitrary", "arbitrary"),
                                             vmem_limit_bytes=VMEM_LIMIT),
        name=f"ffn_{tm}")(x, og, om, osb, *ws)


def _prep_weights(norm1, w_in, w_alpha, b_alpha, gla_norm, moba_q_norm, moba_k_norm, sb_q_norm, sb_k_norm,
                  w_out, norm2, w_gate_up, w_down):
    o = _OFF
    cols = [w_in[:, :, o['gq']:o['ga']],
            jnp.pad(w_in[:, :, o['ga']:o['mq']], ((0, 0), (0, 0), (0, LANES - GLA_RANK))),
            w_in[:, :, o['mq']:o['mk']][:, :, _HEAD_PERM], w_in[:, :, o['mk']:o['sq']],
            w_in[:, :, o['sq']:o['sk']][:, :, _HEAD_PERM], w_in[:, :, o['sk']:]]
    row = lambda a: a[:, None, :]
    tile = lambda g, reps: jnp.tile(g, (1, reps))[:, None, :]
    return dict(
        g1=row(norm1), w_in=jnp.concatenate(cols, axis=2).astype(BF16),
        w_alpha=jnp.pad(w_alpha, ((0, 0), (0, LANES - GLA_RANK), (0, 0))).astype(BF16),
        b_alpha=row(b_alpha),
        gnorm=tile(gla_norm, GLA_HEADS), mqg=tile(moba_q_norm, ATT_HEADS), mkg=tile(moba_k_norm, KV_HEADS),
        sqg=tile(sb_q_norm, ATT_HEADS), skg=tile(sb_k_norm, KV_HEADS),
        wo_g=w_out[:, 0:256].astype(BF16), wo_m=w_out[:, 256:640][:, _HEAD_PERM].astype(BF16),
        wo_s=w_out[:, 640:1024][:, _HEAD_PERM].astype(BF16), g2=row(norm2),
        w_gate=w_gate_up[:, :, :D_FF].astype(BF16), w_up=w_gate_up[:, :, D_FF:].astype(BF16),
        w_down=w_down.astype(BF16))


def _state_to_blockdiag_T(s):
    b = s.shape[0]
    eye = jnp.eye(GLA_HEADS, dtype=s.dtype)
    return jnp.einsum('bhkv,hg->bhvgk', s, eye).reshape(b, GLA_V_W, GLA_QK_W)


def _blockdiag_T_to_state(st):
    b = st.shape[0]
    s5 = st.reshape(b, GLA_HEADS, GLA_DV, GLA_HEADS, GLA_DK)
    return jnp.stack([s5[:, h, :, h, :] for h in range(GLA_HEADS)], axis=1).transpose(0, 1, 3, 2)


def kernel(x_prompt, x_sample, cache_moba_kv, cache_sb_kv, state_gla, page_table, rel_bias, norm1, w_in, w_alpha,
           b_alpha, gla_norm, moba_q_norm, moba_k_norm, sb_q_norm, sb_k_norm, w_out, norm2, w_gate_up, w_down):
    nb, seq, _ = x_prompt.shape
    ns = x_sample.shape[0]
    n_prompt = nb * seq
    tm = 512
    assert seq % (2 * MOBA_BLOCK) == 0 and x_sample.shape[1] == 1 and seq // MOBA_BLOCK < LANES
    bias_own, bias_prev, bias_samp = _bias_tiles(rel_bias)
    xp = x_prompt.reshape(n_prompt, D_MODEL)
    xs = x_sample.reshape(ns, D_MODEL)
    zero_state = jnp.zeros((nb, GLA_V_W, GLA_QK_W), F32)
    outs = dict(pm=[], ps=[], pg=[], sm=[], ss=[], sg=[])
    wts = _prep_weights(norm1, w_in, w_alpha, b_alpha, gla_norm, moba_q_norm, moba_k_norm, sb_q_norm,
                        sb_k_norm, w_out, norm2, w_gate_up, w_down)
    for l in range(DEPTH):
        gq, gk, gl, gv, gg, mq, mkv32, mvt, mk, mkm, sq, skv32, skt, sv = _inproj(xp, wts, l, tm, seq=seq)
        og, st = _gla(gq, gk, gl, gv, gg, wts['gnorm'], l, zero_state, batch=nb, seq=seq, chunk=GLA_CHUNK,
                      chunks_per_step=8)
        osb = _sb_prompt(sq, skt, sv, batch=nb, seq=seq)
        kmean = jnp.pad(mkm.reshape(nb, seq // MOBA_BLOCK, LANES), ((0, 0), (0, LANES - seq // MOBA_BLOCK), (0, 0)))
        om = _moba_prompt(mq, mk, mvt, kmean, bias_own, bias_prev, batch=nb, seq=seq)
        xp = _ffn(xp, og, om, osb, wts, l, tm)
        leaf = lambda a: a.reshape(nb, 2, KV_HEADS, HEAD_DIM, seq).transpose(0, 4, 1, 2, 3)
        outs['pm'].append(leaf(mkv32))
        outs['ps'].append(leaf(skv32))
        outs['pg'].append(_blockdiag_T_to_state(st))
        gq, gk, gl, gv, gg, mq, mkv32, mkv16, sq, skv32, _ = _inproj(xs, wts, l, ns)
        og, st = _gla_sample(gq, gk, gl, gv, gg, wts['gnorm'], l, _state_to_blockdiag_T(state_gla[l]))
        osb = _sb_sample(sq, cache_sb_kv, page_table, l)
        om = _moba_sample(mq, mkv16, cache_moba_kv, page_table, bias_samp, l)
        xs = _ffn(xs, og, om, osb, wts, l, ns)
        outs['sm'].append(mkv32.reshape(ns, 1, 2, KV_HEADS, HEAD_DIM))
        outs['ss'].append(skv32.reshape(ns, 1, 2, KV_HEADS, HEAD_DIM))
        outs['sg'].append(_blockdiag_T_to_state(st))
    return (xp.reshape(nb, seq, D_MODEL), xs.reshape(ns, 1, D_MODEL), jnp.stack(outs['pm']), jnp.stack(outs['ps']),
            jnp.stack(outs['pg']), jnp.stack(outs['sm']), jnp.stack(outs['ss']), jnp.stack(outs['sg']))
```

```python
import functools
import math

import jax
import jax.numpy as jnp
import numpy as np
from jax import lax
from jax.experimental import pallas as pl
from jax.experimental.pallas import tpu as pltpu

F32 = jnp.float32
BF16 = jnp.bfloat16

D_MODEL = 1024
DEPTH = 4
HEAD_DIM = 64
GLA_HEADS = 4
GLA_DK = 32
GLA_DV = 64
GLA_RANK = 16
GLA_TAU = 16.0
GLA_CHUNK = 64
GLA_SUB = 16
ATT_HEADS = 6
KV_HEADS = 2
GROUP = ATT_HEADS // KV_HEADS
MOBA_BLOCK = 256
MOBA_TOPK = 3
REL_BUCKETS = 32
REL_MAX_DIST = 128
RMS_EPS = 1e-6
NEG_INF = -1e30
PAGE_SIZE = 128
D_FF = 2816
GLA_QK_W = GLA_HEADS * GLA_DK
GLA_V_W = GLA_HEADS * GLA_DV
ATT_Q_W = ATT_HEADS * HEAD_DIM
ATT_KV_W = 2 * KV_HEADS * HEAD_DIM
LANES = 128
ATT_TILE = 256
ATT_ROWS = ATT_HEADS * ATT_TILE
SB_ROW_TILE = 1536
MOBA_ROW_TILE = 128
VMEM_LIMIT = 56 * 1024 * 1024
PAGES_PER_STEP = 16
MOBA_PAGES_PER_STEP = 32

_OFF = dict(gq=0, gk=128, gv=256, gg=512, ga=768, mq=784, mk=1168, mv=1296, sq=1424, sk=1808, sv=1936)
IN_W_PAD = 2176
_HEAD_PERM = np.concatenate([np.concatenate([np.arange(64) + 64 * j, np.arange(64) + 64 * (GROUP + j)])
                             for j in range(GROUP)])


def _t5_thresholds():
    n = np.arange(0, 4 * REL_MAX_DIST, dtype=np.int64)
    max_exact = REL_BUCKETS // 2
    nf = np.maximum(n, 1).astype(np.float32)
    large = max_exact + (np.log(nf / np.float32(max_exact)) / np.float32(math.log(REL_MAX_DIST / max_exact))
                         * np.float32(REL_BUCKETS - max_exact)).astype(np.int32)
    bucket = np.where(n < max_exact, n, np.minimum(large, REL_BUCKETS - 1))
    return [int(np.argmax(bucket >= b)) for b in range(REL_BUCKETS)]


_T5_THR = _t5_thresholds()


def _dot(a, b):
    return jnp.dot(a, b, preferred_element_type=F32)


def _dot_nt(a, b):
    return lax.dot_general(a, b, (((1,), (1,)), ((), ())), preferred_element_type=F32)


def _split_hilo(a):
    hi = a.astype(BF16)
    lo = (a - hi.astype(F32)).astype(BF16)
    return hi, lo


def _dot_hilo(a, b_bf16):
    hi, lo = _split_hilo(a)
    return _dot(hi, b_bf16) + _dot(lo, b_bf16)


def _group_mean_matrix(width, group):
    r = lax.broadcasted_iota(jnp.int32, (width, width), 0) // group
    c = lax.broadcasted_iota(jnp.int32, (width, width), 1) // group
    return jnp.where(r == c, 1.0 / group, 0.0).astype(BF16)


def _neg_softplus(z):
    return -(jnp.maximum(z, 0.0) + jnp.log(1.0 + jnp.exp(-jnp.abs(z))))


def _log_sigmoid(x):
    return jnp.minimum(x, 0.0) - jnp.log(1.0 + jnp.exp(-jnp.abs(x)))


def _silu(x):
    return x / (1.0 + jnp.exp(-x))


def _head_rms(x, gain, group):
    ms = _dot_hilo(x * x, _group_mean_matrix(x.shape[1], group))
    return x * lax.rsqrt(ms + RMS_EPS) * gain


def _head_rms_lanes(x, gain):
    lane = lax.broadcasted_iota(jnp.int32, (x.shape[0], LANES), 1)
    low = lane < HEAD_DIM
    cols = []
    for c in range(x.shape[1] // LANES):
        xc = x[:, c * LANES:(c + 1) * LANES]
        x2 = xc * xc
        s_low = jnp.sum(jnp.where(low, x2, 0.0), axis=1, keepdims=True)
        s_high = jnp.sum(jnp.where(low, 0.0, x2), axis=1, keepdims=True)
        ms = jnp.where(low, s_low, s_high) * (1.0 / HEAD_DIM)
        cols.append(xc * lax.rsqrt(ms + RMS_EPS))
    return jnp.concatenate(cols, axis=1) * gain


def _inproj_kernel(x_ref, g1_ref, w_ref, wa_ref, ba_ref, mqg_ref, mkg_ref, sqg_ref, skg_ref,
                   gq_ref, gk_ref, gl_ref, gv_ref, gg_ref, *att_refs, prompt):
    x = x_ref[...]
    ms = jnp.mean(x * x, axis=-1, keepdims=True)
    h = (x * lax.rsqrt(ms + RMS_EPS) * g1_ref[0]).astype(BF16)
    p = _dot(h, w_ref[0])
    gq_ref[...] = p[:, 0:128] * (GLA_DK ** -0.5)
    gk_ref[...] = p[:, 128:256]
    gv_ref[...] = p[:, 256:512]
    gg_ref[...] = p[:, 512:768]
    alpha = _dot(p[:, 768:896].astype(BF16), wa_ref[0]) + ba_ref[0]
    gl_ref[...] = _log_sigmoid(alpha) / GLA_TAU

    def attn_group(base, qg_ref, kg_ref, refs, tiles_of_k=True):
        q = _head_rms_lanes(p[:, base:base + ATT_Q_W], qg_ref[0]) * (HEAD_DIM ** -0.5)
        refs[0][...] = q.astype(BF16)
        k = _head_rms_lanes(p[:, base + 384:base + 512], kg_ref[0])
        v = p[:, base + 512:base + 640]
        if prompt:
            _, kv32t_ref, tiles_ref, rows_ref = refs
            kt, vt = k.T, v.T
            kv32t_ref[0, 0:128, :] = kt
            kv32t_ref[0, 128:256, :] = vt
            tiled = (kt if tiles_of_k else vt).astype(BF16)
            for c in range(tiles_ref.shape[1]):
                tiles_ref[0, c] = tiled[:, c * ATT_TILE:(c + 1) * ATT_TILE]
            rows_ref[...] = (v if tiles_of_k else k).astype(BF16)
        else:
            _, kv32_ref, kv16_ref = refs
            kv32_ref[:, 0:128] = k
            kv32_ref[:, 128:256] = v
            kv16_ref[:, 0:128] = k.astype(BF16)
            kv16_ref[:, 128:256] = v.astype(BF16)
        return k

    if prompt:
        mk = attn_group(896, mqg_ref, mkg_ref, att_refs[0:4], tiles_of_k=False)
        attn_group(1536, sqg_ref, skg_ref, att_refs[5:9])
        mkm_ref = att_refs[4]
        for j in range(mkm_ref.shape[1]):
            mkm_ref[0, j:j + 1, :] = jnp.mean(mk[j * MOBA_BLOCK:(j + 1) * MOBA_BLOCK], axis=0, keepdims=True)
    else:
        attn_group(896, mqg_ref, mkg_ref, att_refs[0:3])
        attn_group(1536, sqg_ref, skg_ref, att_refs[3:6])


def _layer_spec(a, layer):
    return pl.BlockSpec((1,) + a.shape[1:], lambda *_: (layer, 0, 0))


def _inproj(x, wts, layer, tm, seq=None):
    n = x.shape[0]
    grid = n // tm
    row = lambda w: pl.BlockSpec((tm, w), lambda i: (i, 0))
    sds = jax.ShapeDtypeStruct
    outs = [sds((n, 128), F32), sds((n, 128), F32), sds((n, 128), F32), sds((n, 256), F32), sds((n, 256), F32)]
    out_specs = [row(128), row(128), row(128), row(256), row(256)]
    if seq is None:
        group = [(sds((n, ATT_Q_W), BF16), row(ATT_Q_W)), (sds((n, ATT_KV_W), F32), row(ATT_KV_W)),
                 (sds((n, ATT_KV_W), BF16), row(ATT_KV_W))]
        att = group + group
    else:
        tiles = seq // tm
        nblk = tm // ATT_TILE
        group = [(sds((n, ATT_Q_W), BF16), row(ATT_Q_W)),
                 (sds((n // seq, ATT_KV_W, seq), F32),
                  pl.BlockSpec((1, ATT_KV_W, tm), lambda i: (i // tiles, 0, i % tiles))),
                 (sds((n // seq, seq // ATT_TILE, 128, ATT_TILE), BF16),
                  pl.BlockSpec((1, nblk, 128, ATT_TILE), lambda i: (i // tiles, i % tiles, 0, 0))),
                 (sds((n, 128), BF16), row(128))]
        means = (sds((grid, nblk, 128), F32), pl.BlockSpec((1, nblk, 128), lambda i: (i, 0, 0)))
        att = group + [means] + group
    outs += [a for a, _ in att]
    out_specs += [b for _, b in att]
    ins = [x] + [wts[name] for name in ('g1', 'w_in', 'w_alpha', 'b_alpha', 'mqg', 'mkg', 'sqg', 'skg')]
    in_specs = [row(D_MODEL)] + [_layer_spec(a, layer) for a in ins[1:]]
    return pl.pallas_call(
        functools.partial(_inproj_kernel, prompt=seq is not None), grid=(grid,),
        in_specs=in_specs, out_specs=out_specs, out_shape=outs,
        compiler_params=pltpu.CompilerParams(dimension_semantics=("arbitrary",), vmem_limit_bytes=VMEM_LIMIT),
        name=f"inproj_{tm}")(*ins)


def _gla_kernel(q_ref, k_ref, gl_ref, v_ref, gg_ref, gn_ref, s0_ref, o_ref, sT_ref,
                st_scr, p_scr, w_scr, *, chunk, n_chunks):
    c = chunk
    step = pl.program_id(0)
    n_seq = q_ref.shape[0]

    @pl.when(step == 0)
    def _():
        st_scr[...] = s0_ref[...]

    ri = lax.broadcasted_iota(jnp.int32, (c, c), 0)
    ci = lax.broadcasted_iota(jnp.int32, (c, c), 1)
    ltri = jnp.where(ri >= ci, 1.0, 0.0).astype(BF16)
    kh = lax.broadcasted_iota(jnp.int32, (GLA_QK_W, GLA_V_W), 0) // GLA_DK
    vh = lax.broadcasted_iota(jnp.int32, (GLA_QK_W, GLA_V_W), 1) // GLA_DV
    head_ones = jnp.where(kh == vh, 1.0, 0.0).astype(BF16)
    vh2 = lax.broadcasted_iota(jnp.int32, (GLA_V_W, GLA_QK_W), 0) // GLA_DV
    kh2 = lax.broadcasted_iota(jnp.int32, (GLA_V_W, GLA_QK_W), 1) // GLA_DK
    bd_mask = vh2 == kh2
    sc = GLA_SUB
    n_sub = c // sc
    trow = lax.broadcasted_iota(jnp.int32, (sc, GLA_QK_W), 0)
    qk_head = lax.broadcasted_iota(jnp.int32, (sc, GLA_QK_W), 1) // GLA_DK
    v_head = lax.broadcasted_iota(jnp.int32, (sc, GLA_V_W), 1) // GLA_DV
    key_col = lax.broadcasted_iota(jnp.int32, (GLA_HEADS * sc, c), 1)

    def chunks(ic, carry):
        r0 = pl.multiple_of(ic * c, c)
        seqs = range(n_seq)
        ld = lambda ref: [ref[bi, pl.ds(r0, c), :] for bi in seqs]
        q, k, g, v = ld(q_ref), ld(k_ref), ld(gl_ref), ld(v_ref)
        b = []
        for bi in seqs:
            g_hi, g_lo = _split_hilo(g[bi])
            b.append(_dot(ltri, g_hi) + _dot(ltri, g_lo))
        st = [st_scr[bi] for bi in seqs]
        o_inter = [_dot_nt((q[bi] * jnp.exp(b[bi])).astype(BF16), st[bi].astype(BF16)) for bi in seqs]
        v16 = [v[bi].astype(BF16) for bi in seqs]

        s4 = {}
        for i_sub in range(1, n_sub):
            rows = slice(i_sub * sc, (i_sub + 1) * sc)
            for bi in seqs:
                e = b[bi][i_sub * sc - 1:i_sub * sc, :]
                a = q[bi][rows] * jnp.exp(b[bi][rows] - e)
                a4 = jnp.concatenate([jnp.where(qk_head == h, a, 0.0) for h in range(GLA_HEADS)], axis=0)
                kd = (k[bi] * jnp.exp(jnp.minimum(e - b[bi], 0.0))).astype(BF16)
                s4[bi, i_sub] = jnp.where(key_col < i_sub * sc, _dot_nt(a4.astype(BF16), kd), 0.0)

        for i_sub in range(n_sub):
            rows = slice(i_sub * sc, (i_sub + 1) * sc)
            for bi in seqs:
                q_i, b_i, k_i = q[bi][rows], b[bi][rows], k[bi][rows]
                for s in range(sc):
                    sl = q_i * jnp.exp(jnp.minimum(b_i - b_i[s:s + 1], 0.0)) * k_i[s:s + 1]
                    off = (i_sub * sc + s) * sc
                    p_scr[bi, off:off + sc, :] = jnp.where(trow >= s, sl, 0.0)
        for bi in seqs:
            w_scr[bi] = _dot_hilo(p_scr[bi], head_ones)

        for bi in seqs:
            b_last = b[bi][c - 1:c, :]
            kt = (k[bi] * jnp.exp(b_last - b[bi])).astype(BF16)
            upd = _dot(v[bi].T.astype(BF16), kt)
            st_scr[bi] = st[bi] * jnp.exp(b_last) + jnp.where(bd_mask, upd, 0.0)

        o4 = {key: _dot(val.astype(BF16), v16[key[0]]) for key, val in s4.items()}

        for i_sub in range(n_sub):
            rows = slice(i_sub * sc, (i_sub + 1) * sc)
            for bi in seqs:
                o_i = o_inter[bi][rows]
                for s in range(sc):
                    off = (i_sub * sc + s) * sc
                    o_i = o_i + w_scr[bi, off:off + sc, :] * v[bi][i_sub * sc + s:i_sub * sc + s + 1, :]
                if i_sub > 0:
                    for h in range(GLA_HEADS):
                        o_i = o_i + jnp.where(v_head == h, o4[bi, i_sub][h * sc:(h + 1) * sc], 0.0)
                on = _head_rms(o_i, gn_ref[0], GLA_DV)
                dst = pl.ds(r0 + i_sub * sc, sc)
                o_ref[bi, dst, :] = (on * _silu(gg_ref[bi, dst, :])).astype(o_ref.dtype)
        return carry
    lax.fori_loop(0, n_chunks, chunks, 0)

    @pl.when(step == pl.num_programs(0) - 1)
    def _():
        sT_ref[...] = st_scr[...]


def _gla(gq, gk, gl, gv, gg, gnorm, layer, s0T, *, batch, seq, chunk, chunks_per_step):
    rows = chunk * chunks_per_step
    row = lambda w: pl.BlockSpec((batch, rows, w), lambda i: (0, i, 0))
    state = pl.BlockSpec((batch, GLA_V_W, GLA_QK_W), lambda i: (0, 0, 0))
    r3 = lambda a: a.reshape(batch, seq, a.shape[-1])
    kern = functools.partial(_gla_kernel, chunk=chunk, n_chunks=chunks_per_step)
    o, st = pl.pallas_call(
        kern, grid=(seq // rows,),
        in_specs=[row(128), row(128), row(128), row(256), row(256),
                  _layer_spec(gnorm, layer), state],
        out_specs=[row(256), state],
        out_shape=[jax.ShapeDtypeStruct((batch, seq, GLA_V_W), BF16),
                   jax.ShapeDtypeStruct((batch, GLA_V_W, GLA_QK_W), F32)],
        scratch_shapes=[pltpu.VMEM((batch, GLA_V_W, GLA_QK_W), F32),
                        pltpu.VMEM((batch, chunk * GLA_SUB, GLA_QK_W), F32),
                        pltpu.VMEM((batch, chunk * GLA_SUB, GLA_V_W), F32)],
        compiler_params=pltpu.CompilerParams(dimension_semantics=("arbitrary",), vmem_limit_bytes=VMEM_LIMIT),
        name="gla_prompt")(r3(gq), r3(gk), r3(gl), r3(gv), r3(gg), gnorm, s0T)
    return o.reshape(batch * seq, GLA_V_W), st


def _gla_sample_kernel(q_ref, k_ref, gl_ref, v_ref, gg_ref, gn_ref, s0_ref, o_ref, sT_ref):
    rows = (8, GLA_QK_W)
    decay = jnp.exp(gl_ref[0])
    r = lax.broadcasted_iota(jnp.int32, (GLA_V_W, GLA_V_W), 0)
    cc = lax.broadcasted_iota(jnp.int32, (GLA_V_W, GLA_V_W), 1)
    v_diag = jnp.where(r == cc, jnp.broadcast_to(v_ref[0], (GLA_V_W, GLA_V_W)), 0.0).astype(BF16)
    k_rows = jnp.broadcast_to(k_ref[0], (GLA_V_W, GLA_QK_W)).astype(BF16)
    vh = lax.broadcasted_iota(jnp.int32, (GLA_V_W, GLA_QK_W), 0) // GLA_DV
    kh = lax.broadcasted_iota(jnp.int32, (GLA_V_W, GLA_QK_W), 1) // GLA_DK
    outer = jnp.where(vh == kh, _dot(v_diag, k_rows), 0.0)
    st = s0_ref[0] * decay + outer
    sT_ref[0] = st
    q8 = jnp.broadcast_to(q_ref[0], rows).astype(BF16)
    o = _dot_nt(q8, st.astype(BF16))
    on = _head_rms(o, gn_ref[0], GLA_DV)
    o_ref[0] = (on * _silu(gg_ref[0]))[0:1].astype(o_ref.dtype)


def _gla_sample(gq, gk, gl, gv, gg, gnorm, layer, s0T):
    n = gq.shape[0]
    r3 = lambda a: a.reshape(n, 1, a.shape[-1])
    row = lambda w: pl.BlockSpec((1, 1, w), lambda b: (b, 0, 0))
    st_spec = pl.BlockSpec((1, GLA_V_W, GLA_QK_W), lambda b: (b, 0, 0))
    o, st = pl.pallas_call(
        _gla_sample_kernel, grid=(n,),
        in_specs=[row(128), row(128), row(128), row(256), row(256),
                  _layer_spec(gnorm, layer), st_spec],
        out_specs=[row(256), st_spec],
        out_shape=[jax.ShapeDtypeStruct((n, 1, GLA_V_W), BF16), jax.ShapeDtypeStruct((n, GLA_V_W, GLA_QK_W), F32)],
        compiler_params=pltpu.CompilerParams(dimension_semantics=("arbitrary",), vmem_limit_bytes=VMEM_LIMIT),
        name="gla_sample")(r3(gq), r3(gk), r3(gl), r3(gv), r3(gg), gnorm, s0T)
    return o.reshape(n, GLA_V_W), st


def _stack_heads(q_cols):
    lane = lax.broadcasted_iota(jnp.int32, q_cols[0].shape, 1)
    low = lane < HEAD_DIM
    zero = jnp.zeros_like(q_cols[0])
    parts = [jnp.where(low, qc, zero) for qc in q_cols] + [jnp.where(low, zero, qc) for qc in q_cols]
    return jnp.concatenate(parts, axis=0)


def _unstack_heads(o, rows):
    lane = lax.broadcasted_iota(jnp.int32, (rows, LANES), 1)
    low = lane < HEAD_DIM
    return [jnp.where(low, o[j * rows:(j + 1) * rows], o[(GROUP + j) * rows:(GROUP + j + 1) * rows])
            for j in range(GROUP)]


def _suffix_matrix():
    r = lax.broadcasted_iota(jnp.int32, (ATT_TILE, ATT_TILE), 0)
    c = lax.broadcasted_iota(jnp.int32, (ATT_TILE, ATT_TILE), 1)
    u = jnp.where(r > c, 1.0, 0.0).astype(BF16)
    return jnp.concatenate([u, u], axis=0)


def _twice(a):
    return jnp.concatenate([a, a], axis=1)


def _sb_rows(q, kt, v, u2, carry, mask):
    z = _dot(q, kt)
    l = _neg_softplus(z)
    if mask is not None:
        l = jnp.where(mask, l, 0.0)
    hi, lo = _split_hilo(l)
    c = _dot(jnp.concatenate([hi, lo], axis=1), u2)
    w = jnp.exp(z + l + c + _twice(carry))
    if mask is not None:
        w = jnp.where(mask, w, 0.0)
    return _dot(w.astype(BF16), v), carry + jnp.sum(l, axis=1, keepdims=True)


SB_STOP = -104.0


def _sb_prompt_kernel(q_ref, kt_ref, v_ref, o_ref, q_scr, carry_scr, acc_scr):
    i = pl.program_id(1)
    t = ATT_TILE
    q_scr[...] = _stack_heads([q_ref[0, :, 128 * j:128 * (j + 1)] for j in range(GROUP)])
    carry_scr[...] = jnp.zeros(carry_scr.shape, F32)
    acc_scr[...] = jnp.zeros(acc_scr.shape, F32)
    u2 = _suffix_matrix()
    rt = SB_ROW_TILE
    rr = lax.broadcasted_iota(jnp.int32, (rt, t), 0) & (t - 1)
    cc = lax.broadcasted_iota(jnp.int32, (rt, t), 1)

    def tile(kj, diagonal):
        kt = kt_ref[0, kj]
        v = v_ref[0, pl.ds(pl.multiple_of(kj * t, t), t), :]
        for r in range(ATT_ROWS // rt):
            rows = slice(r * rt, (r + 1) * rt)
            mask = cc < rr + (r * rt) % t if diagonal else None
            pv, carry = _sb_rows(q_scr[rows, :], kt, v, u2, carry_scr[rows, :], mask)
            acc_scr[rows, :] += pv
            carry_scr[rows, :] = carry

    def carry_max():
        return jnp.max(jnp.max(carry_scr[...], axis=0, keepdims=True), axis=1, keepdims=True)[0, 0]

    tile(i, True)

    def cond(state):
        kj, cmax = state
        return jnp.logical_and(kj >= 0, cmax > SB_STOP)

    def body(state):
        kj, _ = state
        tile(kj, False)
        return kj - 1, carry_max()
    lax.while_loop(cond, body, (i - 1, carry_max()))

    cols = _unstack_heads(acc_scr[...], t)
    for j in range(GROUP):
        o_ref[0, :, 128 * j:128 * (j + 1)] = cols[j].astype(o_ref.dtype)


def _sb_prompt(q, kt, v, *, batch, seq):
    t = ATT_TILE
    out = pl.pallas_call(
        _sb_prompt_kernel, grid=(batch, seq // t),
        in_specs=[pl.BlockSpec((1, t, ATT_Q_W), lambda b, i: (b, i, 0)),
                  pl.BlockSpec((1, seq // t, LANES, t), lambda b, i: (b, 0, 0, 0)),
                  pl.BlockSpec((1, seq, LANES), lambda b, i: (b, 0, 0))],
        out_specs=pl.BlockSpec((1, t, ATT_Q_W), lambda b, i: (b, i, 0)),
        out_shape=jax.ShapeDtypeStruct((batch, seq, ATT_Q_W), BF16),
        scratch_shapes=[pltpu.VMEM((ATT_ROWS, LANES), BF16), pltpu.VMEM((ATT_ROWS, LANES), F32),
                        pltpu.VMEM((ATT_ROWS, LANES), F32)],
        compiler_params=pltpu.CompilerParams(dimension_semantics=("arbitrary", "arbitrary"),
                                             vmem_limit_bytes=VMEM_LIMIT),
        name="sb_prompt")(q.reshape(batch, seq, ATT_Q_W), kt, v.reshape(batch, seq, LANES))
    return out.reshape(batch * seq, ATT_Q_W)


def _bias_of_dist(dist, rb_ref, h):
    bias = jnp.full(dist.shape, rb_ref[0, h], F32)
    for b in range(1, REL_BUCKETS):
        bias = jnp.where(dist >= _T5_THR[b], rb_ref[b, h], bias)
    return bias - rb_ref[REL_BUCKETS - 1, h]


def _bias_kernel(rb_ref, own_ref, prev_ref, samp_ref):
    t = ATT_TILE
    key = lax.broadcasted_iota(jnp.int32, (t, t), 0)
    qry = lax.broadcasted_iota(jnp.int32, (t, t), 1)
    for h in range(ATT_HEADS):
        own = _bias_of_dist(jnp.maximum(qry - key, 0), rb_ref, h)
        own_ref[:, h * t:(h + 1) * t] = jnp.where(key <= qry, own, NEG_INF)
        prev_ref[:, h * t:(h + 1) * t] = _bias_of_dist(qry - key + t, rb_ref, h)
    s = lax.broadcasted_iota(jnp.int32, (8, t), 1)
    hrow = lax.broadcasted_iota(jnp.int32, (8, t), 0)
    last = jnp.zeros((8, t), F32)
    self_b = jnp.zeros((8, LANES), F32)
    hrow2 = lax.broadcasted_iota(jnp.int32, (8, LANES), 0)
    for h in range(ATT_HEADS):
        last = jnp.where(hrow == h, _bias_of_dist(t - s, rb_ref, h), last)
        self_b = jnp.where(hrow2 == h, rb_ref[0, h] - rb_ref[REL_BUCKETS - 1, h], self_b)
    samp_ref[:, 0:t] = last
    samp_ref[:, t:t + LANES] = self_b


def _bias_tiles(rel_bias):
    t = ATT_TILE
    return pl.pallas_call(
        _bias_kernel,
        in_specs=[pl.BlockSpec(memory_space=pltpu.SMEM)],
        out_shape=[jax.ShapeDtypeStruct((t, ATT_ROWS), F32), jax.ShapeDtypeStruct((t, ATT_ROWS), F32),
                   jax.ShapeDtypeStruct((8, t + LANES), F32)],
        name="t5_bias_tiles")(rel_bias)


def _top3_select(gate, n_valid, axis):
    blk_i = lax.broadcasted_iota(jnp.int32, gate.shape, axis)
    blk = blk_i.astype(F32)
    valid = blk_i < n_valid
    g = jnp.where(valid, gate, NEG_INF)
    sel = jnp.zeros(gate.shape, F32)
    for _ in range(MOBA_TOPK):
        m = jnp.max(g, axis=axis, keepdims=True)
        idx = jnp.min(jnp.where(g == m, blk, float(LANES)), axis=axis, keepdims=True)
        pick = blk == idx
        sel = jnp.where(pick, 1.0, sel)
        g = jnp.where(pick, -jnp.inf, g)
    return jnp.where(valid, sel, 0.0) > 0.5


MOBA_COL_TILE = 256
MOBA_ACC_ROWS = LANES + 16


def _moba_prompt_kernel(q_ref, k_ref, vt_ref, km_ref, own_ref, prev_ref, o_ref,
                        qx_scr, p_scr, al_scr, m_scr, acc_scr):
    i = pl.program_id(1)
    t = ATT_TILE
    q = _stack_heads([q_ref[0, :, 128 * j:128 * (j + 1)] for j in range(GROUP)])
    qt = q.astype(F32).T.astype(BF16)
    n_blk = k_ref.shape[1] // t
    gate = _dot(km_ref[0, 0:n_blk, :].astype(BF16), qt)
    sel = _top3_select(gate, i, 0)
    qx_scr[0:LANES, :] = qt
    qx_scr[LANES:LANES + n_blk, :] = jnp.where(sel, 0.0, NEG_INF).astype(BF16)
    qx_scr[LANES + n_blk:2 * LANES, :] = jnp.full((LANES - n_blk, ATT_ROWS), NEG_INF, BF16)
    blk_col = lax.broadcasted_iota(jnp.int32, (t, LANES), 1)
    ones = jnp.ones((MOBA_ACC_ROWS - LANES, 2 * t), BF16)
    ct = MOBA_COL_TILE
    col_tiles = [slice(c * ct, (c + 1) * ct) for c in range(ATT_ROWS // ct)]

    def keys(kj):
        return k_ref[0, pl.ds(pl.multiple_of(jnp.maximum(kj, 0) * t, t), t), :]

    def past_keys(kj):
        blk = jnp.where(kj >= 0, kj, LANES - 1)
        return jnp.concatenate([keys(kj), jnp.where(blk_col == blk, 1.0, 0.0).astype(BF16)], axis=1)

    def pair_keys(kj):
        return jnp.concatenate([past_keys(kj), past_keys(kj - 1)], axis=0)

    def pair_values(kj):
        clamp = lambda j: jnp.clip(j, 0, n_blk - 1)
        vt = jnp.concatenate([vt_ref[0, clamp(kj)], vt_ref[0, clamp(kj - 1)]], axis=1)
        return jnp.concatenate([vt, ones], axis=0)

    k_own = keys(i)
    vx = jnp.concatenate([vt_ref[0, i], ones[:, 0:t]], axis=0)
    for cols in col_tiles:
        s = _dot(k_own, qx_scr[0:LANES, cols]) + own_ref[:, cols]
        m0 = jnp.max(s, axis=0, keepdims=True)
        acc_scr[:, cols] = _dot(vx, jnp.exp(s - m0).astype(BF16))
        m_scr[:, cols] = jnp.broadcast_to(m0, (8, ct))

    def probabilities(kx, cols, with_prev_bias):
        s = _dot(kx, qx_scr[:, cols])
        if with_prev_bias:
            s = jnp.concatenate([s[0:t] + prev_ref[:, cols], s[t:2 * t]], axis=0)
        m_old = m_scr[0:1, cols]
        m_new = jnp.maximum(m_old, jnp.max(s, axis=0, keepdims=True))
        p_scr[:, cols] = jnp.exp(s - m_new).astype(BF16)
        al_scr[:, cols] = jnp.broadcast_to(jnp.exp(m_old - m_new), (8, ct))
        m_scr[:, cols] = jnp.broadcast_to(m_new, (8, ct))

    def accumulate(vx, cols):
        acc_scr[:, cols] = al_scr[0:1, cols] * acc_scr[:, cols] + _dot(vx, p_scr[:, cols])

    @pl.when(i >= 1)
    def _():
        kx = pair_keys(i - 1)
        for cols in col_tiles:
            probabilities(kx, cols, True)

    n_pairs = jnp.maximum(i - 1, 0) // 2

    def body(n, carry):
        kj = i - 3 - 2 * n
        vx, kx = pair_values(kj + 2), pair_keys(kj)
        for cols in col_tiles:
            accumulate(vx, cols)
            probabilities(kx, cols, False)
        return carry
    lax.fori_loop(0, n_pairs, body, 0)

    @pl.when(i >= 1)
    def _():
        vx = pair_values(i - 1 - 2 * n_pairs)
        for cols in col_tiles:
            accumulate(vx, cols)

    o = (acc_scr[0:LANES, :] / acc_scr[LANES:LANES + 1, :]).T
    cols = _unstack_heads(o, t)
    for j in range(GROUP):
        o_ref[0, :, 128 * j:128 * (j + 1)] = cols[j].astype(o_ref.dtype)


def _moba_prompt(q, k, vt, kmean, bias_own, bias_prev, *, batch, seq):
    t = ATT_TILE
    const = lambda a: pl.BlockSpec(a.shape, lambda b, i: (0, 0))
    out = pl.pallas_call(
        _moba_prompt_kernel, grid=(batch, seq // t),
        in_specs=[pl.BlockSpec((1, t, ATT_Q_W), lambda b, i: (b, i, 0)),
                  pl.BlockSpec((1, seq, LANES), lambda b, i: (b, 0, 0)),
                  pl.BlockSpec((1, seq // t, LANES, t), lambda b, i: (b, 0, 0, 0)),
                  pl.BlockSpec((1, LANES, LANES), lambda b, i: (b, 0, 0)),
                  const(bias_own), const(bias_prev)],
        out_specs=pl.BlockSpec((1, t, ATT_Q_W), lambda b, i: (b, i, 0)),
        out_shape=jax.ShapeDtypeStruct((batch, seq, ATT_Q_W), BF16),
        scratch_shapes=[pltpu.VMEM((2 * LANES, ATT_ROWS), BF16), pltpu.VMEM((2 * t, ATT_ROWS), BF16),
                        pltpu.VMEM((8, ATT_ROWS), F32), pltpu.VMEM((8, ATT_ROWS), F32),
                        pltpu.VMEM((MOBA_ACC_ROWS, ATT_ROWS), F32)],
        compiler_params=pltpu.CompilerParams(dimension_semantics=("arbitrary", "arbitrary"),
                                             vmem_limit_bytes=VMEM_LIMIT),
        name="moba_prompt")(q.reshape(batch, seq, ATT_Q_W), k.reshape(batch, seq, LANES), vt, kmean,
                            bias_own, bias_prev)
    return out.reshape(batch * seq, ATT_Q_W)


def _sample_q_rows(q_row):
    row = lax.broadcasted_iota(jnp.int32, (8, LANES), 0)
    lane = lax.broadcasted_iota(jnp.int32, (8, LANES), 1)
    qf = q_row.astype(F32)
    out = jnp.zeros((8, LANES), F32)
    for h in range(ATT_HEADS):
        j, n = h % GROUP, h // GROUP
        col = jnp.broadcast_to(qf[:, 128 * j:128 * (j + 1)], (8, LANES))
        half = lane >= HEAD_DIM if n == 1 else lane < HEAD_DIM
        out = jnp.where(jnp.logical_and(row == h, half), col, out)
    return out.astype(BF16)


def _sample_o_row(o):
    lane = lax.broadcasted_iota(jnp.int32, (1, LANES), 1)
    return [jnp.where(lane < HEAD_DIM, o[j:j + 1], o[GROUP + j:GROUP + j + 1]) for j in range(GROUP)]


TILES_PER_STEP = PAGES_PER_STEP * PAGE_SIZE // ATT_TILE


def _step_keys_values(pages):
    kt = jnp.concatenate([p[0, 0:128, :] for p in pages], axis=1).astype(BF16)
    vt = jnp.concatenate([p[0, 128:256, :] for p in pages], axis=1).astype(BF16)
    return kt, vt


def _tiles_to_rows(a):
    return jnp.concatenate([a[:, c * ATT_TILE:(c + 1) * ATT_TILE] for c in range(a.shape[1] // ATT_TILE)], axis=0)


def _rows_to_tiles(a):
    return jnp.concatenate([a[8 * c:8 * (c + 1), :] for c in range(a.shape[0] // 8)], axis=1)


def _sb_sample_kernel(pt_ref, q_ref, *refs):
    del pt_ref
    pages = refs[:PAGES_PER_STEP]
    o_ref, carry_ref, carry_scr, acc_scr = refs[PAGES_PER_STEP:]
    step = pl.program_id(1)

    @pl.when(step == 0)
    def _():
        carry_scr[...] = jnp.zeros(carry_scr.shape, F32)
        acc_scr[...] = jnp.zeros(acc_scr.shape, F32)

    q = _sample_q_rows(q_ref[0])
    kt, vt = _step_keys_values(pages)
    z = _tiles_to_rows(_dot(q, kt))
    l = _neg_softplus(z)
    hi, lo = _split_hilo(l)
    c = _dot(jnp.concatenate([hi, lo], axis=1), _suffix_matrix())
    tile_sum = jnp.sum(l, axis=1, keepdims=True)
    carry = carry_scr[...]
    carries = [None] * TILES_PER_STEP
    for tl in reversed(range(TILES_PER_STEP)):
        carries[tl] = carry
        carry = carry + tile_sum[8 * tl:8 * (tl + 1)]
    w = jnp.exp(z + l + c + _twice(jnp.concatenate(carries, axis=0)))
    acc = acc_scr[...] + _dot_nt(_rows_to_tiles(w).astype(BF16), vt)
    carry_scr[...] = carry
    acc_scr[...] = acc

    @pl.when(step == pl.num_programs(1) - 1)
    def _():
        cols = _sample_o_row(acc)
        for j in range(GROUP):
            o_ref[0, :, 128 * j:128 * (j + 1)] = cols[j].astype(o_ref.dtype)
        carry_ref[0] = carry


def _page_specs(layer, n_pool, n_steps, reverse, pages_per_step):
    specs = []
    for p in range(pages_per_step):
        def imap(b, c, pt, p=p):
            cc = (n_steps - 1 - c) if reverse else c
            return (layer * n_pool + pt[b, cc * pages_per_step + p], 0, 0)
        specs.append(pl.BlockSpec((1, ATT_KV_W, PAGE_SIZE), imap))
    return specs


def _cache_pages(cache):
    d, n_pool = cache.shape[:2]
    return cache.transpose(0, 1, 3, 4, 5, 2).reshape(d * n_pool, ATT_KV_W, PAGE_SIZE)


def _sb_sample(q, cache, page_table, layer):
    nseq, n_pages = page_table.shape
    n_pool = cache.shape[1]
    n_steps = n_pages // PAGES_PER_STEP
    cache2 = _cache_pages(cache)
    q3 = q.reshape(nseq, 1, ATT_Q_W)

    def walk(steps):
        grid_spec = pltpu.PrefetchScalarGridSpec(
            num_scalar_prefetch=1, grid=(nseq, steps),
            in_specs=[pl.BlockSpec((1, 1, ATT_Q_W), lambda b, c, pt: (b, 0, 0))]
            + _page_specs(layer, n_pool, n_steps, True, PAGES_PER_STEP),
            out_specs=[pl.BlockSpec((1, 1, ATT_Q_W), lambda b, c, pt: (b, 0, 0)),
                       pl.BlockSpec((1, 8, LANES), lambda b, c, pt: (b, 0, 0))],
            scratch_shapes=[pltpu.VMEM((8, LANES), F32), pltpu.VMEM((8, LANES), F32)])
        return pl.pallas_call(
            _sb_sample_kernel, grid_spec=grid_spec,
            out_shape=[jax.ShapeDtypeStruct((nseq, 1, ATT_Q_W), BF16),
                       jax.ShapeDtypeStruct((nseq, 8, LANES), F32)],
            compiler_params=pltpu.CompilerParams(dimension_semantics=("arbitrary", "arbitrary"),
                                                 vmem_limit_bytes=VMEM_LIMIT),
            name=f"sb_sample_{steps}")(page_table, q3, *([cache2] * PAGES_PER_STEP))

    out, carry = walk(1)
    out = lax.cond(jnp.max(carry) < SB_STOP, lambda: out, lambda: walk(n_steps)[0])
    return out.reshape(nseq, ATT_Q_W)


def _moba_sample_kernel(pt_ref, q_ref, kvn_ref, sb_ref, *refs, n_blocks):
    del pt_ref
    pages = refs[:MOBA_PAGES_PER_STEP]
    o_ref, gate_scr, m_scr, l_scr, acc_scr = refs[MOBA_PAGES_PER_STEP:]
    step = pl.program_id(1)
    bps = MOBA_PAGES_PER_STEP * PAGE_SIZE // MOBA_BLOCK
    rows = bps * 8

    q = _sample_q_rows(q_ref[0])
    kt, vt = _step_keys_values(pages)
    s = _tiles_to_rows(_dot(q, kt))
    g = jnp.mean(s, axis=1, keepdims=True)
    row_blk = lax.broadcasted_iota(jnp.int32, (rows, ATT_TILE), 0) // 8
    last = jnp.logical_and(step == pl.num_programs(1) - 1, row_blk == bps - 1)
    s = s + jnp.where(last, jnp.concatenate([sb_ref[:, 0:ATT_TILE]] * bps, axis=0), 0.0)
    m = jnp.max(s, axis=1, keepdims=True)
    p = jnp.exp(s - m)
    p_bd = jnp.concatenate([jnp.where(row_blk == c, p, 0.0) for c in range(bps)], axis=1).astype(BF16)
    dst = pl.ds(step * bps, bps)
    wide = lambda a: jnp.broadcast_to(a, (rows, LANES)).reshape(bps, 8, LANES)
    gate_scr[dst] = wide(g)
    m_scr[dst] = wide(m)
    l_scr[dst] = wide(jnp.sum(p, axis=1, keepdims=True))
    acc_scr[dst] = _dot_nt(p_bd, vt).reshape(bps, 8, LANES)

    @pl.when(step == pl.num_programs(1) - 1)
    def _():
        gates = [gate_scr[j] for j in range(n_blocks)]
        sel = [jnp.zeros((8, LANES), jnp.bool_)] * n_blocks
        for _ in range(MOBA_TOPK):
            best = functools.reduce(jnp.maximum, gates)
            idx = functools.reduce(jnp.minimum, [jnp.where(gates[j] == best, float(j), float(n_blocks))
                                                 for j in range(n_blocks)])
            for j in range(n_blocks):
                pick = idx == float(j)
                sel[j] = jnp.logical_or(sel[j], pick)
                gates[j] = jnp.where(pick, -jnp.inf, gates[j])
        kvn = kvn_ref[0]
        s_self = (jnp.sum(q.astype(F32) * kvn[:, 0:128].astype(F32), axis=1, keepdims=True)
                  + sb_ref[:, ATT_TILE:ATT_TILE + 1])
        m_tot = jnp.maximum(functools.reduce(jnp.maximum, [jnp.where(sel[j], m_scr[j], NEG_INF)
                                                           for j in range(n_blocks)]), s_self)
        p_self = jnp.exp(s_self - m_tot)
        denom = p_self
        o = p_self * kvn[:, 128:256].astype(F32)
        for j in range(n_blocks):
            coef = jnp.where(sel[j], jnp.exp(m_scr[j] - m_tot), 0.0)
            denom = denom + coef * l_scr[j]
            o = o + coef * acc_scr[j]
        cols = _sample_o_row(o / denom)
        for j in range(GROUP):
            o_ref[0, :, 128 * j:128 * (j + 1)] = cols[j].astype(o_ref.dtype)


def _moba_sample(q, kv_new, cache, page_table, bias_samp, layer):
    nseq, n_pages = page_table.shape
    n_pool = cache.shape[1]
    n_steps = n_pages // MOBA_PAGES_PER_STEP
    n_blocks = n_pages * PAGE_SIZE // MOBA_BLOCK
    cache2 = _cache_pages(cache)
    grid_spec = pltpu.PrefetchScalarGridSpec(
        num_scalar_prefetch=1, grid=(nseq, n_steps),
        in_specs=[pl.BlockSpec((1, 1, ATT_Q_W), lambda b, c, pt: (b, 0, 0)),
                  pl.BlockSpec((1, 1, ATT_KV_W), lambda b, c, pt: (b, 0, 0)),
                  pl.BlockSpec(bias_samp.shape, lambda b, c, pt: (0, 0))]
        + _page_specs(layer, n_pool, n_steps, False, MOBA_PAGES_PER_STEP),
        out_specs=pl.BlockSpec((1, 1, ATT_Q_W), lambda b, c, pt: (b, 0, 0)),
        scratch_shapes=[pltpu.VMEM((n_blocks, 8, LANES), F32)] * 4)
    out = pl.pallas_call(
        functools.partial(_moba_sample_kernel, n_blocks=n_blocks), grid_spec=grid_spec,
        out_shape=jax.ShapeDtypeStruct((nseq, 1, ATT_Q_W), BF16),
        compiler_params=pltpu.CompilerParams(dimension_semantics=("arbitrary", "arbitrary"),
                                             vmem_limit_bytes=VMEM_LIMIT),
        name="moba_sample")(page_table, q.reshape(nseq, 1, ATT_Q_W), kv_new.reshape(nseq, 1, ATT_KV_W),
                            bias_samp, *([cache2] * MOBA_PAGES_PER_STEP))
    return out.reshape(nseq, ATT_Q_W)


FF_CHUNK = 1408


def _ffn_kernel(x_ref, og_ref, om_ref, os_ref, wog_ref, wom_ref, wos_ref, g2_ref, wg_ref, wu_ref, wd_ref, y_ref,
                h2_scr):
    @pl.when(pl.program_id(1) == 0)
    def _():
        x1 = (x_ref[...] + _dot(og_ref[...], wog_ref[0]) + _dot(om_ref[...], wom_ref[0])
              + _dot(os_ref[...], wos_ref[0]))
        ms = jnp.mean(x1 * x1, axis=-1, keepdims=True)
        h2_scr[...] = (x1 * lax.rsqrt(ms + RMS_EPS) * g2_ref[0]).astype(BF16)
        y_ref[...] = x1

    h2 = h2_scr[...]
    a = _silu(_dot(h2, wg_ref[0])) * _dot(h2, wu_ref[0])
    y_ref[...] += _dot(a.astype(BF16), wd_ref[0])


def _ffn(x, og, om, osb, wts, layer, tm):
    n = x.shape[0]
    row = lambda w: pl.BlockSpec((tm, w), lambda i, f: (i, 0))
    ws = [wts[name] for name in ('wo_g', 'wo_m', 'wo_s', 'g2', 'w_gate', 'w_up', 'w_down')]
    w_specs = [_layer_spec(w, layer) for w in ws[:4]] + [
        pl.BlockSpec((1, D_MODEL, FF_CHUNK), lambda i, f: (layer, 0, f)),
        pl.BlockSpec((1, D_MODEL, FF_CHUNK), lambda i, f: (layer, 0, f)),
        pl.BlockSpec((1, FF_CHUNK, D_MODEL), lambda i, f: (layer, f, 0))]
    return pl.pallas_call(
        _ffn_kernel, grid=(n // tm, D_FF // FF_CHUNK),
        in_specs=[row(D_MODEL), row(GLA_V_W), row(ATT_Q_W), row(ATT_Q_W)] + w_specs,
        out_specs=row(D_MODEL), out_shape=jax.ShapeDtypeStruct((n, D_MODEL), F32),
        scratch_shapes=[pltpu.VMEM((tm, D_MODEL), BF16)],
        compiler_params=pltpu.CompilerParams(dimension_semantics=("arbitrary", "arbitrary"),
                                             vmem_limit_bytes=VMEM_LIMIT),
        name=f"ffn_{tm}")(x, og, om, osb, *ws)


def _prep_weights(norm1, w_in, w_alpha, b_alpha, gla_norm, moba_q_norm, moba_k_norm, sb_q_norm, sb_k_norm,
                  w_out, norm2, w_gate_up, w_down):
    o = _OFF
    cols = [w_in[:, :, o['gq']:o['ga']],
            jnp.pad(w_in[:, :, o['ga']:o['mq']], ((0, 0), (0, 0), (0, LANES - GLA_RANK))),
            w_in[:, :, o['mq']:o['mk']][:, :, _HEAD_PERM], w_in[:, :, o['mk']:o['sq']],
            w_in[:, :, o['sq']:o['sk']][:, :, _HEAD_PERM], w_in[:, :, o['sk']:]]
    row = lambda a: a[:, None, :]
    tile = lambda g, reps: jnp.tile(g, (1, reps))[:, None, :]
    return dict(
        g1=row(norm1), w_in=jnp.concatenate(cols, axis=2).astype(BF16),
        w_alpha=jnp.pad(w_alpha, ((0, 0), (0, LANES - GLA_RANK), (0, 0))).astype(BF16),
        b_alpha=row(b_alpha),
        gnorm=tile(gla_norm, GLA_HEADS), mqg=tile(moba_q_norm, ATT_HEADS), mkg=tile(moba_k_norm, KV_HEADS),
        sqg=tile(sb_q_norm, ATT_HEADS), skg=tile(sb_k_norm, KV_HEADS),
        wo_g=w_out[:, 0:256].astype(BF16), wo_m=w_out[:, 256:640][:, _HEAD_PERM].astype(BF16),
        wo_s=w_out[:, 640:1024][:, _HEAD_PERM].astype(BF16), g2=row(norm2),
        w_gate=w_gate_up[:, :, :D_FF].astype(BF16), w_up=w_gate_up[:, :, D_FF:].astype(BF16),
        w_down=w_down.astype(BF16))


def _state_to_blockdiag_T(s):
    b = s.shape[0]
    eye = jnp.eye(GLA_HEADS, dtype=s.dtype)
    return jnp.einsum('bhkv,hg->bhvgk', s, eye).reshape(b, GLA_V_W, GLA_QK_W)


def _blockdiag_T_to_state(st):
    b = st.shape[0]
    s5 = st.reshape(b, GLA_HEADS, GLA_DV, GLA_HEADS, GLA_DK)
    return jnp.stack([s5[:, h, :, h, :] for h in range(GLA_HEADS)], axis=1).transpose(0, 1, 3, 2)


def kernel(x_prompt, x_sample, cache_moba_kv, cache_sb_kv, state_gla, page_table, rel_bias, norm1, w_in, w_alpha,
           b_alpha, gla_norm, moba_q_norm, moba_k_norm, sb_q_norm, sb_k_norm, w_out, norm2, w_gate_up, w_down):
    nb, seq, _ = x_prompt.shape
    ns = x_sample.shape[0]
    n_prompt = nb * seq
    tm = 512
    assert seq % (2 * MOBA_BLOCK) == 0 and x_sample.shape[1] == 1 and seq // MOBA_BLOCK < LANES
    bias_own, bias_prev, bias_samp = _bias_tiles(rel_bias)
    xp = x_prompt.reshape(n_prompt, D_MODEL)
    xs = x_sample.reshape(ns, D_MODEL)
    zero_state = jnp.zeros((nb, GLA_V_W, GLA_QK_W), F32)
    outs = dict(pm=[], ps=[], pg=[], sm=[], ss=[], sg=[])
    wts = _prep_weights(norm1, w_in, w_alpha, b_alpha, gla_norm, moba_q_norm, moba_k_norm, sb_q_norm,
                        sb_k_norm, w_out, norm2, w_gate_up, w_down)
    for l in range(DEPTH):
        gq, gk, gl, gv, gg, mq, mkv32, mvt, mk, mkm, sq, skv32, skt, sv = _inproj(xp, wts, l, tm, seq=seq)
        og, st = _gla(gq, gk, gl, gv, gg, wts['gnorm'], l, zero_state, batch=nb, seq=seq, chunk=GLA_CHUNK,
                      chunks_per_step=8)
        osb = _sb_prompt(sq, skt, sv, batch=nb, seq=seq)
        kmean = jnp.pad(mkm.reshape(nb, seq // MOBA_BLOCK, LANES), ((0, 0), (0, LANES - seq // MOBA_BLOCK), (0, 0)))
        om = _moba_prompt(mq, mk, mvt, kmean, bias_own, bias_prev, batch=nb, seq=seq)
        xp = _ffn(xp, og, om, osb, wts, l, tm)
        leaf = lambda a: a.reshape(nb, 2, KV_HEADS, HEAD_DIM, seq).transpose(0, 4, 1, 2, 3)
        outs['pm'].append(leaf(mkv32))
        outs['ps'].append(leaf(skv32))
        outs['pg'].append(_blockdiag_T_to_state(st))
        gq, gk, gl, gv, gg, mq, mkv32, mkv16, sq, skv32, _ = _inproj(xs, wts, l, ns)
        og, st = _gla_sample(gq, gk, gl, gv, gg, wts['gnorm'], l, _state_to_blockdiag_T(state_gla[l]))
        osb = _sb_sample(sq, cache_sb_kv, page_table, l)
        om = _moba_sample(mq, mkv16, cache_moba_kv, page_table, bias_samp, l)
        xs = _ffn(xs, og, om, osb, wts, l, ns)
        outs['sm'].append(mkv32.reshape(ns, 1, 2, KV_HEADS, HEAD_DIM))
        outs['ss'].append(skv32.reshape(ns, 1, 2, KV_HEADS, HEAD_DIM))
        outs['sg'].append(_blockdiag_T_to_state(st))
    return (xp.reshape(nb, seq, D_MODEL), xs.reshape(ns, 1, D_MODEL), jnp.stack(outs['pm']), jnp.stack(outs['ps']),
            jnp.stack(outs['pg']), jnp.stack(outs['sm']), jnp.stack(outs['ss']), jnp.stack(outs['sg']))
```

```python
import functools
import math

import jax
import jax.numpy as jnp
import numpy as np
from jax import lax
from jax.experimental import pallas as pl
from jax.experimental.pallas import tpu as pltpu

F32 = jnp.float32
BF16 = jnp.bfloat16

D_MODEL = 1024
DEPTH = 4
HEAD_DIM = 64
GLA_HEADS = 4
GLA_DK = 32
GLA_DV = 64
GLA_RANK = 16
GLA_TAU = 16.0
GLA_CHUNK = 64
GLA_SUB = 16
ATT_HEADS = 6
KV_HEADS = 2
GROUP = ATT_HEADS // KV_HEADS
MOBA_BLOCK = 256
MOBA_TOPK = 3
REL_BUCKETS = 32
REL_MAX_DIST = 128
RMS_EPS = 1e-6
NEG_INF = -1e30
PAGE_SIZE = 128
D_FF = 2816
GLA_QK_W = GLA_HEADS * GLA_DK
GLA_V_W = GLA_HEADS * GLA_DV
ATT_Q_W = ATT_HEADS * HEAD_DIM
ATT_KV_W = 2 * KV_HEADS * HEAD_DIM
LANES = 128
ATT_TILE = 256
ATT_ROWS = ATT_HEADS * ATT_TILE
SB_ROW_TILE = 1536
MOBA_ROW_TILE = 128
VMEM_LIMIT = 56 * 1024 * 1024
PAGES_PER_STEP = 16
MOBA_PAGES_PER_STEP = 32

_OFF = dict(gq=0, gk=128, gv=256, gg=512, ga=768, mq=784, mk=1168, mv=1296, sq=1424, sk=1808, sv=1936)
IN_W_PAD = 2176
_HEAD_PERM = np.concatenate([np.concatenate([np.arange(64) + 64 * j, np.arange(64) + 64 * (GROUP + j)])
                             for j in range(GROUP)])


def _t5_thresholds():
    n = np.arange(0, 4 * REL_MAX_DIST, dtype=np.int64)
    max_exact = REL_BUCKETS // 2
    nf = np.maximum(n, 1).astype(np.float32)
    large = max_exact + (np.log(nf / np.float32(max_exact)) / np.float32(math.log(REL_MAX_DIST / max_exact))
                         * np.float32(REL_BUCKETS - max_exact)).astype(np.int32)
    bucket = np.where(n < max_exact, n, np.minimum(large, REL_BUCKETS - 1))
    return [int(np.argmax(bucket >= b)) for b in range(REL_BUCKETS)]


_T5_THR = _t5_thresholds()


def _dot(a, b):
    return jnp.dot(a, b, preferred_element_type=F32)


def _dot_nt(a, b):
    return lax.dot_general(a, b, (((1,), (1,)), ((), ())), preferred_element_type=F32)


def _split_hilo(a):
    hi = a.astype(BF16)
    lo = (a - hi.astype(F32)).astype(BF16)
    return hi, lo


def _dot_hilo(a, b_bf16):
    hi, lo = _split_hilo(a)
    return _dot(hi, b_bf16) + _dot(lo, b_bf16)


def _group_mean_matrix(width, group):
    r = lax.broadcasted_iota(jnp.int32, (width, width), 0) // group
    c = lax.broadcasted_iota(jnp.int32, (width, width), 1) // group
    return jnp.where(r == c, 1.0 / group, 0.0).astype(BF16)


def _neg_softplus(z):
    return -(jnp.maximum(z, 0.0) + jnp.log(1.0 + jnp.exp(-jnp.abs(z))))


def _log_sigmoid(x):
    return jnp.minimum(x, 0.0) - jnp.log(1.0 + jnp.exp(-jnp.abs(x)))


def _silu(x):
    return x / (1.0 + jnp.exp(-x))


def _head_rms(x, gain, group):
    ms = _dot_hilo(x * x, _group_mean_matrix(x.shape[1], group))
    return x * lax.rsqrt(ms + RMS_EPS) * gain


def _head_rms_lanes(x, gain):
    lane = lax.broadcasted_iota(jnp.int32, (x.shape[0], LANES), 1)
    low = lane < HEAD_DIM
    cols = []
    for c in range(x.shape[1] // LANES):
        xc = x[:, c * LANES:(c + 1) * LANES]
        x2 = xc * xc
        s_low = jnp.sum(jnp.where(low, x2, 0.0), axis=1, keepdims=True)
        s_high = jnp.sum(jnp.where(low, 0.0, x2), axis=1, keepdims=True)
        ms = jnp.where(low, s_low, s_high) * (1.0 / HEAD_DIM)
        cols.append(xc * lax.rsqrt(ms + RMS_EPS))
    return jnp.concatenate(cols, axis=1) * gain


def _inproj_kernel(x_ref, g1_ref, w_ref, wa_ref, ba_ref, mqg_ref, mkg_ref, sqg_ref, skg_ref,
                   gq_ref, gk_ref, gl_ref, gv_ref, gg_ref, *att_refs, prompt):
    x = x_ref[...]
    ms = jnp.mean(x * x, axis=-1, keepdims=True)
    h = (x * lax.rsqrt(ms + RMS_EPS) * g1_ref[0]).astype(BF16)
    p = _dot(h, w_ref[0])
    gq_ref[...] = p[:, 0:128] * (GLA_DK ** -0.5)
    gk_ref[...] = p[:, 128:256]
    gv_ref[...] = p[:, 256:512]
    gg_ref[...] = p[:, 512:768]
    alpha = _dot(p[:, 768:896].astype(BF16), wa_ref[0]) + ba_ref[0]
    gl_ref[...] = _log_sigmoid(alpha) / GLA_TAU

    def attn_group(base, qg_ref, kg_ref, refs, tiles_of_k=True):
        q = _head_rms_lanes(p[:, base:base + ATT_Q_W], qg_ref[0]) * (HEAD_DIM ** -0.5)
        refs[0][...] = q.astype(BF16)
        k = _head_rms_lanes(p[:, base + 384:base + 512], kg_ref[0])
        v = p[:, base + 512:base + 640]
        if prompt:
            _, kv32t_ref, tiles_ref, rows_ref = refs
            kt, vt = k.T, v.T
            kv32t_ref[0, 0:128, :] = kt
            kv32t_ref[0, 128:256, :] = vt
            tiled = (kt if tiles_of_k else vt).astype(BF16)
            for c in range(tiles_ref.shape[1]):
                tiles_ref[0, c] = tiled[:, c * ATT_TILE:(c + 1) * ATT_TILE]
            rows_ref[...] = (v if tiles_of_k else k).astype(BF16)
        else:
            _, kv32_ref, kv16_ref = refs
            kv32_ref[:, 0:128] = k
            kv32_ref[:, 128:256] = v
            kv16_ref[:, 0:128] = k.astype(BF16)
            kv16_ref[:, 128:256] = v.astype(BF16)
        return k

    if prompt:
        mk = attn_group(896, mqg_ref, mkg_ref, att_refs[0:4])
        attn_group(1536, sqg_ref, skg_ref, att_refs[5:9])
        mkm_ref = att_refs[4]
        for j in range(mkm_ref.shape[1]):
            mkm_ref[0, j:j + 1, :] = jnp.mean(mk[j * MOBA_BLOCK:(j + 1) * MOBA_BLOCK], axis=0, keepdims=True)
    else:
        attn_group(896, mqg_ref, mkg_ref, att_refs[0:3])
        attn_group(1536, sqg_ref, skg_ref, att_refs[3:6])


def _layer_spec(a, layer):
    return pl.BlockSpec((1,) + a.shape[1:], lambda *_: (layer, 0, 0))


def _inproj(x, wts, layer, tm, seq=None):
    n = x.shape[0]
    grid = n // tm
    row = lambda w: pl.BlockSpec((tm, w), lambda i: (i, 0))
    sds = jax.ShapeDtypeStruct
    outs = [sds((n, 128), F32), sds((n, 128), F32), sds((n, 128), F32), sds((n, 256), F32), sds((n, 256), F32)]
    out_specs = [row(128), row(128), row(128), row(256), row(256)]
    if seq is None:
        group = [(sds((n, ATT_Q_W), BF16), row(ATT_Q_W)), (sds((n, ATT_KV_W), F32), row(ATT_KV_W)),
                 (sds((n, ATT_KV_W), BF16), row(ATT_KV_W))]
        att = group + group
    else:
        tiles = seq // tm
        nblk = tm // ATT_TILE
        group = [(sds((n, ATT_Q_W), BF16), row(ATT_Q_W)),
                 (sds((n // seq, ATT_KV_W, seq), F32),
                  pl.BlockSpec((1, ATT_KV_W, tm), lambda i: (i // tiles, 0, i % tiles))),
                 (sds((n // seq, seq // ATT_TILE, 128, ATT_TILE), BF16),
                  pl.BlockSpec((1, nblk, 128, ATT_TILE), lambda i: (i // tiles, i % tiles, 0, 0))),
                 (sds((n, 128), BF16), row(128))]
        means = (sds((grid, nblk, 128), F32), pl.BlockSpec((1, nblk, 128), lambda i: (i, 0, 0)))
        att = group + [means] + group
    outs += [a for a, _ in att]
    out_specs += [b for _, b in att]
    ins = [x] + [wts[name] for name in ('g1', 'w_in', 'w_alpha', 'b_alpha', 'mqg', 'mkg', 'sqg', 'skg')]
    in_specs = [row(D_MODEL)] + [_layer_spec(a, layer) for a in ins[1:]]
    return pl.pallas_call(
        functools.partial(_inproj_kernel, prompt=seq is not None), grid=(grid,),
        in_specs=in_specs, out_specs=out_specs, out_shape=outs,
        compiler_params=pltpu.CompilerParams(dimension_semantics=("arbitrary",), vmem_limit_bytes=VMEM_LIMIT),
        name=f"inproj_{tm}")(*ins)


def _gla_kernel(q_ref, k_ref, gl_ref, v_ref, gg_ref, gn_ref, s0_ref, o_ref, sT_ref,
                st_scr, p_scr, w_scr, *, chunk, n_chunks):
    c = chunk
    step = pl.program_id(0)
    n_seq = q_ref.shape[0]

    @pl.when(step == 0)
    def _():
        st_scr[...] = s0_ref[...]

    ri = lax.broadcasted_iota(jnp.int32, (c, c), 0)
    ci = lax.broadcasted_iota(jnp.int32, (c, c), 1)
    ltri = jnp.where(ri >= ci, 1.0, 0.0).astype(BF16)
    kh = lax.broadcasted_iota(jnp.int32, (GLA_QK_W, GLA_V_W), 0) // GLA_DK
    vh = lax.broadcasted_iota(jnp.int32, (GLA_QK_W, GLA_V_W), 1) // GLA_DV
    head_ones = jnp.where(kh == vh, 1.0, 0.0).astype(BF16)
    vh2 = lax.broadcasted_iota(jnp.int32, (GLA_V_W, GLA_QK_W), 0) // GLA_DV
    kh2 = lax.broadcasted_iota(jnp.int32, (GLA_V_W, GLA_QK_W), 1) // GLA_DK
    bd_mask = vh2 == kh2
    sc = GLA_SUB
    n_sub = c // sc
    trow = lax.broadcasted_iota(jnp.int32, (sc, GLA_QK_W), 0)
    qk_head = lax.broadcasted_iota(jnp.int32, (sc, GLA_QK_W), 1) // GLA_DK
    v_head = lax.broadcasted_iota(jnp.int32, (sc, GLA_V_W), 1) // GLA_DV
    key_col = lax.broadcasted_iota(jnp.int32, (GLA_HEADS * sc, c), 1)

    def chunks(ic, carry):
        r0 = pl.multiple_of(ic * c, c)
        seqs = range(n_seq)
        ld = lambda ref: [ref[bi, pl.ds(r0, c), :] for bi in seqs]
        q, k, g, v = ld(q_ref), ld(k_ref), ld(gl_ref), ld(v_ref)
        b = []
        for bi in seqs:
            g_hi, g_lo = _split_hilo(g[bi])
            b.append(_dot(ltri, g_hi) + _dot(ltri, g_lo))
        st = [st_scr[bi] for bi in seqs]
        o_inter = [_dot_nt((q[bi] * jnp.exp(b[bi])).astype(BF16), st[bi].astype(BF16)) for bi in seqs]
        v16 = [v[bi].astype(BF16) for bi in seqs]

        s4 = {}
        for i_sub in range(1, n_sub):
            rows = slice(i_sub * sc, (i_sub + 1) * sc)
            for bi in seqs:
                e = b[bi][i_sub * sc - 1:i_sub * sc, :]
                a = q[bi][rows] * jnp.exp(b[bi][rows] - e)
                a4 = jnp.concatenate([jnp.where(qk_head == h, a, 0.0) for h in range(GLA_HEADS)], axis=0)
                kd = (k[bi] * jnp.exp(jnp.minimum(e - b[bi], 0.0))).astype(BF16)
                s4[bi, i_sub] = jnp.where(key_col < i_sub * sc, _dot_nt(a4.astype(BF16), kd), 0.0)

        for i_sub in range(n_sub):
            rows = slice(i_sub * sc, (i_sub + 1) * sc)
            for bi in seqs:
                q_i, b_i, k_i = q[bi][rows], b[bi][rows], k[bi][rows]
                for s in range(sc):
                    sl = q_i * jnp.exp(jnp.minimum(b_i - b_i[s:s + 1], 0.0)) * k_i[s:s + 1]
                    off = (i_sub * sc + s) * sc
                    p_scr[bi, off:off + sc, :] = jnp.where(trow >= s, sl, 0.0)
        for bi in seqs:
            w_scr[bi] = _dot_hilo(p_scr[bi], head_ones)

        for bi in seqs:
            b_last = b[bi][c - 1:c, :]
            kt = (k[bi] * jnp.exp(b_last - b[bi])).astype(BF16)
            upd = _dot(v[bi].T.astype(BF16), kt)
            st_scr[bi] = st[bi] * jnp.exp(b_last) + jnp.where(bd_mask, upd, 0.0)

        o4 = {key: _dot(val.astype(BF16), v16[key[0]]) for key, val in s4.items()}

        for i_sub in range(n_sub):
            rows = slice(i_sub * sc, (i_sub + 1) * sc)
            for bi in seqs:
                o_i = o_inter[bi][rows]
                for s in range(sc):
                    off = (i_sub * sc + s) * sc
                    o_i = o_i + w_scr[bi, off:off + sc, :] * v[bi][i_sub * sc + s:i_sub * sc + s + 1, :]
                if i_sub > 0:
                    for h in range(GLA_HEADS):
                        o_i = o_i + jnp.where(v_head == h, o4[bi, i_sub][h * sc:(h + 1) * sc], 0.0)
                on = _head_rms(o_i, gn_ref[0], GLA_DV)
                dst = pl.ds(r0 + i_sub * sc, sc)
                o_ref[bi, dst, :] = (on * _silu(gg_ref[bi, dst, :])).astype(o_ref.dtype)
        return carry
    lax.fori_loop(0, n_chunks, chunks, 0)

    @pl.when(step == pl.num_programs(0) - 1)
    def _():
        sT_ref[...] = st_scr[...]


def _gla(gq, gk, gl, gv, gg, gnorm, layer, s0T, *, batch, seq, chunk, chunks_per_step):
    rows = chunk * chunks_per_step
    row = lambda w: pl.BlockSpec((batch, rows, w), lambda i: (0, i, 0))
    state = pl.BlockSpec((batch, GLA_V_W, GLA_QK_W), lambda i: (0, 0, 0))
    r3 = lambda a: a.reshape(batch, seq, a.shape[-1])
    kern = functools.partial(_gla_kernel, chunk=chunk, n_chunks=chunks_per_step)
    o, st = pl.pallas_call(
        kern, grid=(seq // rows,),
        in_specs=[row(128), row(128), row(128), row(256), row(256),
                  _layer_spec(gnorm, layer), state],
        out_specs=[row(256), state],
        out_shape=[jax.ShapeDtypeStruct((batch, seq, GLA_V_W), BF16),
                   jax.ShapeDtypeStruct((batch, GLA_V_W, GLA_QK_W), F32)],
        scratch_shapes=[pltpu.VMEM((batch, GLA_V_W, GLA_QK_W), F32),
                        pltpu.VMEM((batch, chunk * GLA_SUB, GLA_QK_W), F32),
                        pltpu.VMEM((batch, chunk * GLA_SUB, GLA_V_W), F32)],
        compiler_params=pltpu.CompilerParams(dimension_semantics=("arbitrary",), vmem_limit_bytes=VMEM_LIMIT),
        name="gla_prompt")(r3(gq), r3(gk), r3(gl), r3(gv), r3(gg), gnorm, s0T)
    return o.reshape(batch * seq, GLA_V_W), st


def _gla_sample_kernel(q_ref, k_ref, gl_ref, v_ref, gg_ref, gn_ref, s0_ref, o_ref, sT_ref):
    rows = (8, GLA_QK_W)
    decay = jnp.exp(gl_ref[0])
    r = lax.broadcasted_iota(jnp.int32, (GLA_V_W, GLA_V_W), 0)
    cc = lax.broadcasted_iota(jnp.int32, (GLA_V_W, GLA_V_W), 1)
    v_diag = jnp.where(r == cc, jnp.broadcast_to(v_ref[0], (GLA_V_W, GLA_V_W)), 0.0).astype(BF16)
    k_rows = jnp.broadcast_to(k_ref[0], (GLA_V_W, GLA_QK_W)).astype(BF16)
    vh = lax.broadcasted_iota(jnp.int32, (GLA_V_W, GLA_QK_W), 0) // GLA_DV
    kh = lax.broadcasted_iota(jnp.int32, (GLA_V_W, GLA_QK_W), 1) // GLA_DK
    outer = jnp.where(vh == kh, _dot(v_diag, k_rows), 0.0)
    st = s0_ref[0] * decay + outer
    sT_ref[0] = st
    q8 = jnp.broadcast_to(q_ref[0], rows).astype(BF16)
    o = _dot_nt(q8, st.astype(BF16))
    on = _head_rms(o, gn_ref[0], GLA_DV)
    o_ref[0] = (on * _silu(gg_ref[0]))[0:1].astype(o_ref.dtype)


def _gla_sample(gq, gk, gl, gv, gg, gnorm, layer, s0T):
    n = gq.shape[0]
    r3 = lambda a: a.reshape(n, 1, a.shape[-1])
    row = lambda w: pl.BlockSpec((1, 1, w), lambda b: (b, 0, 0))
    st_spec = pl.BlockSpec((1, GLA_V_W, GLA_QK_W), lambda b: (b, 0, 0))
    o, st = pl.pallas_call(
        _gla_sample_kernel, grid=(n,),
        in_specs=[row(128), row(128), row(128), row(256), row(256),
                  _layer_spec(gnorm, layer), st_spec],
        out_specs=[row(256), st_spec],
        out_shape=[jax.ShapeDtypeStruct((n, 1, GLA_V_W), BF16), jax.ShapeDtypeStruct((n, GLA_V_W, GLA_QK_W), F32)],
        compiler_params=pltpu.CompilerParams(dimension_semantics=("arbitrary",), vmem_limit_bytes=VMEM_LIMIT),
        name="gla_sample")(r3(gq), r3(gk), r3(gl), r3(gv), r3(gg), gnorm, s0T)
    return o.reshape(n, GLA_V_W), st


def _stack_heads(q_cols):
    lane = lax.broadcasted_iota(jnp.int32, q_cols[0].shape, 1)
    low = lane < HEAD_DIM
    zero = jnp.zeros_like(q_cols[0])
    parts = [jnp.where(low, qc, zero) for qc in q_cols] + [jnp.where(low, zero, qc) for qc in q_cols]
    return jnp.concatenate(parts, axis=0)


def _unstack_heads(o, rows):
    lane = lax.broadcasted_iota(jnp.int32, (rows, LANES), 1)
    low = lane < HEAD_DIM
    return [jnp.where(low, o[j * rows:(j + 1) * rows], o[(GROUP + j) * rows:(GROUP + j + 1) * rows])
            for j in range(GROUP)]


def _suffix_matrix():
    r = lax.broadcasted_iota(jnp.int32, (ATT_TILE, ATT_TILE), 0)
    c = lax.broadcasted_iota(jnp.int32, (ATT_TILE, ATT_TILE), 1)
    u = jnp.where(r > c, 1.0, 0.0).astype(BF16)
    return jnp.concatenate([u, u], axis=0)


def _twice(a):
    return jnp.concatenate([a, a], axis=1)


def _sb_rows(q, kt, v, u2, carry, mask):
    z = _dot(q, kt)
    l = _neg_softplus(z)
    if mask is not None:
        l = jnp.where(mask, l, 0.0)
    hi, lo = _split_hilo(l)
    c = _dot(jnp.concatenate([hi, lo], axis=1), u2)
    w = jnp.exp(z + l + c + _twice(carry))
    if mask is not None:
        w = jnp.where(mask, w, 0.0)
    return _dot(w.astype(BF16), v), carry + jnp.sum(l, axis=1, keepdims=True)


SB_STOP = -104.0


def _sb_prompt_kernel(q_ref, kt_ref, v_ref, o_ref, q_scr, carry_scr, acc_scr):
    i = pl.program_id(1)
    t = ATT_TILE
    q_scr[...] = _stack_heads([q_ref[0, :, 128 * j:128 * (j + 1)] for j in range(GROUP)])
    carry_scr[...] = jnp.zeros(carry_scr.shape, F32)
    acc_scr[...] = jnp.zeros(acc_scr.shape, F32)
    u2 = _suffix_matrix()
    rt = SB_ROW_TILE
    rr = lax.broadcasted_iota(jnp.int32, (rt, t), 0) & (t - 1)
    cc = lax.broadcasted_iota(jnp.int32, (rt, t), 1)

    def tile(kj, diagonal):
        kt = kt_ref[0, kj]
        v = v_ref[0, pl.ds(pl.multiple_of(kj * t, t), t), :]
        for r in range(ATT_ROWS // rt):
            rows = slice(r * rt, (r + 1) * rt)
            mask = cc < rr + (r * rt) % t if diagonal else None
            pv, carry = _sb_rows(q_scr[rows, :], kt, v, u2, carry_scr[rows, :], mask)
            acc_scr[rows, :] += pv
            carry_scr[rows, :] = carry

    def carry_max():
        return jnp.max(jnp.max(carry_scr[...], axis=0, keepdims=True), axis=1, keepdims=True)[0, 0]

    tile(i, True)

    def cond(state):
        kj, cmax = state
        return jnp.logical_and(kj >= 0, cmax > SB_STOP)

    def body(state):
        kj, _ = state
        tile(kj, False)
        return kj - 1, carry_max()
    lax.while_loop(cond, body, (i - 1, carry_max()))

    cols = _unstack_heads(acc_scr[...], t)
    for j in range(GROUP):
        o_ref[0, :, 128 * j:128 * (j + 1)] = cols[j].astype(o_ref.dtype)


def _sb_prompt(q, kt, v, *, batch, seq):
    t = ATT_TILE
    out = pl.pallas_call(
        _sb_prompt_kernel, grid=(batch, seq // t),
        in_specs=[pl.BlockSpec((1, t, ATT_Q_W), lambda b, i: (b, i, 0)),
                  pl.BlockSpec((1, seq // t, LANES, t), lambda b, i: (b, 0, 0, 0)),
                  pl.BlockSpec((1, seq, LANES), lambda b, i: (b, 0, 0))],
        out_specs=pl.BlockSpec((1, t, ATT_Q_W), lambda b, i: (b, i, 0)),
        out_shape=jax.ShapeDtypeStruct((batch, seq, ATT_Q_W), BF16),
        scratch_shapes=[pltpu.VMEM((ATT_ROWS, LANES), BF16), pltpu.VMEM((ATT_ROWS, LANES), F32),
                        pltpu.VMEM((ATT_ROWS, LANES), F32)],
        compiler_params=pltpu.CompilerParams(dimension_semantics=("arbitrary", "arbitrary"),
                                             vmem_limit_bytes=VMEM_LIMIT),
        name="sb_prompt")(q.reshape(batch, seq, ATT_Q_W), kt, v.reshape(batch, seq, LANES))
    return out.reshape(batch * seq, ATT_Q_W)


def _bias_of_dist(dist, rb_ref, h):
    bias = jnp.full(dist.shape, rb_ref[0, h], F32)
    for b in range(1, REL_BUCKETS):
        bias = jnp.where(dist >= _T5_THR[b], rb_ref[b, h], bias)
    return bias - rb_ref[REL_BUCKETS - 1, h]


def _bias_kernel(rb_ref, own_ref, prev_ref, samp_ref):
    t = ATT_TILE
    r = lax.broadcasted_iota(jnp.int32, (t, t), 0)
    c = lax.broadcasted_iota(jnp.int32, (t, t), 1)
    for h in range(ATT_HEADS):
        own = _bias_of_dist(jnp.maximum(r - c, 0), rb_ref, h)
        own_ref[h * t:(h + 1) * t, :] = jnp.where(c <= r, own, NEG_INF)
        prev_ref[h * t:(h + 1) * t, :] = _bias_of_dist(r - c + t, rb_ref, h)
    s = lax.broadcasted_iota(jnp.int32, (8, t), 1)
    hrow = lax.broadcasted_iota(jnp.int32, (8, t), 0)
    last = jnp.zeros((8, t), F32)
    self_b = jnp.zeros((8, LANES), F32)
    hrow2 = lax.broadcasted_iota(jnp.int32, (8, LANES), 0)
    for h in range(ATT_HEADS):
        last = jnp.where(hrow == h, _bias_of_dist(t - s, rb_ref, h), last)
        self_b = jnp.where(hrow2 == h, rb_ref[0, h] - rb_ref[REL_BUCKETS - 1, h], self_b)
    samp_ref[:, 0:t] = last
    samp_ref[:, t:t + LANES] = self_b


def _bias_tiles(rel_bias):
    t = ATT_TILE
    return pl.pallas_call(
        _bias_kernel,
        in_specs=[pl.BlockSpec(memory_space=pltpu.SMEM)],
        out_shape=[jax.ShapeDtypeStruct((ATT_ROWS, t), F32), jax.ShapeDtypeStruct((ATT_ROWS, t), F32),
                   jax.ShapeDtypeStruct((8, t + LANES), F32)],
        name="t5_bias_tiles")(rel_bias)


def _top3_select(gate, n_valid):
    lane_i = lax.broadcasted_iota(jnp.int32, gate.shape, 1)
    lane = lane_i.astype(F32)
    valid = lane_i < n_valid
    g = jnp.where(valid, gate, NEG_INF)
    sel = jnp.zeros(gate.shape, F32)
    for _ in range(MOBA_TOPK):
        m = jnp.max(g, axis=1, keepdims=True)
        idx = jnp.min(jnp.where(g == m, lane, float(LANES)), axis=1, keepdims=True)
        pick = lane == idx
        sel = jnp.where(pick, 1.0, sel)
        g = jnp.where(pick, -jnp.inf, g)
    return jnp.where(valid, sel, 0.0) > 0.5


def _moba_prompt_kernel(q_ref, kt_ref, v_ref, km_ref, own_ref, prev_ref, o_ref,
                        qx_scr, p_scr, al_scr, m_scr, acc_scr):
    i = pl.program_id(1)
    t = ATT_TILE
    q = _stack_heads([q_ref[0, :, 128 * j:128 * (j + 1)] for j in range(GROUP)])
    gate = _dot_nt(q, km_ref[0].astype(BF16))
    sel = _top3_select(gate, i)
    qx_scr[:, 0:LANES] = q
    qx_scr[:, LANES:2 * LANES] = jnp.where(sel, 0.0, NEG_INF).astype(BF16)
    blk_row = lax.broadcasted_iota(jnp.int32, (LANES, t), 0)
    ones = jnp.ones((t, LANES), BF16)
    rt = MOBA_ROW_TILE
    row_tiles = [slice(r * rt, (r + 1) * rt) for r in range(ATT_ROWS // rt)]

    def past_keys(kj):
        blk = jnp.where(kj >= 0, kj, LANES - 1)
        return jnp.concatenate([kt_ref[0, jnp.maximum(kj, 0)],
                                jnp.where(blk_row == blk, 1.0, 0.0).astype(BF16)], axis=0)

    def values(kj):
        v = v_ref[0, pl.ds(pl.multiple_of(jnp.maximum(kj, 0) * t, t), t), :]
        return jnp.concatenate([v, ones], axis=1)

    vx = values(i)
    for rows in row_tiles:
        s = _dot(qx_scr[rows, 0:LANES], kt_ref[0, i]) + own_ref[rows, :]
        m0 = jnp.broadcast_to(jnp.max(s, axis=1, keepdims=True), (rt, LANES))
        acc_scr[rows, :] = _dot(jnp.exp(s - _twice(m0)).astype(BF16), vx)
        m_scr[rows, :] = m0

    def pair_keys(kj):
        return jnp.concatenate([past_keys(kj), past_keys(kj - 1)], axis=1)

    def pair_values(kj):
        return jnp.concatenate([values(kj), values(kj - 1)], axis=0)

    def probabilities(kx, rows, with_prev_bias):
        s = _dot(qx_scr[rows, :], kx)
        if with_prev_bias:
            s = jnp.concatenate([s[:, 0:t] + prev_ref[rows, :], s[:, t:2 * t]], axis=1)
        m_old = m_scr[rows, :]
        m_new = jnp.maximum(m_old, jnp.broadcast_to(jnp.max(s, axis=1, keepdims=True), (rt, LANES)))
        p_scr[rows, :] = jnp.exp(s - _twice(_twice(m_new))).astype(BF16)
        al_scr[rows, :] = jnp.exp(m_old - m_new)
        m_scr[rows, :] = m_new

    def accumulate(vx, rows):
        acc_scr[rows, :] = _twice(al_scr[rows, :]) * acc_scr[rows, :] + _dot(p_scr[rows, :], vx)

    @pl.when(i >= 1)
    def _():
        kx = pair_keys(i - 1)
        for rows in row_tiles:
            probabilities(kx, rows, True)

    n_pairs = jnp.maximum(i - 1, 0) // 2

    def body(n, carry):
        kj = i - 3 - 2 * n
        vx, kx = pair_values(kj + 2), pair_keys(kj)
        for rows in row_tiles:
            accumulate(vx, rows)
            probabilities(kx, rows, False)
        return carry
    lax.fori_loop(0, n_pairs, body, 0)

    @pl.when(i >= 1)
    def _():
        vx = pair_values(i - 1 - 2 * n_pairs)
        for rows in row_tiles:
            accumulate(vx, rows)

    o = acc_scr[:, 0:LANES] / acc_scr[:, LANES:2 * LANES]
    cols = _unstack_heads(o, t)
    for j in range(GROUP):
        o_ref[0, :, 128 * j:128 * (j + 1)] = cols[j].astype(o_ref.dtype)


def _moba_prompt(q, kt, v, kmean, bias_own, bias_prev, *, batch, seq):
    t = ATT_TILE
    const = lambda a: pl.BlockSpec(a.shape, lambda b, i: (0, 0))
    out = pl.pallas_call(
        _moba_prompt_kernel, grid=(batch, seq // t),
        in_specs=[pl.BlockSpec((1, t, ATT_Q_W), lambda b, i: (b, i, 0)),
                  pl.BlockSpec((1, seq // t, LANES, t), lambda b, i: (b, 0, 0, 0)),
                  pl.BlockSpec((1, seq, LANES), lambda b, i: (b, 0, 0)),
                  pl.BlockSpec((1, LANES, LANES), lambda b, i: (b, 0, 0)),
                  const(bias_own), const(bias_prev)],
        out_specs=pl.BlockSpec((1, t, ATT_Q_W), lambda b, i: (b, i, 0)),
        out_shape=jax.ShapeDtypeStruct((batch, seq, ATT_Q_W), BF16),
        scratch_shapes=[pltpu.VMEM((ATT_ROWS, 2 * LANES), BF16), pltpu.VMEM((ATT_ROWS, 2 * t), BF16),
                        pltpu.VMEM((ATT_ROWS, LANES), F32), pltpu.VMEM((ATT_ROWS, LANES), F32),
                        pltpu.VMEM((ATT_ROWS, 2 * LANES), F32)],
        compiler_params=pltpu.CompilerParams(dimension_semantics=("arbitrary", "arbitrary"),
                                             vmem_limit_bytes=VMEM_LIMIT),
        name="moba_prompt")(q.reshape(batch, seq, ATT_Q_W), kt, v.reshape(batch, seq, LANES), kmean,
                            bias_own, bias_prev)
    return out.reshape(batch * seq, ATT_Q_W)


def _sample_q_rows(q_row):
    row = lax.broadcasted_iota(jnp.int32, (8, LANES), 0)
    lane = lax.broadcasted_iota(jnp.int32, (8, LANES), 1)
    qf = q_row.astype(F32)
    out = jnp.zeros((8, LANES), F32)
    for h in range(ATT_HEADS):
        j, n = h % GROUP, h // GROUP
        col = jnp.broadcast_to(qf[:, 128 * j:128 * (j + 1)], (8, LANES))
        half = lane >= HEAD_DIM if n == 1 else lane < HEAD_DIM
        out = jnp.where(jnp.logical_and(row == h, half), col, out)
    return out.astype(BF16)


def _sample_o_row(o):
    lane = lax.broadcasted_iota(jnp.int32, (1, LANES), 1)
    return [jnp.where(lane < HEAD_DIM, o[j:j + 1], o[GROUP + j:GROUP + j + 1]) for j in range(GROUP)]


TILES_PER_STEP = PAGES_PER_STEP * PAGE_SIZE // ATT_TILE


def _step_keys_values(pages):
    kt = jnp.concatenate([p[0, 0:128, :] for p in pages], axis=1).astype(BF16)
    vt = jnp.concatenate([p[0, 128:256, :] for p in pages], axis=1).astype(BF16)
    return kt, vt


def _tiles_to_rows(a):
    return jnp.concatenate([a[:, c * ATT_TILE:(c + 1) * ATT_TILE] for c in range(a.shape[1] // ATT_TILE)], axis=0)


def _rows_to_tiles(a):
    return jnp.concatenate([a[8 * c:8 * (c + 1), :] for c in range(a.shape[0] // 8)], axis=1)


def _sb_sample_kernel(pt_ref, q_ref, *refs):
    del pt_ref
    pages = refs[:PAGES_PER_STEP]
    o_ref, carry_ref, carry_scr, acc_scr = refs[PAGES_PER_STEP:]
    step = pl.program_id(1)

    @pl.when(step == 0)
    def _():
        carry_scr[...] = jnp.zeros(carry_scr.shape, F32)
        acc_scr[...] = jnp.zeros(acc_scr.shape, F32)

    q = _sample_q_rows(q_ref[0])
    kt, vt = _step_keys_values(pages)
    z = _tiles_to_rows(_dot(q, kt))
    l = _neg_softplus(z)
    hi, lo = _split_hilo(l)
    c = _dot(jnp.concatenate([hi, lo], axis=1), _suffix_matrix())
    tile_sum = jnp.sum(l, axis=1, keepdims=True)
    carry = carry_scr[...]
    carries = [None] * TILES_PER_STEP
    for tl in reversed(range(TILES_PER_STEP)):
        carries[tl] = carry
        carry = carry + tile_sum[8 * tl:8 * (tl + 1)]
    w = jnp.exp(z + l + c + _twice(jnp.concatenate(carries, axis=0)))
    acc = acc_scr[...] + _dot_nt(_rows_to_tiles(w).astype(BF16), vt)
    carry_scr[...] = carry
    acc_scr[...] = acc

    @pl.when(step == pl.num_programs(1) - 1)
    def _():
        cols = _sample_o_row(acc)
        for j in range(GROUP):
            o_ref[0, :, 128 * j:128 * (j + 1)] = cols[j].astype(o_ref.dtype)
        carry_ref[0] = carry


def _page_specs(layer, n_pool, n_steps, reverse, pages_per_step):
    specs = []
    for p in range(pages_per_step):
        def imap(b, c, pt, p=p):
            cc = (n_steps - 1 - c) if reverse else c
            return (layer * n_pool + pt[b, cc * pages_per_step + p], 0, 0)
        specs.append(pl.BlockSpec((1, ATT_KV_W, PAGE_SIZE), imap))
    return specs


def _cache_pages(cache):
    d, n_pool = cache.shape[:2]
    return cache.transpose(0, 1, 3, 4, 5, 2).reshape(d * n_pool, ATT_KV_W, PAGE_SIZE)


def _sb_sample(q, cache, page_table, layer):
    nseq, n_pages = page_table.shape
    n_pool = cache.shape[1]
    n_steps = n_pages // PAGES_PER_STEP
    cache2 = _cache_pages(cache)
    q3 = q.reshape(nseq, 1, ATT_Q_W)

    def walk(steps):
        grid_spec = pltpu.PrefetchScalarGridSpec(
            num_scalar_prefetch=1, grid=(nseq, steps),
            in_specs=[pl.BlockSpec((1, 1, ATT_Q_W), lambda b, c, pt: (b, 0, 0))]
            + _page_specs(layer, n_pool, n_steps, True, PAGES_PER_STEP),
            out_specs=[pl.BlockSpec((1, 1, ATT_Q_W), lambda b, c, pt: (b, 0, 0)),
                       pl.BlockSpec((1, 8, LANES), lambda b, c, pt: (b, 0, 0))],
            scratch_shapes=[pltpu.VMEM((8, LANES), F32), pltpu.VMEM((8, LANES), F32)])
        return pl.pallas_call(
            _sb_sample_kernel, grid_spec=grid_spec,
            out_shape=[jax.ShapeDtypeStruct((nseq, 1, ATT_Q_W), BF16),
                       jax.ShapeDtypeStruct((nseq, 8, LANES), F32)],
            compiler_params=pltpu.CompilerParams(dimension_semantics=("arbitrary", "arbitrary"),
                                                 vmem_limit_bytes=VMEM_LIMIT),
            name=f"sb_sample_{steps}")(page_table, q3, *([cache2] * PAGES_PER_STEP))

    out, carry = walk(1)
    out = lax.cond(jnp.max(carry) < SB_STOP, lambda: out, lambda: walk(n_steps)[0])
    return out.reshape(nseq, ATT_Q_W)


def _moba_sample_kernel(pt_ref, q_ref, kvn_ref, sb_ref, *refs, n_blocks):
    del pt_ref
    pages = refs[:MOBA_PAGES_PER_STEP]
    o_ref, gate_scr, m_scr, l_scr, acc_scr = refs[MOBA_PAGES_PER_STEP:]
    step = pl.program_id(1)
    bps = MOBA_PAGES_PER_STEP * PAGE_SIZE // MOBA_BLOCK
    rows = bps * 8

    q = _sample_q_rows(q_ref[0])
    kt, vt = _step_keys_values(pages)
    s = _tiles_to_rows(_dot(q, kt))
    g = jnp.mean(s, axis=1, keepdims=True)
    row_blk = lax.broadcasted_iota(jnp.int32, (rows, ATT_TILE), 0) // 8
    last = jnp.logical_and(step == pl.num_programs(1) - 1, row_blk == bps - 1)
    s = s + jnp.where(last, jnp.concatenate([sb_ref[:, 0:ATT_TILE]] * bps, axis=0), 0.0)
    m = jnp.max(s, axis=1, keepdims=True)
    p = jnp.exp(s - m)
    p_bd = jnp.concatenate([jnp.where(row_blk == c, p, 0.0) for c in range(bps)], axis=1).astype(BF16)
    dst = pl.ds(step * bps, bps)
    wide = lambda a: jnp.broadcast_to(a, (rows, LANES)).reshape(bps, 8, LANES)
    gate_scr[dst] = wide(g)
    m_scr[dst] = wide(m)
    l_scr[dst] = wide(jnp.sum(p, axis=1, keepdims=True))
    acc_scr[dst] = _dot_nt(p_bd, vt).reshape(bps, 8, LANES)

    @pl.when(step == pl.num_programs(1) - 1)
    def _():
        gates = [gate_scr[j] for j in range(n_blocks)]
        sel = [jnp.zeros((8, LANES), jnp.bool_)] * n_blocks
        for _ in range(MOBA_TOPK):
            best = functools.reduce(jnp.maximum, gates)
            idx = functools.reduce(jnp.minimum, [jnp.where(gates[j] == best, float(j), float(n_blocks))
                                                 for j in range(n_blocks)])
            for j in range(n_blocks):
                pick = idx == float(j)
                sel[j] = jnp.logical_or(sel[j], pick)
                gates[j] = jnp.where(pick, -jnp.inf, gates[j])
        kvn = kvn_ref[0]
        s_self = (jnp.sum(q.astype(F32) * kvn[:, 0:128].astype(F32), axis=1, keepdims=True)
                  + sb_ref[:, ATT_TILE:ATT_TILE + 1])
        m_tot = jnp.maximum(functools.reduce(jnp.maximum, [jnp.where(sel[j], m_scr[j], NEG_INF)
                                                           for j in range(n_blocks)]), s_self)
        p_self = jnp.exp(s_self - m_tot)
        denom = p_self
        o = p_self * kvn[:, 128:256].astype(F32)
        for j in range(n_blocks):
            coef = jnp.where(sel[j], jnp.exp(m_scr[j] - m_tot), 0.0)
            denom = denom + coef * l_scr[j]
            o = o + coef * acc_scr[j]
        cols = _sample_o_row(o / denom)
        for j in range(GROUP):
            o_ref[0, :, 128 * j:128 * (j + 1)] = cols[j].astype(o_ref.dtype)


def _moba_sample(q, kv_new, cache, page_table, bias_samp, layer):
    nseq, n_pages = page_table.shape
    n_pool = cache.shape[1]
    n_steps = n_pages // MOBA_PAGES_PER_STEP
    n_blocks = n_pages * PAGE_SIZE // MOBA_BLOCK
    cache2 = _cache_pages(cache)
    grid_spec = pltpu.PrefetchScalarGridSpec(
        num_scalar_prefetch=1, grid=(nseq, n_steps),
        in_specs=[pl.BlockSpec((1, 1, ATT_Q_W), lambda b, c, pt: (b, 0, 0)),
                  pl.BlockSpec((1, 1, ATT_KV_W), lambda b, c, pt: (b, 0, 0)),
                  pl.BlockSpec(bias_samp.shape, lambda b, c, pt: (0, 0))]
        + _page_specs(layer, n_pool, n_steps, False, MOBA_PAGES_PER_STEP),
        out_specs=pl.BlockSpec((1, 1, ATT_Q_W), lambda b, c, pt: (b, 0, 0)),
        scratch_shapes=[pltpu.VMEM((n_blocks, 8, LANES), F32)] * 4)
    out = pl.pallas_call(
        functools.partial(_moba_sample_kernel, n_blocks=n_blocks), grid_spec=grid_spec,
        out_shape=jax.ShapeDtypeStruct((nseq, 1, ATT_Q_W), BF16),
        compiler_params=pltpu.CompilerParams(dimension_semantics=("arbitrary", "arbitrary"),
                                             vmem_limit_bytes=VMEM_LIMIT),
        name="moba_sample")(page_table, q.reshape(nseq, 1, ATT_Q_W), kv_new.reshape(nseq, 1, ATT_KV_W),
                            bias_samp, *([cache2] * MOBA_PAGES_PER_STEP))
    return out.reshape(nseq, ATT_Q_W)


FF_CHUNK = 1408


def _ffn_kernel(x_ref, og_ref, om_ref, os_ref, wog_ref, wom_ref, wos_ref, g2_ref, wg_ref, wu_ref, wd_ref, y_ref,
                h2_scr):
    @pl.when(pl.program_id(1) == 0)
    def _():
        x1 = (x_ref[...] + _dot(og_ref[...], wog_ref[0]) + _dot(om_ref[...], wom_ref[0])
              + _dot(os_ref[...], wos_ref[0]))
        ms = jnp.mean(x1 * x1, axis=-1, keepdims=True)
        h2_scr[...] = (x1 * lax.rsqrt(ms + RMS_EPS) * g2_ref[0]).astype(BF16)
        y_ref[...] = x1

    h2 = h2_scr[...]
    a = _silu(_dot(h2, wg_ref[0])) * _dot(h2, wu_ref[0])
    y_ref[...] += _dot(a.astype(BF16), wd_ref[0])


def _ffn(x, og, om, osb, wts, layer, tm):
    n = x.shape[0]
    row = lambda w: pl.BlockSpec((tm, w), lambda i, f: (i, 0))
    ws = [wts[name] for name in ('wo_g', 'wo_m', 'wo_s', 'g2', 'w_gate', 'w_up', 'w_down')]
    w_specs = [_layer_spec(w, layer) for w in ws[:4]] + [
        pl.BlockSpec((1, D_MODEL, FF_CHUNK), lambda i, f: (layer, 0, f)),
        pl.BlockSpec((1, D_MODEL, FF_CHUNK), lambda i, f: (layer, 0, f)),
        pl.BlockSpec((1, FF_CHUNK, D_MODEL), lambda i, f: (layer, f, 0))]
    return pl.pallas_call(
        _ffn_kernel, grid=(n // tm, D_FF // FF_CHUNK),
        in_specs=[row(D_MODEL), row(GLA_V_W), row(ATT_Q_W), row(ATT_Q_W)] + w_specs,
        out_specs=row(D_MODEL), out_shape=jax.ShapeDtypeStruct((n, D_MODEL), F32),
        scratch_shapes=[pltpu.VMEM((tm, D_MODEL), BF16)],
        compiler_params=pltpu.CompilerParams(dimension_semantics=("arbitrary", "arbitrary"),
                                             vmem_limit_bytes=VMEM_LIMIT),
        name=f"ffn_{tm}")(x, og, om, osb, *ws)


def _prep_weights(norm1, w_in, w_alpha, b_alpha, gla_norm, moba_q_norm, moba_k_norm, sb_q_norm, sb_k_norm,
                  w_out, norm2, w_gate_up, w_down):
    o = _OFF
    cols = [w_in[:, :, o['gq']:o['ga']],
            jnp.pad(w_in[:, :, o['ga']:o['mq']], ((0, 0), (0, 0), (0, LANES - GLA_RANK))),
            w_in[:, :, o['mq']:o['mk']][:, :, _HEAD_PERM], w_in[:, :, o['mk']:o['sq']],
            w_in[:, :, o['sq']:o['sk']][:, :, _HEAD_PERM], w_in[:, :, o['sk']:]]
    row = lambda a: a[:, None, :]
    tile = lambda g, reps: jnp.tile(g, (1, reps))[:, None, :]
    return dict(
        g1=row(norm1), w_in=jnp.concatenate(cols, axis=2).astype(BF16),
        w_alpha=jnp.pad(w_alpha, ((0, 0), (0, LANES - GLA_RANK), (0, 0))).astype(BF16),
        b_alpha=row(b_alpha),
        gnorm=tile(gla_norm, GLA_HEADS), mqg=tile(moba_q_norm, ATT_HEADS), mkg=tile(moba_k_norm, KV_HEADS),
        sqg=tile(sb_q_norm, ATT_HEADS), skg=tile(sb_k_norm, KV_HEADS),
        wo_g=w_out[:, 0:256].astype(BF16), wo_m=w_out[:, 256:640][:, _HEAD_PERM].astype(BF16),
        wo_s=w_out[:, 640:1024][:, _HEAD_PERM].astype(BF16), g2=row(norm2),
        w_gate=w_gate_up[:, :, :D_FF].astype(BF16), w_up=w_gate_up[:, :, D_FF:].astype(BF16),
        w_down=w_down.astype(BF16))


def _state_to_blockdiag_T(s):
    b = s.shape[0]
    eye = jnp.eye(GLA_HEADS, dtype=s.dtype)
    return jnp.einsum('bhkv,hg->bhvgk', s, eye).reshape(b, GLA_V_W, GLA_QK_W)


def _blockdiag_T_to_state(st):
    b = st.shape[0]
    s5 = st.reshape(b, GLA_HEADS, GLA_DV, GLA_HEADS, GLA_DK)
    return jnp.stack([s5[:, h, :, h, :] for h in range(GLA_HEADS)], axis=1).transpose(0, 1, 3, 2)


def kernel(x_prompt, x_sample, cache_moba_kv, cache_sb_kv, state_gla, page_table, rel_bias, norm1, w_in, w_alpha,
           b_alpha, gla_norm, moba_q_norm, moba_k_norm, sb_q_norm, sb_k_norm, w_out, norm2, w_gate_up, w_down):
    nb, seq, _ = x_prompt.shape
    ns = x_sample.shape[0]
    n_prompt = nb * seq
    tm = 512
    assert seq % (2 * MOBA_BLOCK) == 0 and x_sample.shape[1] == 1 and seq // MOBA_BLOCK < LANES
    bias_own, bias_prev, bias_samp = _bias_tiles(rel_bias)
    xp = x_prompt.reshape(n_prompt, D_MODEL)
    xs = x_sample.reshape(ns, D_MODEL)
    zero_state = jnp.zeros((nb, GLA_V_W, GLA_QK_W), F32)
    outs = dict(pm=[], ps=[], pg=[], sm=[], ss=[], sg=[])
    wts = _prep_weights(norm1, w_in, w_alpha, b_alpha, gla_norm, moba_q_norm, moba_k_norm, sb_q_norm,
                        sb_k_norm, w_out, norm2, w_gate_up, w_down)
    for l in range(DEPTH):
        gq, gk, gl, gv, gg, mq, mkv32, mkt, mv, mkm, sq, skv32, skt, sv = _inproj(xp, wts, l, tm, seq=seq)
        og, st = _gla(gq, gk, gl, gv, gg, wts['gnorm'], l, zero_state, batch=nb, seq=seq, chunk=GLA_CHUNK,
                      chunks_per_step=8)
        osb = _sb_prompt(sq, skt, sv, batch=nb, seq=seq)
        kmean = jnp.pad(mkm.reshape(nb, seq // MOBA_BLOCK, LANES), ((0, 0), (0, LANES - seq // MOBA_BLOCK), (0, 0)))
        om = _moba_prompt(mq, mkt, mv, kmean, bias_own, bias_prev, batch=nb, seq=seq)
        xp = _ffn(xp, og, om, osb, wts, l, tm)
        leaf = lambda a: a.reshape(nb, 2, KV_HEADS, HEAD_DIM, seq).transpose(0, 4, 1, 2, 3)
        outs['pm'].append(leaf(mkv32))
        outs['ps'].append(leaf(skv32))
        outs['pg'].append(_blockdiag_T_to_state(st))
        gq, gk, gl, gv, gg, mq, mkv32, mkv16, sq, skv32, _ = _inproj(xs, wts, l, ns)
        og, st = _gla_sample(gq, gk, gl, gv, gg, wts['gnorm'], l, _state_to_blockdiag_T(state_gla[l]))
        osb = _sb_sample(sq, cache_sb_kv, page_table, l)
        om = _moba_sample(mq, mkv16, cache_moba_kv, page_table, bias_samp, l)
        xs = _ffn(xs, og, om, osb, wts, l, ns)
        outs['sm'].append(mkv32.reshape(ns, 1, 2, KV_HEADS, HEAD_DIM))
        outs['ss'].append(skv32.reshape(ns, 1, 2, KV_HEADS, HEAD_DIM))
        outs['sg'].append(_blockdiag_T_to_state(st))
    return (xp.reshape(nb, seq, D_MODEL), xs.reshape(ns, 1, D_MODEL), jnp.stack(outs['pm']), jnp.stack(outs['ps']),
            jnp.stack(outs['pg']), jnp.stack(outs['sm']), jnp.stack(outs['ss']), jnp.stack(outs['sg']))
```

```python
import functools
import math

import jax
import jax.numpy as jnp
import numpy as np
from jax import lax
from jax.experimental import pallas as pl
from jax.experimental.pallas import tpu as pltpu

F32 = jnp.float32
BF16 = jnp.bfloat16

D_MODEL = 1024
DEPTH = 4
HEAD_DIM = 64
GLA_HEADS = 4
GLA_DK = 32
GLA_DV = 64
GLA_RANK = 16
GLA_TAU = 16.0
GLA_CHUNK = 64
GLA_SUB = 16
ATT_HEADS = 6
KV_HEADS = 2
GROUP = ATT_HEADS // KV_HEADS
MOBA_BLOCK = 256
MOBA_TOPK = 3
REL_BUCKETS = 32
REL_MAX_DIST = 128
RMS_EPS = 1e-6
NEG_INF = -1e30
PAGE_SIZE = 128
D_FF = 2816
GLA_QK_W = GLA_HEADS * GLA_DK
GLA_V_W = GLA_HEADS * GLA_DV
ATT_Q_W = ATT_HEADS * HEAD_DIM
ATT_KV_W = 2 * KV_HEADS * HEAD_DIM
LANES = 128
ATT_TILE = 256
ATT_ROWS = ATT_HEADS * ATT_TILE
SB_ROW_TILE = 1536
MOBA_ROW_TILE = 128
VMEM_LIMIT = 56 * 1024 * 1024
PAGES_PER_STEP = 16
MOBA_PAGES_PER_STEP = 32

_OFF = dict(gq=0, gk=128, gv=256, gg=512, ga=768, mq=784, mk=1168, mv=1296, sq=1424, sk=1808, sv=1936)
IN_W_PAD = 2176
_HEAD_PERM = np.concatenate([np.concatenate([np.arange(64) + 64 * j, np.arange(64) + 64 * (GROUP + j)])
                             for j in range(GROUP)])


def _t5_thresholds():
    n = np.arange(0, 4 * REL_MAX_DIST, dtype=np.int64)
    max_exact = REL_BUCKETS // 2
    nf = np.maximum(n, 1).astype(np.float32)
    large = max_exact + (np.log(nf / np.float32(max_exact)) / np.float32(math.log(REL_MAX_DIST / max_exact))
                         * np.float32(REL_BUCKETS - max_exact)).astype(np.int32)
    bucket = np.where(n < max_exact, n, np.minimum(large, REL_BUCKETS - 1))
    return [int(np.argmax(bucket >= b)) for b in range(REL_BUCKETS)]


_T5_THR = _t5_thresholds()


def _dot(a, b):
    return jnp.dot(a, b, preferred_element_type=F32)


def _dot_nt(a, b):
    return lax.dot_general(a, b, (((1,), (1,)), ((), ())), preferred_element_type=F32)


def _split_hilo(a):
    hi = a.astype(BF16)
    lo = (a - hi.astype(F32)).astype(BF16)
    return hi, lo


def _dot_hilo(a, b_bf16):
    hi, lo = _split_hilo(a)
    return _dot(hi, b_bf16) + _dot(lo, b_bf16)


def _group_mean_matrix(width, group):
    r = lax.broadcasted_iota(jnp.int32, (width, width), 0) // group
    c = lax.broadcasted_iota(jnp.int32, (width, width), 1) // group
    return jnp.where(r == c, 1.0 / group, 0.0).astype(BF16)


def _neg_softplus(z):
    return -(jnp.maximum(z, 0.0) + jnp.log(1.0 + jnp.exp(-jnp.abs(z))))


def _log_sigmoid(x):
    return jnp.minimum(x, 0.0) - jnp.log(1.0 + jnp.exp(-jnp.abs(x)))


def _silu(x):
    return x / (1.0 + jnp.exp(-x))


def _head_rms(x, gain, group):
    ms = _dot_hilo(x * x, _group_mean_matrix(x.shape[1], group))
    return x * lax.rsqrt(ms + RMS_EPS) * gain


def _head_rms_lanes(x, gain):
    lane = lax.broadcasted_iota(jnp.int32, (x.shape[0], LANES), 1)
    low = lane < HEAD_DIM
    cols = []
    for c in range(x.shape[1] // LANES):
        xc = x[:, c * LANES:(c + 1) * LANES]
        x2 = xc * xc
        s_low = jnp.sum(jnp.where(low, x2, 0.0), axis=1, keepdims=True)
        s_high = jnp.sum(jnp.where(low, 0.0, x2), axis=1, keepdims=True)
        ms = jnp.where(low, s_low, s_high) * (1.0 / HEAD_DIM)
        cols.append(xc * lax.rsqrt(ms + RMS_EPS))
    return jnp.concatenate(cols, axis=1) * gain


def _inproj_kernel(x_ref, g1_ref, w_ref, wa_ref, ba_ref, mqg_ref, mkg_ref, sqg_ref, skg_ref,
                   gq_ref, gk_ref, gl_ref, gv_ref, gg_ref, *att_refs, prompt):
    x = x_ref[...]
    ms = jnp.mean(x * x, axis=-1, keepdims=True)
    h = (x * lax.rsqrt(ms + RMS_EPS) * g1_ref[0]).astype(BF16)
    p = _dot(h, w_ref[0])
    gq_ref[...] = p[:, 0:128] * (GLA_DK ** -0.5)
    gk_ref[...] = p[:, 128:256]
    gv_ref[...] = p[:, 256:512]
    gg_ref[...] = p[:, 512:768]
    alpha = _dot(p[:, 768:896].astype(BF16), wa_ref[0]) + ba_ref[0]
    gl_ref[...] = _log_sigmoid(alpha) / GLA_TAU

    def attn_group(base, qg_ref, kg_ref, refs, tiles_of_k=True):
        q = _head_rms_lanes(p[:, base:base + ATT_Q_W], qg_ref[0]) * (HEAD_DIM ** -0.5)
        refs[0][...] = q.astype(BF16)
        k = _head_rms_lanes(p[:, base + 384:base + 512], kg_ref[0])
        v = p[:, base + 512:base + 640]
        if prompt:
            _, kv32t_ref, tiles_ref, rows_ref = refs
            kt, vt = k.T, v.T
            kv32t_ref[0, 0:128, :] = kt
            kv32t_ref[0, 128:256, :] = vt
            tiled = (kt if tiles_of_k else vt).astype(BF16)
            for c in range(tiles_ref.shape[1]):
                tiles_ref[0, c] = tiled[:, c * ATT_TILE:(c + 1) * ATT_TILE]
            rows_ref[...] = (v if tiles_of_k else k).astype(BF16)
        else:
            _, kv32_ref, kv16_ref = refs
            kv32_ref[:, 0:128] = k
            kv32_ref[:, 128:256] = v
            kv16_ref[:, 0:128] = k.astype(BF16)
            kv16_ref[:, 128:256] = v.astype(BF16)
        return k

    if prompt:
        mk = attn_group(896, mqg_ref, mkg_ref, att_refs[0:4])
        attn_group(1536, sqg_ref, skg_ref, att_refs[5:9])
        mkm_ref = att_refs[4]
        for j in range(mkm_ref.shape[1]):
            mkm_ref[0, j:j + 1, :] = jnp.mean(mk[j * MOBA_BLOCK:(j + 1) * MOBA_BLOCK], axis=0, keepdims=True)
    else:
        attn_group(896, mqg_ref, mkg_ref, att_refs[0:3])
        attn_group(1536, sqg_ref, skg_ref, att_refs[3:6])


def _layer_spec(a, layer):
    return pl.BlockSpec((1,) + a.shape[1:], lambda *_: (layer, 0, 0))


def _inproj(x, wts, layer, tm, seq=None):
    n = x.shape[0]
    grid = n // tm
    row = lambda w: pl.BlockSpec((tm, w), lambda i: (i, 0))
    sds = jax.ShapeDtypeStruct
    outs = [sds((n, 128), F32), sds((n, 128), F32), sds((n, 128), F32), sds((n, 256), F32), sds((n, 256), F32)]
    out_specs = [row(128), row(128), row(128), row(256), row(256)]
    if seq is None:
        group = [(sds((n, ATT_Q_W), BF16), row(ATT_Q_W)), (sds((n, ATT_KV_W), F32), row(ATT_KV_W)),
                 (sds((n, ATT_KV_W), BF16), row(ATT_KV_W))]
        att = group + group
    else:
        tiles = seq // tm
        nblk = tm // ATT_TILE
        group = [(sds((n, ATT_Q_W), BF16), row(ATT_Q_W)),
                 (sds((n // seq, ATT_KV_W, seq), F32),
                  pl.BlockSpec((1, ATT_KV_W, tm), lambda i: (i // tiles, 0, i % tiles))),
                 (sds((n // seq, seq // ATT_TILE, 128, ATT_TILE), BF16),
                  pl.BlockSpec((1, nblk, 128, ATT_TILE), lambda i: (i // tiles, i % tiles, 0, 0))),
                 (sds((n, 128), BF16), row(128))]
        means = (sds((grid, nblk, 128), F32), pl.BlockSpec((1, nblk, 128), lambda i: (i, 0, 0)))
        att = group + [means] + group
    outs += [a for a, _ in att]
    out_specs += [b for _, b in att]
    ins = [x] + [wts[name] for name in ('g1', 'w_in', 'w_alpha', 'b_alpha', 'mqg', 'mkg', 'sqg', 'skg')]
    in_specs = [row(D_MODEL)] + [_layer_spec(a, layer) for a in ins[1:]]
    return pl.pallas_call(
        functools.partial(_inproj_kernel, prompt=seq is not None), grid=(grid,),
        in_specs=in_specs, out_specs=out_specs, out_shape=outs,
        compiler_params=pltpu.CompilerParams(dimension_semantics=("arbitrary",), vmem_limit_bytes=VMEM_LIMIT),
        name=f"inproj_{tm}")(*ins)


def _gla_kernel(q_ref, k_ref, gl_ref, v_ref, gg_ref, gn_ref, s0_ref, o_ref, sT_ref,
                st_scr, p_scr, w_scr, *, chunk, n_chunks):
    c = chunk
    step = pl.program_id(0)
    n_seq = q_ref.shape[0]

    @pl.when(step == 0)
    def _():
        st_scr[...] = s0_ref[...]

    ri = lax.broadcasted_iota(jnp.int32, (c, c), 0)
    ci = lax.broadcasted_iota(jnp.int32, (c, c), 1)
    ltri = jnp.where(ri >= ci, 1.0, 0.0).astype(BF16)
    kh = lax.broadcasted_iota(jnp.int32, (GLA_QK_W, GLA_V_W), 0) // GLA_DK
    vh = lax.broadcasted_iota(jnp.int32, (GLA_QK_W, GLA_V_W), 1) // GLA_DV
    head_ones = jnp.where(kh == vh, 1.0, 0.0).astype(BF16)
    vh2 = lax.broadcasted_iota(jnp.int32, (GLA_V_W, GLA_QK_W), 0) // GLA_DV
    kh2 = lax.broadcasted_iota(jnp.int32, (GLA_V_W, GLA_QK_W), 1) // GLA_DK
    bd_mask = vh2 == kh2
    sc = GLA_SUB
    n_sub = c // sc
    trow = lax.broadcasted_iota(jnp.int32, (sc, GLA_QK_W), 0)
    qk_head = lax.broadcasted_iota(jnp.int32, (sc, GLA_QK_W), 1) // GLA_DK
    v_head = lax.broadcasted_iota(jnp.int32, (sc, GLA_V_W), 1) // GLA_DV
    key_col = lax.broadcasted_iota(jnp.int32, (GLA_HEADS * sc, c), 1)

    def chunks(ic, carry):
        r0 = pl.multiple_of(ic * c, c)
        seqs = range(n_seq)
        ld = lambda ref: [ref[bi, pl.ds(r0, c), :] for bi in seqs]
        q, k, g, v = ld(q_ref), ld(k_ref), ld(gl_ref), ld(v_ref)
        b = []
        for bi in seqs:
            g_hi, g_lo = _split_hilo(g[bi])
            b.append(_dot(ltri, g_hi) + _dot(ltri, g_lo))
        st = [st_scr[bi] for bi in seqs]
        o_inter = [_dot_nt((q[bi] * jnp.exp(b[bi])).astype(BF16), st[bi].astype(BF16)) for bi in seqs]
        v16 = [v[bi].astype(BF16) for bi in seqs]

        s4 = {}
        for i_sub in range(1, n_sub):
            rows = slice(i_sub * sc, (i_sub + 1) * sc)
            for bi in seqs:
                e = b[bi][i_sub * sc - 1:i_sub * sc, :]
                a = q[bi][rows] * jnp.exp(b[bi][rows] - e)
                a4 = jnp.concatenate([jnp.where(qk_head == h, a, 0.0) for h in range(GLA_HEADS)], axis=0)
                kd = (k[bi] * jnp.exp(jnp.minimum(e - b[bi], 0.0))).astype(BF16)
                s4[bi, i_sub] = jnp.where(key_col < i_sub * sc, _dot_nt(a4.astype(BF16), kd), 0.0)

        for i_sub in range(n_sub):
            rows = slice(i_sub * sc, (i_sub + 1) * sc)
            for bi in seqs:
                q_i, b_i, k_i = q[bi][rows], b[bi][rows], k[bi][rows]
                for s in range(sc):
                    sl = q_i * jnp.exp(jnp.minimum(b_i - b_i[s:s + 1], 0.0)) * k_i[s:s + 1]
                    off = (i_sub * sc + s) * sc
                    p_scr[bi, off:off + sc, :] = jnp.where(trow >= s, sl, 0.0)
        for bi in seqs:
            w_scr[bi] = _dot_hilo(p_scr[bi], head_ones)

        for bi in seqs:
            b_last = b[bi][c - 1:c, :]
            kt = (k[bi] * jnp.exp(b_last - b[bi])).astype(BF16)
            upd = _dot(v[bi].T.astype(BF16), kt)
            st_scr[bi] = st[bi] * jnp.exp(b_last) + jnp.where(bd_mask, upd, 0.0)

        o4 = {key: _dot(val.astype(BF16), v16[key[0]]) for key, val in s4.items()}

        for i_sub in range(n_sub):
            rows = slice(i_sub * sc, (i_sub + 1) * sc)
            for bi in seqs:
                o_i = o_inter[bi][rows]
                for s in range(sc):
                    off = (i_sub * sc + s) * sc
                    o_i = o_i + w_scr[bi, off:off + sc, :] * v[bi][i_sub * sc + s:i_sub * sc + s + 1, :]
                if i_sub > 0:
                    for h in range(GLA_HEADS):
                        o_i = o_i + jnp.where(v_head == h, o4[bi, i_sub][h * sc:(h + 1) * sc], 0.0)
                on = _head_rms(o_i, gn_ref[0], GLA_DV)
                dst = pl.ds(r0 + i_sub * sc, sc)
                o_ref[bi, dst, :] = (on * _silu(gg_ref[bi, dst, :])).astype(o_ref.dtype)
        return carry
    lax.fori_loop(0, n_chunks, chunks, 0)

    @pl.when(step == pl.num_programs(0) - 1)
    def _():
        sT_ref[...] = st_scr[...]


def _gla(gq, gk, gl, gv, gg, gnorm, layer, s0T, *, batch, seq, chunk, chunks_per_step):
    rows = chunk * chunks_per_step
    row = lambda w: pl.BlockSpec((batch, rows, w), lambda i: (0, i, 0))
    state = pl.BlockSpec((batch, GLA_V_W, GLA_QK_W), lambda i: (0, 0, 0))
    r3 = lambda a: a.reshape(batch, seq, a.shape[-1])
    kern = functools.partial(_gla_kernel, chunk=chunk, n_chunks=chunks_per_step)
    o, st = pl.pallas_call(
        kern, grid=(seq // rows,),
        in_specs=[row(128), row(128), row(128), row(256), row(256),
                  _layer_spec(gnorm, layer), state],
        out_specs=[row(256), state],
        out_shape=[jax.ShapeDtypeStruct((batch, seq, GLA_V_W), BF16),
                   jax.ShapeDtypeStruct((batch, GLA_V_W, GLA_QK_W), F32)],
        scratch_shapes=[pltpu.VMEM((batch, GLA_V_W, GLA_QK_W), F32),
                        pltpu.VMEM((batch, chunk * GLA_SUB, GLA_QK_W), F32),
                        pltpu.VMEM((batch, chunk * GLA_SUB, GLA_V_W), F32)],
        compiler_params=pltpu.CompilerParams(dimension_semantics=("arbitrary",), vmem_limit_bytes=VMEM_LIMIT),
        name="gla_prompt")(r3(gq), r3(gk), r3(gl), r3(gv), r3(gg), gnorm, s0T)
    return o.reshape(batch * seq, GLA_V_W), st


def _gla_sample_kernel(q_ref, k_ref, gl_ref, v_ref, gg_ref, gn_ref, s0_ref, o_ref, sT_ref):
    rows = (8, GLA_QK_W)
    decay = jnp.exp(gl_ref[0])
    r = lax.broadcasted_iota(jnp.int32, (GLA_V_W, GLA_V_W), 0)
    cc = lax.broadcasted_iota(jnp.int32, (GLA_V_W, GLA_V_W), 1)
    v_diag = jnp.where(r == cc, jnp.broadcast_to(v_ref[0], (GLA_V_W, GLA_V_W)), 0.0).astype(BF16)
    k_rows = jnp.broadcast_to(k_ref[0], (GLA_V_W, GLA_QK_W)).astype(BF16)
    vh = lax.broadcasted_iota(jnp.int32, (GLA_V_W, GLA_QK_W), 0) // GLA_DV
    kh = lax.broadcasted_iota(jnp.int32, (GLA_V_W, GLA_QK_W), 1) // GLA_DK
    outer = jnp.where(vh == kh, _dot(v_diag, k_rows), 0.0)
    st = s0_ref[0] * decay + outer
    sT_ref[0] = st
    q8 = jnp.broadcast_to(q_ref[0], rows).astype(BF16)
    o = _dot_nt(q8, st.astype(BF16))
    on = _head_rms(o, gn_ref[0], GLA_DV)
    o_ref[0] = (on * _silu(gg_ref[0]))[0:1].astype(o_ref.dtype)


def _gla_sample(gq, gk, gl, gv, gg, gnorm, layer, s0T):
    n = gq.shape[0]
    r3 = lambda a: a.reshape(n, 1, a.shape[-1])
    row = lambda w: pl.BlockSpec((1, 1, w), lambda b: (b, 0, 0))
    st_spec = pl.BlockSpec((1, GLA_V_W, GLA_QK_W), lambda b: (b, 0, 0))
    o, st = pl.pallas_call(
        _gla_sample_kernel, grid=(n,),
        in_specs=[row(128), row(128), row(128), row(256), row(256),
                  _layer_spec(gnorm, layer), st_spec],
        out_specs=[row(256), st_spec],
        out_shape=[jax.ShapeDtypeStruct((n, 1, GLA_V_W), BF16), jax.ShapeDtypeStruct((n, GLA_V_W, GLA_QK_W), F32)],
        compiler_params=pltpu.CompilerParams(dimension_semantics=("arbitrary",), vmem_limit_bytes=VMEM_LIMIT),
        name="gla_sample")(r3(gq), r3(gk), r3(gl), r3(gv), r3(gg), gnorm, s0T)
    return o.reshape(n, GLA_V_W), st


def _stack_heads(q_cols):
    lane = lax.broadcasted_iota(jnp.int32, q_cols[0].shape, 1)
    low = lane < HEAD_DIM
    zero = jnp.zeros_like(q_cols[0])
    parts = [jnp.where(low, qc, zero) for qc in q_cols] + [jnp.where(low, zero, qc) for qc in q_cols]
    return jnp.concatenate(parts, axis=0)


def _unstack_heads(o, rows):
    lane = lax.broadcasted_iota(jnp.int32, (rows, LANES), 1)
    low = lane < HEAD_DIM
    return [jnp.where(low, o[j * rows:(j + 1) * rows], o[(GROUP + j) * rows:(GROUP + j + 1) * rows])
            for j in range(GROUP)]


def _suffix_matrix():
    r = lax.broadcasted_iota(jnp.int32, (ATT_TILE, ATT_TILE), 0)
    c = lax.broadcasted_iota(jnp.int32, (ATT_TILE, ATT_TILE), 1)
    u = jnp.where(r > c, 1.0, 0.0).astype(BF16)
    return jnp.concatenate([u, u], axis=0)


def _twice(a):
    return jnp.concatenate([a, a], axis=1)


def _sb_rows(q, kt, v, u2, carry, mask):
    z = _dot(q, kt)
    l = _neg_softplus(z)
    if mask is not None:
        l = jnp.where(mask, l, 0.0)
    hi, lo = _split_hilo(l)
    c = _dot(jnp.concatenate([hi, lo], axis=1), u2)
    w = jnp.exp(z + l + c + _twice(carry))
    if mask is not None:
        w = jnp.where(mask, w, 0.0)
    return _dot(w.astype(BF16), v), carry + jnp.sum(l, axis=1, keepdims=True)


SB_STOP = -104.0


def _sb_prompt_kernel(q_ref, kt_ref, v_ref, o_ref, q_scr, carry_scr, acc_scr):
    i = pl.program_id(1)
    t = ATT_TILE
    q_scr[...] = _stack_heads([q_ref[0, :, 128 * j:128 * (j + 1)] for j in range(GROUP)])
    carry_scr[...] = jnp.zeros(carry_scr.shape, F32)
    acc_scr[...] = jnp.zeros(acc_scr.shape, F32)
    u2 = _suffix_matrix()
    rt = SB_ROW_TILE
    rr = lax.broadcasted_iota(jnp.int32, (rt, t), 0) & (t - 1)
    cc = lax.broadcasted_iota(jnp.int32, (rt, t), 1)

    def tile(kj, diagonal):
        kt = kt_ref[0, kj]
        v = v_ref[0, pl.ds(pl.multiple_of(kj * t, t), t), :]
        for r in range(ATT_ROWS // rt):
            rows = slice(r * rt, (r + 1) * rt)
            mask = cc < rr + (r * rt) % t if diagonal else None
            pv, carry = _sb_rows(q_scr[rows, :], kt, v, u2, carry_scr[rows, :], mask)
            acc_scr[rows, :] += pv
            carry_scr[rows, :] = carry

    def carry_max():
        return jnp.max(jnp.max(carry_scr[...], axis=0, keepdims=True), axis=1, keepdims=True)[0, 0]

    tile(i, True)

    def cond(state):
        kj, cmax = state
        return jnp.logical_and(kj >= 0, cmax > SB_STOP)

    def body(state):
        kj, _ = state
        tile(kj, False)
        return kj - 1, carry_max()
    lax.while_loop(cond, body, (i - 1, carry_max()))

    cols = _unstack_heads(acc_scr[...], t)
    for j in range(GROUP):
        o_ref[0, :, 128 * j:128 * (j + 1)] = cols[j].astype(o_ref.dtype)


def _sb_prompt(q, kt, v, *, batch, seq):
    t = ATT_TILE
    out = pl.pallas_call(
        _sb_prompt_kernel, grid=(batch, seq // t),
        in_specs=[pl.BlockSpec((1, t, ATT_Q_W), lambda b, i: (b, i, 0)),
                  pl.BlockSpec((1, seq // t, LANES, t), lambda b, i: (b, 0, 0, 0)),
                  pl.BlockSpec((1, seq, LANES), lambda b, i: (b, 0, 0))],
        out_specs=pl.BlockSpec((1, t, ATT_Q_W), lambda b, i: (b, i, 0)),
        out_shape=jax.ShapeDtypeStruct((batch, seq, ATT_Q_W), BF16),
        scratch_shapes=[pltpu.VMEM((ATT_ROWS, LANES), BF16), pltpu.VMEM((ATT_ROWS, LANES), F32),
                        pltpu.VMEM((ATT_ROWS, LANES), F32)],
        compiler_params=pltpu.CompilerParams(dimension_semantics=("arbitrary", "arbitrary"),
                                             vmem_limit_bytes=VMEM_LIMIT),
        name="sb_prompt")(q.reshape(batch, seq, ATT_Q_W), kt, v.reshape(batch, seq, LANES))
    return out.reshape(batch * seq, ATT_Q_W)


def _bias_of_dist(dist, rb_ref, h):
    bias = jnp.full(dist.shape, rb_ref[0, h], F32)
    for b in range(1, REL_BUCKETS):
        bias = jnp.where(dist >= _T5_THR[b], rb_ref[b, h], bias)
    return bias - rb_ref[REL_BUCKETS - 1, h]


def _bias_kernel(rb_ref, own_ref, prev_ref, samp_ref):
    t = ATT_TILE
    r = lax.broadcasted_iota(jnp.int32, (t, t), 0)
    c = lax.broadcasted_iota(jnp.int32, (t, t), 1)
    for h in range(ATT_HEADS):
        own = _bias_of_dist(jnp.maximum(r - c, 0), rb_ref, h)
        own_ref[h * t:(h + 1) * t, :] = jnp.where(c <= r, own, NEG_INF)
        prev_ref[h * t:(h + 1) * t, :] = _bias_of_dist(r - c + t, rb_ref, h)
    s = lax.broadcasted_iota(jnp.int32, (8, t), 1)
    hrow = lax.broadcasted_iota(jnp.int32, (8, t), 0)
    last = jnp.zeros((8, t), F32)
    self_b = jnp.zeros((8, LANES), F32)
    hrow2 = lax.broadcasted_iota(jnp.int32, (8, LANES), 0)
    for h in range(ATT_HEADS):
        last = jnp.where(hrow == h, _bias_of_dist(t - s, rb_ref, h), last)
        self_b = jnp.where(hrow2 == h, rb_ref[0, h] - rb_ref[REL_BUCKETS - 1, h], self_b)
    samp_ref[:, 0:t] = last
    samp_ref[:, t:t + LANES] = self_b


def _bias_tiles(rel_bias):
    t = ATT_TILE
    return pl.pallas_call(
        _bias_kernel,
        in_specs=[pl.BlockSpec(memory_space=pltpu.SMEM)],
        out_shape=[jax.ShapeDtypeStruct((ATT_ROWS, t), F32), jax.ShapeDtypeStruct((ATT_ROWS, t), F32),
                   jax.ShapeDtypeStruct((8, t + LANES), F32)],
        name="t5_bias_tiles")(rel_bias)


def _top3_select(gate, n_valid):
    blk_i = lax.broadcasted_iota(jnp.int32, gate.shape, 0)
    blk = blk_i.astype(F32)
    valid = blk_i < n_valid
    g = jnp.where(valid, gate, NEG_INF)
    sel = jnp.zeros(gate.shape, F32)
    for _ in range(MOBA_TOPK):
        m = jnp.max(g, axis=0, keepdims=True)
        idx = jnp.min(jnp.where(g == m, blk, float(LANES)), axis=0, keepdims=True)
        pick = blk == idx
        sel = jnp.where(pick, 1.0, sel)
        g = jnp.where(pick, -jnp.inf, g)
    return jnp.where(valid, sel, 0.0) > 0.5


def _moba_prompt_kernel(q_ref, kt_ref, v_ref, km_ref, own_ref, prev_ref, o_ref,
                        qx_scr, p_scr, al_scr, m_scr, acc_scr):
    i = pl.program_id(1)
    t = ATT_TILE
    q = _stack_heads([q_ref[0, :, 128 * j:128 * (j + 1)] for j in range(GROUP)])
    n_blk = v_ref.shape[1] // t
    gate = _dot(km_ref[0, 0:n_blk, :].astype(BF16), q.astype(F32).T.astype(BF16))
    mask_t = jnp.where(_top3_select(gate, i), 0.0, NEG_INF)
    mask_t = jnp.concatenate([mask_t, jnp.full((LANES - n_blk, ATT_ROWS), NEG_INF, F32)], axis=0)
    qx_scr[:, 0:LANES] = q
    qx_scr[:, LANES:2 * LANES] = mask_t.T.astype(BF16)
    blk_row = lax.broadcasted_iota(jnp.int32, (LANES, t), 0)
    ones = jnp.ones((t, LANES), BF16)
    rt = MOBA_ROW_TILE
    row_tiles = [slice(r * rt, (r + 1) * rt) for r in range(ATT_ROWS // rt)]

    def past_keys(kj):
        blk = jnp.where(kj >= 0, kj, LANES - 1)
        return jnp.concatenate([kt_ref[0, jnp.maximum(kj, 0)],
                                jnp.where(blk_row == blk, 1.0, 0.0).astype(BF16)], axis=0)

    def values(kj):
        v = v_ref[0, pl.ds(pl.multiple_of(jnp.maximum(kj, 0) * t, t), t), :]
        return jnp.concatenate([v, ones], axis=1)

    vx = values(i)
    for rows in row_tiles:
        s = _dot(qx_scr[rows, 0:LANES], kt_ref[0, i]) + own_ref[rows, :]
        m0 = jnp.broadcast_to(jnp.max(s, axis=1, keepdims=True), (rt, LANES))
        acc_scr[rows, :] = _dot(jnp.exp(s - _twice(m0)).astype(BF16), vx)
        m_scr[rows, :] = m0

    def pair_keys(kj):
        return jnp.concatenate([past_keys(kj), past_keys(kj - 1)], axis=1)

    def pair_values(kj):
        return jnp.concatenate([values(kj), values(kj - 1)], axis=0)

    def probabilities(kx, rows, with_prev_bias):
        s = _dot(qx_scr[rows, :], kx)
        if with_prev_bias:
            s = jnp.concatenate([s[:, 0:t] + prev_ref[rows, :], s[:, t:2 * t]], axis=1)
        m_old = m_scr[rows, :]
        m_new = jnp.maximum(m_old, jnp.broadcast_to(jnp.max(s, axis=1, keepdims=True), (rt, LANES)))
        p_scr[rows, :] = jnp.exp(s - _twice(_twice(m_new))).astype(BF16)
        al_scr[rows, :] = jnp.exp(m_old - m_new)
        m_scr[rows, :] = m_new

    def accumulate(vx, rows):
        acc_scr[rows, :] = _twice(al_scr[rows, :]) * acc_scr[rows, :] + _dot(p_scr[rows, :], vx)

    @pl.when(i >= 1)
    def _():
        kx = pair_keys(i - 1)
        for rows in row_tiles:
            probabilities(kx, rows, True)

    n_pairs = jnp.maximum(i - 1, 0) // 2

    def body(n, carry):
        kj = i - 3 - 2 * n
        vx, kx = pair_values(kj + 2), pair_keys(kj)
        for rows in row_tiles:
            accumulate(vx, rows)
            probabilities(kx, rows, False)
        return carry
    lax.fori_loop(0, n_pairs, body, 0)

    @pl.when(i >= 1)
    def _():
        vx = pair_values(i - 1 - 2 * n_pairs)
        for rows in row_tiles:
            accumulate(vx, rows)

    o = acc_scr[:, 0:LANES] / acc_scr[:, LANES:2 * LANES]
    cols = _unstack_heads(o, t)
    for j in range(GROUP):
        o_ref[0, :, 128 * j:128 * (j + 1)] = cols[j].astype(o_ref.dtype)


def _moba_prompt(q, kt, v, kmean, bias_own, bias_prev, *, batch, seq):
    t = ATT_TILE
    const = lambda a: pl.BlockSpec(a.shape, lambda b, i: (0, 0))
    out = pl.pallas_call(
        _moba_prompt_kernel, grid=(batch, seq // t),
        in_specs=[pl.BlockSpec((1, t, ATT_Q_W), lambda b, i: (b, i, 0)),
                  pl.BlockSpec((1, seq // t, LANES, t), lambda b, i: (b, 0, 0, 0)),
                  pl.BlockSpec((1, seq, LANES), lambda b, i: (b, 0, 0)),
                  pl.BlockSpec((1, LANES, LANES), lambda b, i: (b, 0, 0)),
                  const(bias_own), const(bias_prev)],
        out_specs=pl.BlockSpec((1, t, ATT_Q_W), lambda b, i: (b, i, 0)),
        out_shape=jax.ShapeDtypeStruct((batch, seq, ATT_Q_W), BF16),
        scratch_shapes=[pltpu.VMEM((ATT_ROWS, 2 * LANES), BF16), pltpu.VMEM((ATT_ROWS, 2 * t), BF16),
                        pltpu.VMEM((ATT_ROWS, LANES), F32), pltpu.VMEM((ATT_ROWS, LANES), F32),
                        pltpu.VMEM((ATT_ROWS, 2 * LANES), F32)],
        compiler_params=pltpu.CompilerParams(dimension_semantics=("arbitrary", "arbitrary"),
                                             vmem_limit_bytes=VMEM_LIMIT),
        name="moba_prompt")(q.reshape(batch, seq, ATT_Q_W), kt, v.reshape(batch, seq, LANES), kmean,
                            bias_own, bias_prev)
    return out.reshape(batch * seq, ATT_Q_W)


def _sample_q_rows(q_row):
    row = lax.broadcasted_iota(jnp.int32, (8, LANES), 0)
    lane = lax.broadcasted_iota(jnp.int32, (8, LANES), 1)
    qf = q_row.astype(F32)
    out = jnp.zeros((8, LANES), F32)
    for h in range(ATT_HEADS):
        j, n = h % GROUP, h // GROUP
        col = jnp.broadcast_to(qf[:, 128 * j:128 * (j + 1)], (8, LANES))
        half = lane >= HEAD_DIM if n == 1 else lane < HEAD_DIM
        out = jnp.where(jnp.logical_and(row == h, half), col, out)
    return out.astype(BF16)


def _sample_o_row(o):
    lane = lax.broadcasted_iota(jnp.int32, (1, LANES), 1)
    return [jnp.where(lane < HEAD_DIM, o[j:j + 1], o[GROUP + j:GROUP + j + 1]) for j in range(GROUP)]


TILES_PER_STEP = PAGES_PER_STEP * PAGE_SIZE // ATT_TILE


def _step_keys_values(pages):
    kt = jnp.concatenate([p[0, 0:128, :] for p in pages], axis=1).astype(BF16)
    vt = jnp.concatenate([p[0, 128:256, :] for p in pages], axis=1).astype(BF16)
    return kt, vt


def _tiles_to_rows(a):
    return jnp.concatenate([a[:, c * ATT_TILE:(c + 1) * ATT_TILE] for c in range(a.shape[1] // ATT_TILE)], axis=0)


def _rows_to_tiles(a):
    return jnp.concatenate([a[8 * c:8 * (c + 1), :] for c in range(a.shape[0] // 8)], axis=1)


def _sb_sample_kernel(pt_ref, q_ref, *refs):
    del pt_ref
    pages = refs[:PAGES_PER_STEP]
    o_ref, carry_ref, carry_scr, acc_scr = refs[PAGES_PER_STEP:]
    step = pl.program_id(1)

    @pl.when(step == 0)
    def _():
        carry_scr[...] = jnp.zeros(carry_scr.shape, F32)
        acc_scr[...] = jnp.zeros(acc_scr.shape, F32)

    q = _sample_q_rows(q_ref[0])
    kt, vt = _step_keys_values(pages)
    z = _tiles_to_rows(_dot(q, kt))
    l = _neg_softplus(z)
    hi, lo = _split_hilo(l)
    c = _dot(jnp.concatenate([hi, lo], axis=1), _suffix_matrix())
    tile_sum = jnp.sum(l, axis=1, keepdims=True)
    carry = carry_scr[...]
    carries = [None] * TILES_PER_STEP
    for tl in reversed(range(TILES_PER_STEP)):
        carries[tl] = carry
        carry = carry + tile_sum[8 * tl:8 * (tl + 1)]
    w = jnp.exp(z + l + c + _twice(jnp.concatenate(carries, axis=0)))
    acc = acc_scr[...] + _dot_nt(_rows_to_tiles(w).astype(BF16), vt)
    carry_scr[...] = carry
    acc_scr[...] = acc

    @pl.when(step == pl.num_programs(1) - 1)
    def _():
        cols = _sample_o_row(acc)
        for j in range(GROUP):
            o_ref[0, :, 128 * j:128 * (j + 1)] = cols[j].astype(o_ref.dtype)
        carry_ref[0] = carry


def _page_specs(layer, n_pool, n_steps, reverse, pages_per_step):
    specs = []
    for p in range(pages_per_step):
        def imap(b, c, pt, p=p):
            cc = (n_steps - 1 - c) if reverse else c
            return (layer * n_pool + pt[b, cc * pages_per_step + p], 0, 0)
        specs.append(pl.BlockSpec((1, ATT_KV_W, PAGE_SIZE), imap))
    return specs


def _cache_pages(cache):
    d, n_pool = cache.shape[:2]
    return cache.transpose(0, 1, 3, 4, 5, 2).reshape(d * n_pool, ATT_KV_W, PAGE_SIZE)


def _sb_sample(q, cache, page_table, layer):
    nseq, n_pages = page_table.shape
    n_pool = cache.shape[1]
    n_steps = n_pages // PAGES_PER_STEP
    cache2 = _cache_pages(cache)
    q3 = q.reshape(nseq, 1, ATT_Q_W)

    def walk(steps):
        grid_spec = pltpu.PrefetchScalarGridSpec(
            num_scalar_prefetch=1, grid=(nseq, steps),
            in_specs=[pl.BlockSpec((1, 1, ATT_Q_W), lambda b, c, pt: (b, 0, 0))]
            + _page_specs(layer, n_pool, n_steps, True, PAGES_PER_STEP),
            out_specs=[pl.BlockSpec((1, 1, ATT_Q_W), lambda b, c, pt: (b, 0, 0)),
                       pl.BlockSpec((1, 8, LANES), lambda b, c, pt: (b, 0, 0))],
            scratch_shapes=[pltpu.VMEM((8, LANES), F32), pltpu.VMEM((8, LANES), F32)])
        return pl.pallas_call(
            _sb_sample_kernel, grid_spec=grid_spec,
            out_shape=[jax.ShapeDtypeStruct((nseq, 1, ATT_Q_W), BF16),
                       jax.ShapeDtypeStruct((nseq, 8, LANES), F32)],
            compiler_params=pltpu.CompilerParams(dimension_semantics=("arbitrary", "arbitrary"),
                                                 vmem_limit_bytes=VMEM_LIMIT),
            name=f"sb_sample_{steps}")(page_table, q3, *([cache2] * PAGES_PER_STEP))

    out, carry = walk(1)
    out = lax.cond(jnp.max(carry) < SB_STOP, lambda: out, lambda: walk(n_steps)[0])
    return out.reshape(nseq, ATT_Q_W)


def _moba_sample_kernel(pt_ref, q_ref, kvn_ref, sb_ref, *refs, n_blocks):
    del pt_ref
    pages = refs[:MOBA_PAGES_PER_STEP]
    o_ref, gate_scr, m_scr, l_scr, acc_scr = refs[MOBA_PAGES_PER_STEP:]
    step = pl.program_id(1)
    bps = MOBA_PAGES_PER_STEP * PAGE_SIZE // MOBA_BLOCK
    rows = bps * 8

    q = _sample_q_rows(q_ref[0])
    kt, vt = _step_keys_values(pages)
    s = _tiles_to_rows(_dot(q, kt))
    g = jnp.mean(s, axis=1, keepdims=True)
    row_blk = lax.broadcasted_iota(jnp.int32, (rows, ATT_TILE), 0) // 8
    last = jnp.logical_and(step == pl.num_programs(1) - 1, row_blk == bps - 1)
    s = s + jnp.where(last, jnp.concatenate([sb_ref[:, 0:ATT_TILE]] * bps, axis=0), 0.0)
    m = jnp.max(s, axis=1, keepdims=True)
    p = jnp.exp(s - m)
    p_bd = jnp.concatenate([jnp.where(row_blk == c, p, 0.0) for c in range(bps)], axis=1).astype(BF16)
    dst = pl.ds(step * bps, bps)
    wide = lambda a: jnp.broadcast_to(a, (rows, LANES)).reshape(bps, 8, LANES)
    gate_scr[dst] = wide(g)
    m_scr[dst] = wide(m)
    l_scr[dst] = wide(jnp.sum(p, axis=1, keepdims=True))
    acc_scr[dst] = _dot_nt(p_bd, vt).reshape(bps, 8, LANES)

    @pl.when(step == pl.num_programs(1) - 1)
    def _():
        gates = [gate_scr[j] for j in range(n_blocks)]
        sel = [jnp.zeros((8, LANES), jnp.bool_)] * n_blocks
        for _ in range(MOBA_TOPK):
            best = functools.reduce(jnp.maximum, gates)
            idx = functools.reduce(jnp.minimum, [jnp.where(gates[j] == best, float(j), float(n_blocks))
                                                 for j in range(n_blocks)])
            for j in range(n_blocks):
                pick = idx == float(j)
                sel[j] = jnp.logical_or(sel[j], pick)
                gates[j] = jnp.where(pick, -jnp.inf, gates[j])
        kvn = kvn_ref[0]
        s_self = (jnp.sum(q.astype(F32) * kvn[:, 0:128].astype(F32), axis=1, keepdims=True)
                  + sb_ref[:, ATT_TILE:ATT_TILE + 1])
        m_tot = jnp.maximum(functools.reduce(jnp.maximum, [jnp.where(sel[j], m_scr[j], NEG_INF)
                                                           for j in range(n_blocks)]), s_self)
        p_self = jnp.exp(s_self - m_tot)
        denom = p_self
        o = p_self * kvn[:, 128:256].astype(F32)
        for j in range(n_blocks):
            coef = jnp.where(sel[j], jnp.exp(m_scr[j] - m_tot), 0.0)
            denom = denom + coef * l_scr[j]
            o = o + coef * acc_scr[j]
        cols = _sample_o_row(o / denom)
        for j in range(GROUP):
            o_ref[0, :, 128 * j:128 * (j + 1)] = cols[j].astype(o_ref.dtype)


def _moba_sample(q, kv_new, cache, page_table, bias_samp, layer):
    nseq, n_pages = page_table.shape
    n_pool = cache.shape[1]
    n_steps = n_pages // MOBA_PAGES_PER_STEP
    n_blocks = n_pages * PAGE_SIZE // MOBA_BLOCK
    cache2 = _cache_pages(cache)
    grid_spec = pltpu.PrefetchScalarGridSpec(
        num_scalar_prefetch=1, grid=(nseq, n_steps),
        in_specs=[pl.BlockSpec((1, 1, ATT_Q_W), lambda b, c, pt: (b, 0, 0)),
                  pl.BlockSpec((1, 1, ATT_KV_W), lambda b, c, pt: (b, 0, 0)),
                  pl.BlockSpec(bias_samp.shape, lambda b, c, pt: (0, 0))]
        + _page_specs(layer, n_pool, n_steps, False, MOBA_PAGES_PER_STEP),
        out_specs=pl.BlockSpec((1, 1, ATT_Q_W), lambda b, c, pt: (b, 0, 0)),
        scratch_shapes=[pltpu.VMEM((n_blocks, 8, LANES), F32)] * 4)
    out = pl.pallas_call(
        functools.partial(_moba_sample_kernel, n_blocks=n_blocks), grid_spec=grid_spec,
        out_shape=jax.ShapeDtypeStruct((nseq, 1, ATT_Q_W), BF16),
        compiler_params=pltpu.CompilerParams(dimension_semantics=("arbitrary", "arbitrary"),
                                             vmem_limit_bytes=VMEM_LIMIT),
        name="moba_sample")(page_table, q.reshape(nseq, 1, ATT_Q_W), kv_new.reshape(nseq, 1, ATT_KV_W),
                            bias_samp, *([cache2] * MOBA_PAGES_PER_STEP))
    return out.reshape(nseq, ATT_Q_W)


FF_CHUNK = 1408


def _ffn_kernel(x_ref, og_ref, om_ref, os_ref, wog_ref, wom_ref, wos_ref, g2_ref, wg_ref, wu_ref, wd_ref, y_ref,
                h2_scr):
    @pl.when(pl.program_id(1) == 0)
    def _():
        x1 = (x_ref[...] + _dot(og_ref[...], wog_ref[0]) + _dot(om_ref[...], wom_ref[0])
              + _dot(os_ref[...], wos_ref[0]))
        ms = jnp.mean(x1 * x1, axis=-1, keepdims=True)
        h2_scr[...] = (x1 * lax.rsqrt(ms + RMS_EPS) * g2_ref[0]).astype(BF16)
        y_ref[...] = x1

    h2 = h2_scr[...]
    a = _silu(_dot(h2, wg_ref[0])) * _dot(h2, wu_ref[0])
    y_ref[...] += _dot(a.astype(BF16), wd_ref[0])


def _ffn(x, og, om, osb, wts, layer, tm):
    n = x.shape[0]
    row = lambda w: pl.BlockSpec((tm, w), lambda i, f: (i, 0))
    ws = [wts[name] for name in ('wo_g', 'wo_m', 'wo_s', 'g2', 'w_gate', 'w_up', 'w_down')]
    w_specs = [_layer_spec(w, layer) for w in ws[:4]] + [
        pl.BlockSpec((1, D_MODEL, FF_CHUNK), lambda i, f: (layer, 0, f)),
        pl.BlockSpec((1, D_MODEL, FF_CHUNK), lambda i, f: (layer, 0, f)),
        pl.BlockSpec((1, FF_CHUNK, D_MODEL), lambda i, f: (layer, f, 0))]
    return pl.pallas_call(
        _ffn_kernel, grid=(n // tm, D_FF // FF_CHUNK),
        in_specs=[row(D_MODEL), row(GLA_V_W), row(ATT_Q_W), row(ATT_Q_W)] + w_specs,
        out_specs=row(D_MODEL), out_shape=jax.ShapeDtypeStruct((n, D_MODEL), F32),
        scratch_shapes=[pltpu.VMEM((tm, D_MODEL), BF16)],
        compiler_params=pltpu.CompilerParams(dimension_semantics=("arbitrary", "arbitrary"),
                                             vmem_limit_bytes=VMEM_LIMIT),
        name=f"ffn_{tm}")(x, og, om, osb, *ws)


def _prep_weights(norm1, w_in, w_alpha, b_alpha, gla_norm, moba_q_norm, moba_k_norm, sb_q_norm, sb_k_norm,
                  w_out, norm2, w_gate_up, w_down):
    o = _OFF
    cols = [w_in[:, :, o['gq']:o['ga']],
            jnp.pad(w_in[:, :, o['ga']:o['mq']], ((0, 0), (0, 0), (0, LANES - GLA_RANK))),
            w_in[:, :, o['mq']:o['mk']][:, :, _HEAD_PERM], w_in[:, :, o['mk']:o['sq']],
            w_in[:, :, o['sq']:o['sk']][:, :, _HEAD_PERM], w_in[:, :, o['sk']:]]
    row = lambda a: a[:, None, :]
    tile = lambda g, reps: jnp.tile(g, (1, reps))[:, None, :]
    return dict(
        g1=row(norm1), w_in=jnp.concatenate(cols, axis=2).astype(BF16),
        w_alpha=jnp.pad(w_alpha, ((0, 0), (0, LANES - GLA_RANK), (0, 0))).astype(BF16),
        b_alpha=row(b_alpha),
        gnorm=tile(gla_norm, GLA_HEADS), mqg=tile(moba_q_norm, ATT_HEADS), mkg=tile(moba_k_norm, KV_HEADS),
        sqg=tile(sb_q_norm, ATT_HEADS), skg=tile(sb_k_norm, KV_HEADS),
        wo_g=w_out[:, 0:256].astype(BF16), wo_m=w_out[:, 256:640][:, _HEAD_PERM].astype(BF16),
        wo_s=w_out[:, 640:1024][:, _HEAD_PERM].astype(BF16), g2=row(norm2),
        w_gate=w_gate_up[:, :, :D_FF].astype(BF16), w_up=w_gate_up[:, :, D_FF:].astype(BF16),
        w_down=w_down.astype(BF16))


def _state_to_blockdiag_T(s):
    b = s.shape[0]
    eye = jnp.eye(GLA_HEADS, dtype=s.dtype)
    return jnp.einsum('bhkv,hg->bhvgk', s, eye).reshape(b, GLA_V_W, GLA_QK_W)


def _blockdiag_T_to_state(st):
    b = st.shape[0]
    s5 = st.reshape(b, GLA_HEADS, GLA_DV, GLA_HEADS, GLA_DK)
    return jnp.stack([s5[:, h, :, h, :] for h in range(GLA_HEADS)], axis=1).transpose(0, 1, 3, 2)


def kernel(x_prompt, x_sample, cache_moba_kv, cache_sb_kv, state_gla, page_table, rel_bias, norm1, w_in, w_alpha,
           b_alpha, gla_norm, moba_q_norm, moba_k_norm, sb_q_norm, sb_k_norm, w_out, norm2, w_gate_up, w_down):
    nb, seq, _ = x_prompt.shape
    ns = x_sample.shape[0]
    n_prompt = nb * seq
    tm = 512
    assert seq % (2 * MOBA_BLOCK) == 0 and x_sample.shape[1] == 1 and seq // MOBA_BLOCK < LANES
    bias_own, bias_prev, bias_samp = _bias_tiles(rel_bias)
    xp = x_prompt.reshape(n_prompt, D_MODEL)
    xs = x_sample.reshape(ns, D_MODEL)
    zero_state = jnp.zeros((nb, GLA_V_W, GLA_QK_W), F32)
    outs = dict(pm=[], ps=[], pg=[], sm=[], ss=[], sg=[])
    wts = _prep_weights(norm1, w_in, w_alpha, b_alpha, gla_norm, moba_q_norm, moba_k_norm, sb_q_norm,
                        sb_k_norm, w_out, norm2, w_gate_up, w_down)
    for l in range(DEPTH):
        gq, gk, gl, gv, gg, mq, mkv32, mkt, mv, mkm, sq, skv32, skt, sv = _inproj(xp, wts, l, tm, seq=seq)
        og, st = _gla(gq, gk, gl, gv, gg, wts['gnorm'], l, zero_state, batch=nb, seq=seq, chunk=GLA_CHUNK,
                      chunks_per_step=8)
        osb = _sb_prompt(sq, skt, sv, batch=nb, seq=seq)
        kmean = jnp.pad(mkm.reshape(nb, seq // MOBA_BLOCK, LANES), ((0, 0), (0, LANES - seq // MOBA_BLOCK), (0, 0)))
        om = _moba_prompt(mq, mkt, mv, kmean, bias_own, bias_prev, batch=nb, seq=seq)
        xp = _ffn(xp, og, om, osb, wts, l, tm)
        leaf = lambda a: a.reshape(nb, 2, KV_HEADS, HEAD_DIM, seq).transpose(0, 4, 1, 2, 3)
        outs['pm'].append(leaf(mkv32))
        outs['ps'].append(leaf(skv32))
        outs['pg'].append(_blockdiag_T_to_state(st))
        gq, gk, gl, gv, gg, mq, mkv32, mkv16, sq, skv32, _ = _inproj(xs, wts, l, ns)
        og, st = _gla_sample(gq, gk, gl, gv, gg, wts['gnorm'], l, _state_to_blockdiag_T(state_gla[l]))
        osb = _sb_sample(sq, cache_sb_kv, page_table, l)
        om = _moba_sample(mq, mkv16, cache_moba_kv, page_table, bias_samp, l)
        xs = _ffn(xs, og, om, osb, wts, l, ns)
        outs['sm'].append(mkv32.reshape(ns, 1, 2, KV_HEADS, HEAD_DIM))
        outs['ss'].append(skv32.reshape(ns, 1, 2, KV_HEADS, HEAD_DIM))
        outs['sg'].append(_blockdiag_T_to_state(st))
    return (xp.reshape(nb, seq, D_MODEL), xs.reshape(ns, 1, D_MODEL), jnp.stack(outs['pm']), jnp.stack(outs['ps']),
            jnp.stack(outs['pg']), jnp.stack(outs['sm']), jnp.stack(outs['ss']), jnp.stack(outs['sg']))
```
